```python
import math
import jax, jax.numpy as jnp
from jax import lax
import numpy as np

D_MODEL = 2048
BATCH = 4
SEQ = 2048
DEPTH = 4
DEC_BATCH = 128
DEC_SEQ = 8
PAST_LEN = 16384
PAGE_SIZE = 128

N_MIXERS = 2
N_A_LAYERS = (DEPTH + 1) // N_MIXERS
N_B_LAYERS = DEPTH // N_MIXERS
HEAD_DIM = 128
MIX_W = D_MODEL
X_HEADS = 4
X_W = X_HEADS * HEAD_DIM
N_MEM = 256
DELTA_HEADS = (MIX_W - X_W) // HEAD_DIM
DELTA_W = DELTA_HEADS * HEAD_DIM
CONV_W = 4
DELTA_CHUNK = 64
GMLP_GROUPS = DELTA_HEADS
GMLP_GD = HEAD_DIM
GMLP_W = GMLP_GROUPS * GMLP_GD
GMLP_CHUNK = 128
FFN_DIM = 5632
IN_A = 4 * DELTA_W + 2 * DELTA_HEADS + X_W
IN_B = 2 * GMLP_W + X_W
EPS = 1e-6

kernel_name = 'hybrid_deltanet_gmlp_memory_decoder_step'


def _rmsnorm(x, gain):
    xf = x.astype(jnp.float32)
    y = xf * lax.rsqrt(jnp.mean(xf * xf, axis=-1, keepdims=True) + EPS)
    return (y * gain.astype(jnp.float32)).astype(x.dtype)


def _l2norm(x):
    xf = x.astype(jnp.float32)
    return (xf * lax.rsqrt(jnp.sum(xf * xf, axis=-1, keepdims=True) + EPS)).astype(x.dtype)


def _layernorm(x, gain, bias):
    xf = x.astype(jnp.float32)
    mu = jnp.mean(xf, axis=-1, keepdims=True)
    xc = xf - mu
    var = jnp.mean(xc * xc, axis=-1, keepdims=True)
    y = xc * lax.rsqrt(var + EPS) * gain.astype(jnp.float32) + bias.astype(jnp.float32)
    return y.astype(x.dtype)


def _swiglu(x, w_gu, w_dn):
    gate, up = jnp.split(x @ w_gu, 2, axis=-1)
    return (jax.nn.silu(gate) * up) @ w_dn


def _short_conv(x, buf, w):
    T = x.shape[1]
    xp = jnp.concatenate([buf.astype(x.dtype), x], axis=1)
    y = xp[:, 0:T] * w[0]
    for j in range(1, CONV_W):
        y = y + xp[:, j:j + T] * w[j]
    return jax.nn.silu(y), xp[:, -(CONV_W - 1):]


def _gated_delta(q, k, v, beta, g, s0):
    Bn, T, H, D = q.shape
    C = min(DELTA_CHUNK, T)
    n = -(-T // C)
    pad = n * C - T

    def prep(a):
        a = a.astype(jnp.float32)
        a = jnp.pad(a, [(0, 0), (0, pad)] + [(0, 0)] * (a.ndim - 2))
        a = jnp.moveaxis(a, 2, 1)
        return a.reshape((Bn, H, n, C) + a.shape[3:])

    q, k, v, beta, g = prep(q), prep(k), prep(v), prep(beta), prep(g)
    gc = jnp.cumsum(g, axis=-1)
    diff = gc[..., :, None] - gc[..., None, :]
    strict = jnp.tril(jnp.ones((C, C), bool), -1)
    causal = jnp.tril(jnp.ones((C, C), bool))
    kk = jnp.einsum('bhnid,bhnjd->bhnij', k, k)
    lmat = jnp.where(strict, beta[..., :, None] * kk * jnp.exp(jnp.where(strict, diff, 0.0)), 0.0)
    eye = jnp.eye(C, dtype=jnp.float32)
    rhs = jnp.concatenate([v * beta[..., None], k * (beta * jnp.exp(gc))[..., None]], axis=-1)
    sol = lax.linalg.triangular_solve(lmat + eye, rhs, left_side=True, lower=True)
    u, w = sol[..., :D], sol[..., D:]
    qk = jnp.einsum('bhnid,bhnjd->bhnij', q, k)
    a_intra = jnp.where(causal, qk * jnp.exp(jnp.where(causal, diff, 0.0)), 0.0)
    q_dec = q * jnp.exp(gc)[..., None]
    k_dec = k * jnp.exp(gc[..., -1:] - gc)[..., None]
    g_last = jnp.exp(gc[..., -1])

    def step(s, inp):
        q_c, k_c, u_c, w_c, a_c, gl = inp
        v_new = u_c - jnp.einsum('bhcd,bhde->bhce', w_c, s)
        o = jnp.einsum('bhcd,bhde->bhce', q_c, s) + jnp.einsum('bhij,bhje->bhie', a_c, v_new)
        s = s * gl[..., None, None] + jnp.einsum('bhcd,bhce->bhde', k_c, v_new)
        return s, o

    xs = (jnp.moveaxis(q_dec, 2, 0), jnp.moveaxis(k_dec, 2, 0), jnp.moveaxis(u, 2, 0),
          jnp.moveaxis(w, 2, 0), jnp.moveaxis(a_intra, 2, 0), jnp.moveaxis(g_last, 2, 0))
    s_fin, o = lax.scan(step, s0.astype(jnp.float32), xs)
    o = jnp.moveaxis(o, 0, 2).reshape(Bn, H, n * C, D)[:, :, :T]
    return jnp.moveaxis(o, 1, 2), s_fin


def _delta_mixer(p_mix, conv_buf, s0, conv_w, a_log, dt_bias, out_gain):
    Bn, T, _ = p_mix.shape
    qkv, z, a, b = jnp.split(p_mix, [3 * DELTA_W, 4 * DELTA_W, 4 * DELTA_W + DELTA_HEADS], axis=-1)
    qkv, new_buf = _short_conv(qkv, conv_buf, conv_w)
    q, k, v = [t.reshape(Bn, T, DELTA_HEADS, HEAD_DIM) for t in jnp.split(qkv, 3, axis=-1)]
    q = _l2norm(q) * (HEAD_DIM ** -0.5)
    k = _l2norm(k)
    beta = jax.nn.sigmoid(b.astype(jnp.float32))
    g = -jnp.exp(a_log.astype(jnp.float32)) * jax.nn.softplus(a.astype(jnp.float32) + dt_bias.astype(jnp.float32))
    o, s_new = _gated_delta(q, k, v, beta, g, s0)
    o = _rmsnorm(o, out_gain) * jax.nn.silu(z.reshape(Bn, T, DELTA_HEADS, HEAD_DIM).astype(jnp.float32))
    return o.reshape(Bn, T, DELTA_W).astype(p_mix.dtype), new_buf, s_new


def _gmlp_mixer(p_mix, ln_gain, ln_bias, w_s, b_s):
    Bn, T, _ = p_mix.shape
    u, v = jnp.split(jax.nn.gelu(p_mix, approximate=False), 2, axis=-1)
    v = _layernorm(v, ln_gain, ln_bias)
    C = min(GMLP_CHUNK, T)
    n = T // C
    w = jnp.where(jnp.tril(jnp.ones((C, C), bool)), w_s[:, :C, :C], 0.0)
    vg = v.reshape(Bn, n, C, GMLP_GROUPS, GMLP_GD)
    mixed = jnp.einsum('gij,bnjgc->bnigc', w, vg) + b_s[:, :C].T[None, None, :, :, None]
    return u * mixed.reshape(Bn, T, GMLP_W), v


def _mem_kv(mem, gain, w_kv):
    Bn, N, _ = mem.shape
    k, v = jnp.split(_rmsnorm(mem, gain) @ w_kv, 2, axis=-1)
    return k.reshape(Bn, N, X_HEADS, HEAD_DIM), v.reshape(Bn, N, X_HEADS, HEAD_DIM)


def _mem_attn(q, mem_k, mem_v):
    Bn, T, _ = q.shape
    q = q.reshape(Bn, T, X_HEADS, HEAD_DIM)
    s = jnp.einsum('bthd,bnhd->bhtn', q, mem_k).astype(jnp.float32) * (HEAD_DIM ** -0.5)
    p = jax.nn.softmax(s, axis=-1).astype(mem_v.dtype)
    return jnp.einsum('bhtn,bnhd->bthd', p, mem_v).reshape(Bn, T, X_W)


def _trunk(x, mem_ks, mem_vs, conv_bufs, delta_states, p):
    new_conv, new_delta, new_v = [], [], []
    for i in range(DEPTH):
        g = p['norm_gains'][i]
        j = i // N_MIXERS
        x = x + 0.5 * _rmsnorm(_swiglu(_rmsnorm(x, g[0]), p['w_ffn_gu'][i, 0], p['w_ffn_dn'][i, 0]), g[1])
        h = _rmsnorm(x, g[2])
        if i % N_MIXERS == 0:
            proj = h @ p['w_in_a'][j]
            mix, buf, s = _delta_mixer(proj[..., :-X_W], conv_bufs[j], delta_states[j], p['conv_w'][j],
                                       p['a_log'][j], p['dt_bias'][j], p['delta_norm_gain'][j])
            new_conv.append(buf)
            new_delta.append(s)
        else:
            proj = h @ p['w_in_b'][j]
            mix, v_rows = _gmlp_mixer(proj[..., :-X_W], p['gmlp_ln_gain'][j], p['gmlp_ln_bias'][j],
                                      p['w_spatial'][j], p['b_spatial'][j])
            new_v.append(v_rows)
        mem_out = _mem_attn(proj[..., -X_W:], mem_ks[i], mem_vs[i])
        x = x + _rmsnorm(jnp.concatenate([mix, mem_out], axis=-1) @ p['w_out'][i], g[3])
        x = x + 0.5 * _rmsnorm(_swiglu(_rmsnorm(x, g[4]), p['w_ffn_gu'][i, 1], p['w_ffn_dn'][i, 1]), g[5])
    return x, jnp.stack(new_conv), jnp.stack(new_delta), jnp.stack(new_v)


def setup_inputs(seed: int = 0) -> dict:
    key = jax.random.key(seed)
    ks = jax.random.split(key, 24)

    def nrm(k, shape, scale):
        return jax.random.normal(k, shape, jnp.float32) * scale

    dt = jnp.exp(jax.random.uniform(ks[13], (N_A_LAYERS, DELTA_HEADS), jnp.float32,
                                    math.log(1e-3), math.log(1e-1)))
    return {
        'x_prompt': nrm(ks[0], (BATCH, SEQ, D_MODEL), 1.0),
        'x_sample': nrm(ks[1], (DEC_BATCH, DEC_SEQ, D_MODEL), 1.0),
        'mem_prompt': nrm(ks[2], (BATCH, N_MEM, D_MODEL), 1.0),
        'cache_mem_k': nrm(ks[3], (DEPTH, DEC_BATCH, N_MEM, X_HEADS, HEAD_DIM), 1.0),
        'cache_mem_v': nrm(ks[4], (DEPTH, DEC_BATCH, N_MEM, X_HEADS, HEAD_DIM), 1.0),
        'state_delta': nrm(ks[5], (N_A_LAYERS, DEC_BATCH, DELTA_HEADS, HEAD_DIM, HEAD_DIM), 0.1),
        'state_conv': nrm(ks[6], (N_A_LAYERS, DEC_BATCH, CONV_W - 1, 3 * DELTA_W), 1.0),
        'norm_gains': 1.0 + nrm(ks[7], (DEPTH, 6, D_MODEL), 0.02),
        'w_ffn_gu': nrm(ks[8], (DEPTH, 2, D_MODEL, 2 * FFN_DIM), D_MODEL ** -0.5),
        'w_ffn_dn': nrm(ks[9], (DEPTH, 2, FFN_DIM, D_MODEL), FFN_DIM ** -0.5),
        'w_in_a': nrm(ks[10], (N_A_LAYERS, D_MODEL, IN_A), D_MODEL ** -0.5),
        'conv_w': nrm(ks[11], (N_A_LAYERS, CONV_W, 3 * DELTA_W), CONV_W ** -0.5),
        'a_log': jnp.log(jax.random.uniform(ks[12], (N_A_LAYERS, DELTA_HEADS), jnp.float32, 1.0, 16.0)),
        'dt_bias': dt + jnp.log(-jnp.expm1(-dt)),
        'delta_norm_gain': 1.0 + nrm(ks[14], (N_A_LAYERS, HEAD_DIM), 0.02),
        'w_in_b': nrm(ks[15], (N_B_LAYERS, D_MODEL, IN_B), D_MODEL ** -0.5),
        'gmlp_ln_gain': 1.0 + nrm(ks[16], (N_B_LAYERS, GMLP_W), 0.02),
        'gmlp_ln_bias': nrm(ks[17], (N_B_LAYERS, GMLP_W), 0.02),
        'w_spatial': nrm(ks[18], (N_B_LAYERS, GMLP_GROUPS, GMLP_CHUNK, GMLP_CHUNK), GMLP_CHUNK ** -0.5),
        'b_spatial': 1.0 + nrm(ks[19], (N_B_LAYERS, GMLP_GROUPS, GMLP_CHUNK), 0.02),
        'mem_norm_gain': 1.0 + nrm(ks[20], (DEPTH, D_MODEL), 0.02),
        'w_mem_kv': nrm(ks[21], (DEPTH, D_MODEL, 2 * X_W), D_MODEL ** -0.5),
        'w_out': nrm(ks[22], (DEPTH, MIX_W, D_MODEL), MIX_W ** -0.5),
    }


def reference(x_prompt, x_sample, mem_prompt, cache_mem_k, cache_mem_v, state_delta, state_conv,
              norm_gains, w_ffn_gu, w_ffn_dn, w_in_a, conv_w, a_log, dt_bias, delta_norm_gain,
              w_in_b, gmlp_ln_gain, gmlp_ln_bias, w_spatial, b_spatial, mem_norm_gain, w_mem_kv, w_out):
    p = {
        'norm_gains': norm_gains, 'w_ffn_gu': w_ffn_gu, 'w_ffn_dn': w_ffn_dn,
        'w_in_a': w_in_a, 'conv_w': conv_w, 'a_log': a_log, 'dt_bias': dt_bias,
        'delta_norm_gain': delta_norm_gain, 'w_in_b': w_in_b, 'gmlp_ln_gain': gmlp_ln_gain,
        'gmlp_ln_bias': gmlp_ln_bias, 'w_spatial': w_spatial, 'b_spatial': b_spatial, 'w_out': w_out,
    }
    mkv = [_mem_kv(mem_prompt, mem_norm_gain[i], w_mem_kv[i]) for i in range(DEPTH)]
    mem_k_prompt = jnp.stack([kv[0] for kv in mkv])
    mem_v_prompt = jnp.stack([kv[1] for kv in mkv])
    conv0 = jnp.zeros((N_A_LAYERS, BATCH, CONV_W - 1, 3 * DELTA_W), x_prompt.dtype)
    delta0 = jnp.zeros((N_A_LAYERS, BATCH, DELTA_HEADS, HEAD_DIM, HEAD_DIM), jnp.float32)
    y_prompt, conv_prompt, delta_prompt, _ = _trunk(x_prompt, mem_k_prompt, mem_v_prompt, conv0, delta0, p)
    y_sample, conv_sample, delta_sample, gmlp_v_sample = _trunk(x_sample, cache_mem_k, cache_mem_v,
                                                                state_conv, state_delta, p)
    return (y_prompt, y_sample, mem_k_prompt, mem_v_prompt, delta_prompt, conv_prompt,
            delta_sample, conv_sample, gmlp_v_sample)
```

```python
import functools

import jax
import jax.numpy as jnp
from jax import lax
from jax.experimental import pallas as pl
from jax.experimental.pallas import tpu as pltpu

F32 = jnp.float32
BF16 = jnp.bfloat16
HIGHEST = lax.Precision.HIGHEST

EPS = 1e-6
HEAD_DIM = 128
X_HEADS = 4
X_W = X_HEADS * HEAD_DIM
DELTA_HEADS = 12
DELTA_W = DELTA_HEADS * HEAD_DIM
CONV_W = 4
DELTA_CHUNK = 64
GMLP_GROUPS = 12
GMLP_W = GMLP_GROUPS * HEAD_DIM
GMLP_CHUNK = 128
N_MIXERS = 2

LANES = 128
SUBLANES = 8
VMEM_LIMIT = 56 * 1024 * 1024

HEAD_GROUP = 4
N_HEAD_GROUPS = DELTA_HEADS // HEAD_GROUP
AB_W = 512
IN_A_W = 4 * DELTA_W + X_W + AB_W


def _params(*sem):
    return pltpu.CompilerParams(dimension_semantics=sem, vmem_limit_bytes=VMEM_LIMIT)


def _rms(x, gain):
    ms = jnp.mean(x * x, axis=-1, keepdims=True)
    return x * lax.rsqrt(ms + EPS) * gain


def _silu(x):
    return x * jax.nn.sigmoid(x)


def _softplus(x):
    return jnp.maximum(x, 0.0) + jnp.log1p(jnp.exp(-jnp.abs(x)))


def _bdot(a, b):
    return jnp.dot(a.astype(BF16), b.astype(BF16), preferred_element_type=F32)


def _ffn_kernel(x_ref, gpre_ref, gpost_ref, wg_ref, wu_ref, wd_ref, o_ref, hn_ref, acc_ref):
    f = pl.program_id(1)

    @pl.when(f == 0)
    def _():
        hn_ref[...] = _rms(x_ref[...], gpre_ref[...]).astype(BF16)
        acc_ref[...] = jnp.zeros_like(acc_ref)

    h = hn_ref[...]
    gate = jnp.dot(h, wg_ref[...], preferred_element_type=F32)
    up = jnp.dot(h, wu_ref[...], preferred_element_type=F32)
    act = (_silu(gate) * up).astype(BF16)
    acc_ref[...] += jnp.dot(act, wd_ref[...], preferred_element_type=F32)

    @pl.when(f == pl.num_programs(1) - 1)
    def _():
        o_ref[...] = x_ref[...] + 0.5 * _rms(acc_ref[...], gpost_ref[...])


def _ffn(x, gains, w_gu, w_dn, layer, slot, pre, post):
    m, d = x.shape
    ffn = w_dn.shape[2]
    tm = min(512, m)
    tf = 512
    nf = ffn // tf
    assert m % tm == 0 and ffn % tf == 0
    return pl.pallas_call(
        _ffn_kernel,
        grid=(m // tm, nf),
        in_specs=[
            pl.BlockSpec((tm, d), lambda i, f: (i, 0)),
            pl.BlockSpec((None, 1, d), lambda i, f: (layer * 6 + pre, 0, 0)),
            pl.BlockSpec((None, 1, d), lambda i, f: (layer * 6 + post, 0, 0)),
            pl.BlockSpec((None, None, d, tf), lambda i, f: (layer, slot, 0, f)),
            pl.BlockSpec((None, None, d, tf), lambda i, f: (layer, slot, 0, nf + f)),
            pl.BlockSpec((None, None, tf, d), lambda i, f: (layer, slot, f, 0)),
        ],
        out_specs=pl.BlockSpec((tm, d), lambda i, f: (i, 0)),
        out_shape=jax.ShapeDtypeStruct((m, d), F32),
        scratch_shapes=[pltpu.VMEM((tm, d), BF16), pltpu.VMEM((tm, d), F32)],
        compiler_params=_params("parallel", "arbitrary"),
    )(x, gains, gains, w_gu, w_gu, w_dn)


def _norm_matmul_kernel(x_ref, g_ref, w_ref, o_ref, hn_ref):
    @pl.when(pl.program_id(1) == 0)
    def _():
        hn_ref[...] = _rms(x_ref[...], g_ref[...]).astype(BF16)

    o_ref[...] = jnp.dot(hn_ref[...], w_ref[...], preferred_element_type=F32)


def _norm_matmul(x, gains, gain_idx, w, tn):
    m, d = x.shape
    n = w.shape[1]
    tm = min(512, m)
    assert m % tm == 0 and n % tn == 0
    return pl.pallas_call(
        _norm_matmul_kernel,
        grid=(m // tm, n // tn),
        in_specs=[
            pl.BlockSpec((tm, d), lambda i, j: (i, 0)),
            pl.BlockSpec((None, 1, d), lambda i, j: (gain_idx, 0, 0)),
            pl.BlockSpec((d, tn), lambda i, j: (0, j)),
        ],
        out_specs=pl.BlockSpec((tm, tn), lambda i, j: (i, j)),
        out_shape=jax.ShapeDtypeStruct((m, n), F32),
        scratch_shapes=[pltpu.VMEM((tm, d), BF16)],
        compiler_params=_params("parallel", "arbitrary"),
    )(x, gains, w)


def _out_proj_kernel(x_ref, mix_ref, mem_ref, wmix_ref, wmem_ref, g_ref, o_ref):
    y = _bdot(mix_ref[...], wmix_ref[...]) + _bdot(mem_ref[...], wmem_ref[...])
    o_ref[...] = x_ref[...] + _rms(y, g_ref[...])


def _out_proj(x, mix, mem, w_out, gains, layer):
    m, d = x.shape
    wm = mix.shape[1]
    wx = mem.shape[1]
    tm = min(512, m)
    assert wm % wx == 0
    return pl.pallas_call(
        _out_proj_kernel,
        grid=(m // tm,),
        in_specs=[
            pl.BlockSpec((tm, d), lambda i: (i, 0)),
            pl.BlockSpec((tm, wm), lambda i: (i, 0)),
            pl.BlockSpec((tm, wx), lambda i: (i, 0)),
            pl.BlockSpec((None, wm, d), lambda i: (layer, 0, 0)),
            pl.BlockSpec((None, wx, d), lambda i: (layer, wm // wx, 0)),
            pl.BlockSpec((None, 1, d), lambda i: (layer * 6 + 3, 0, 0)),
        ],
        out_specs=pl.BlockSpec((tm, d), lambda i: (i, 0)),
        out_shape=jax.ShapeDtypeStruct((m, d), F32),
        compiler_params=_params("parallel"),
    )(x, mix, mem, w_out, w_out, gains)


def _unit_lower_inverse(lmat, c):
    def mm(a, b):
        return jnp.einsum("nij,njk->nik", a, b, precision=HIGHEST, preferred_element_type=F32)

    ri = lax.broadcasted_iota(jnp.int32, (c, c), 0)
    ci = lax.broadcasted_iota(jnp.int32, (c, c), 1)
    eye = (ri == ci).astype(F32)
    base = min(c, 16)
    same = (ri // base) == (ci // base)
    dm = jnp.where(same, lmat, 0.0)
    x = eye - dm
    p = dm
    width = 1
    while 2 * width < base:
        p = mm(p, p)
        x = x + mm(x, p)
        width *= 2
    size = base
    while size < c:
        inner = ((ri // (2 * size)) == (ci // (2 * size))) & ((ri // size) != (ci // size))
        cm = jnp.where(inner, lmat, 0.0)
        x = x - mm(mm(x, cm), x)
        size *= 2
    return x


def _delta_chunk_terms(q, k, v, gcol, bcol, c):
    d = q.shape[-1]
    ri = lax.broadcasted_iota(jnp.int32, (c, c), 0)
    ci = lax.broadcasted_iota(jnp.int32, (c, c), 1)
    eye = (ri == ci).astype(F32)
    strict = ci < ri
    causal = ci <= ri
    grow = jnp.sum(gcol * eye, axis=1, keepdims=True)
    diff = gcol - grow
    kb = k.astype(BF16)
    kk = jnp.einsum("ncd,nmd->ncm", kb, kb, preferred_element_type=F32)
    lmat = jnp.where(strict, bcol * kk * jnp.exp(jnp.where(strict, diff, 0.0)), 0.0)
    tinv = _unit_lower_inverse(lmat, c)
    egc = jnp.exp(gcol)
    rhs = jnp.concatenate([v * bcol, k * (bcol * egc)], axis=-1)
    sol = jnp.einsum("ncm,nmd->ncd", tinv, rhs, precision=HIGHEST, preferred_element_type=F32)
    u, w = sol[..., :d], sol[..., d:]
    qk = jnp.einsum("ncd,nmd->ncm", q.astype(BF16), kb, preferred_element_type=F32)
    a_intra = jnp.where(causal, qk * jnp.exp(jnp.where(causal, diff, 0.0)), 0.0)
    glast = gcol[:, c - 1:c, :]
    q_dec = q * egc
    k_dec = k * jnp.exp(glast - gcol)
    return u, w.astype(BF16), a_intra.astype(BF16), q_dec.astype(BF16), k_dec.astype(BF16), jnp.exp(glast)


def _l2norm(x):
    return x * lax.rsqrt(jnp.sum(x * x, axis=-1, keepdims=True) + EPS)


def _block_cumsum(g, c):
    rows = g.shape[0]
    ri = lax.broadcasted_iota(jnp.int32, (rows, rows), 0)
    ci = lax.broadcasted_iota(jnp.int32, (rows, rows), 1)
    tri = (((ri // c) == (ci // c)) & (ci <= ri)).astype(F32)
    return jnp.dot(tri, g, precision=HIGHEST, preferred_element_type=F32)


def _shifted(x, prev, shift, row):
    xs = pltpu.roll(x, shift, x.ndim - 2)
    ps = pltpu.roll(prev, shift, prev.ndim - 2)
    if x.ndim == 2:
        reps = x.shape[0] // SUBLANES
        ps = jnp.broadcast_to(ps[None], (reps,) + ps.shape).reshape(x.shape)
    return jnp.where(row < shift, ps, xs)


def _short_conv_silu(x, prev, w, row):
    y = x * w[CONV_W - 1:CONV_W]
    for shift in range(1, CONV_W):
        tap = CONV_W - 1 - shift
        y = y + _shifted(x, prev, shift, row) * w[tap:tap + 1]
    return _silu(y)


def _delta_prompt_kernel(q_ref, k_ref, v_ref, z_ref, ab_ref, wq_ref, wk_ref, wv_ref, alog_ref, dtb_ref,
                         og_ref, o_ref, sout_ref, s_ref, pq_ref, pk_ref, pv_ref, *, tt):
    t = pl.program_id(2)
    c = DELTA_CHUNK
    n = tt // c
    d = HEAD_DIM

    @pl.when(t == 0)
    def _():
        s_ref[...] = jnp.zeros_like(s_ref)
        pq_ref[...] = jnp.zeros_like(pq_ref)
        pk_ref[...] = jnp.zeros_like(pk_ref)
        pv_ref[...] = jnp.zeros_like(pv_ref)

    row = lax.broadcasted_iota(jnp.int32, (tt, 1), 0)

    def conv(x_ref, p_ref, w_ref):
        x = x_ref[...]
        y = _short_conv_silu(x, p_ref[...], w_ref[...], row)
        p_ref[...] = x[tt - SUBLANES:, :]
        return y

    qc = conv(q_ref, pq_ref, wq_ref)
    kc = conv(k_ref, pk_ref, wk_ref)
    vc = conv(v_ref, pv_ref, wv_ref)

    ab = ab_ref[...]
    g_all = -jnp.exp(alog_ref[...]) * _softplus(ab + dtb_ref[...])
    beta_all = jax.nn.sigmoid(ab)
    gc_all = _block_cumsum(g_all, c)

    terms = []
    for hh in range(HEAD_GROUP):
        sl = slice(hh * d, (hh + 1) * d)
        qh = (_l2norm(qc[:, sl]) * (d ** -0.5)).reshape(n, c, d)
        kh = _l2norm(kc[:, sl]).reshape(n, c, d)
        vh = vc[:, sl].reshape(n, c, d)
        gcol = gc_all[:, hh:hh + 1].reshape(n, c, 1)
        bcol = beta_all[:, HEAD_GROUP + hh:HEAD_GROUP + hh + 1].reshape(n, c, 1)
        terms.append(_delta_chunk_terms(qh, kh, vh, gcol, bcol, c))

    gain = og_ref[...]
    states = [s_ref[hh] for hh in range(HEAD_GROUP)]
    for ch in range(n):
        for hh in range(HEAD_GROUP):
            u, w, a_intra, q_dec, k_dec, g_last = terms[hh]
            s = states[hh]
            sb = s.astype(BF16)
            v_new = u[ch] - jnp.dot(w[ch], sb, preferred_element_type=F32)
            vb = v_new.astype(BF16)
            o = (jnp.dot(q_dec[ch], sb, preferred_element_type=F32)
                 + jnp.dot(a_intra[ch], vb, preferred_element_type=F32))
            states[hh] = s * g_last[ch] + lax.dot_general(
                k_dec[ch], vb, (((0,), (0,)), ((), ())), preferred_element_type=F32)
            zz = z_ref[ch * c:(ch + 1) * c, hh * d:(hh + 1) * d]
            o_ref[ch * c:(ch + 1) * c, hh * d:(hh + 1) * d] = _rms(o, gain) * _silu(zz)
    for hh in range(HEAD_GROUP):
        s_ref[hh] = states[hh]
        sout_ref[hh] = states[hh]


def _delta_prompt(proj, batch, seq, conv_w, alog_rows, dtb_rows, out_gain):
    tt = min(512, seq)
    nt = seq // tt
    gw = HEAD_GROUP * HEAD_DIM
    ng = N_HEAD_GROUPS
    assert seq % tt == 0 and tt % DELTA_CHUNK == 0

    def col(base):
        return pl.BlockSpec((tt, gw), lambda b, g, t: (b * nt + t, base * ng + g))

    def wcol(base):
        return pl.BlockSpec((CONV_W, gw), lambda b, g, t: (0, base * ng + g))

    ab_base = (4 * DELTA_W + X_W) // LANES
    return pl.pallas_call(
        functools.partial(_delta_prompt_kernel, tt=tt),
        grid=(batch, ng, nt),
        in_specs=[
            col(0), col(1), col(2), col(3),
            pl.BlockSpec((tt, LANES), lambda b, g, t: (b * nt + t, ab_base + g)),
            wcol(0), wcol(1), wcol(2),
            pl.BlockSpec((1, LANES), lambda b, g, t: (0, g)),
            pl.BlockSpec((1, LANES), lambda b, g, t: (0, g)),
            pl.BlockSpec((1, HEAD_DIM), lambda b, g, t: (0, 0)),
        ],
        out_specs=[
            pl.BlockSpec((tt, gw), lambda b, g, t: (b * nt + t, g)),
            pl.BlockSpec((None, HEAD_GROUP, HEAD_DIM, HEAD_DIM), lambda b, g, t: (b, g, 0, 0)),
        ],
        out_shape=[
            jax.ShapeDtypeStruct((batch * seq, DELTA_W), F32),
            jax.ShapeDtypeStruct((batch, DELTA_HEADS, HEAD_DIM, HEAD_DIM), F32),
        ],
        scratch_shapes=[
            pltpu.VMEM((HEAD_GROUP, HEAD_DIM, HEAD_DIM), F32),
            pltpu.VMEM((SUBLANES, gw), F32),
            pltpu.VMEM((SUBLANES, gw), F32),
            pltpu.VMEM((SUBLANES, gw), F32),
        ],
        compiler_params=_params("parallel", "parallel", "arbitrary"),
    )(proj, proj, proj, proj, proj, conv_w, conv_w, conv_w, alog_rows, dtb_rows, out_gain)


def _delta_sample_kernel(qkv_ref, z_ref, ab_ref, buf_ref, s0_ref, w_ref, alog_ref, dtb_ref, og_ref,
                         o_ref, sout_ref, *, bb, seq):
    d = HEAD_DIM
    c = seq
    x = qkv_ref[...].reshape(bb, seq, 3 * DELTA_W)
    row = lax.broadcasted_iota(jnp.int32, (1, seq, 1), 1)
    y = _short_conv_silu(x, buf_ref[...], w_ref[...], row)

    ab = ab_ref[...]
    g_all = -jnp.exp(alog_ref[...]) * _softplus(ab + dtb_ref[...])
    beta_all = jax.nn.sigmoid(ab)
    gc_all = _block_cumsum(g_all, c)
    gain = og_ref[...]

    for h in range(DELTA_HEADS):
        lane_a = (h // HEAD_GROUP) * LANES + h % HEAD_GROUP
        lane_b = lane_a + HEAD_GROUP
        qh = _l2norm(y[:, :, h * d:(h + 1) * d]) * (d ** -0.5)
        kh = _l2norm(y[:, :, DELTA_W + h * d:DELTA_W + (h + 1) * d])
        vh = y[:, :, 2 * DELTA_W + h * d:2 * DELTA_W + (h + 1) * d]
        gcol = gc_all[:, lane_a:lane_a + 1].reshape(bb, c, 1)
        bcol = beta_all[:, lane_b:lane_b + 1].reshape(bb, c, 1)
        u, w, a_intra, q_dec, k_dec, g_last = _delta_chunk_terms(qh, kh, vh, gcol, bcol, c)
        s = s0_ref[:, h]
        sb = s.astype(BF16)
        v_new = u - jnp.einsum("bcd,bde->bce", w, sb, preferred_element_type=F32)
        vb = v_new.astype(BF16)
        o = (jnp.einsum("bcd,bde->bce", q_dec, sb, preferred_element_type=F32)
             + jnp.einsum("bij,bje->bie", a_intra, vb, preferred_element_type=F32))
        sout_ref[:, h] = s * g_last + jnp.einsum("bcd,bce->bde", k_dec, vb, preferred_element_type=F32)
        zz = z_ref[:, h * d:(h + 1) * d].reshape(bb, seq, d)
        o_ref[:, h * d:(h + 1) * d] = (_rms(o, gain) * _silu(zz)).reshape(bb * seq, d)


def _delta_sample(proj, batch, seq, conv_buf8, s0, conv_w, alog_rows, dtb_rows, out_gain):
    bb = 8
    assert batch % bb == 0 and seq == SUBLANES
    rows = bb * seq
    qkv_w = 3 * DELTA_W
    return pl.pallas_call(
        functools.partial(_delta_sample_kernel, bb=bb, seq=seq),
        grid=(batch // bb,),
        in_specs=[
            pl.BlockSpec((rows, qkv_w), lambda i: (i, 0)),
            pl.BlockSpec((rows, DELTA_W), lambda i: (i, qkv_w // DELTA_W)),
            pl.BlockSpec((rows, AB_W), lambda i: (i, (4 * DELTA_W + X_W) // AB_W)),
            pl.BlockSpec((bb, SUBLANES, qkv_w), lambda i: (i, 0, 0)),
            pl.BlockSpec((bb, DELTA_HEADS, HEAD_DIM, HEAD_DIM), lambda i: (i, 0, 0, 0)),
            pl.BlockSpec((CONV_W, qkv_w), lambda i: (0, 0)),
            pl.BlockSpec((1, AB_W), lambda i: (0, 0)),
            pl.BlockSpec((1, AB_W), lambda i: (0, 0)),
            pl.BlockSpec((1, HEAD_DIM), lambda i: (0, 0)),
        ],
        out_specs=[
            pl.BlockSpec((rows, DELTA_W), lambda i: (i, 0)),
            pl.BlockSpec((bb, DELTA_HEADS, HEAD_DIM, HEAD_DIM), lambda i: (i, 0, 0, 0)),
        ],
        out_shape=[
            jax.ShapeDtypeStruct((batch * seq, DELTA_W), F32),
            jax.ShapeDtypeStruct((batch, DELTA_HEADS, HEAD_DIM, HEAD_DIM), F32),
        ],
        compiler_params=_params("parallel"),
    )(proj, proj, proj, conv_buf8, s0, conv_w, alog_rows, dtb_rows, out_gain)


def _gelu(x):
    return 0.5 * x * (1.0 + lax.erf(x * (2.0 ** -0.5)))


def _layernorm(x, gain, bias):
    mu = jnp.mean(x, axis=-1, keepdims=True)
    xc = x - mu
    var = jnp.mean(xc * xc, axis=-1, keepdims=True)
    return xc * lax.rsqrt(var + EPS) * gain + bias


def _gmlp_prompt_kernel(u_ref, v_ref, lng_ref, lnb_ref, ws_ref, bs_ref, o_ref, *, chunks):
    c = GMLP_CHUNK
    d = HEAD_DIM
    ri = lax.broadcasted_iota(jnp.int32, (c, c), 0)
    ci = lax.broadcasted_iota(jnp.int32, (c, c), 1)
    lower = ci <= ri
    v = _layernorm(_gelu(v_ref[...]), lng_ref[...], lnb_ref[...]).astype(BF16)
    bs = bs_ref[...]
    for g in range(GMLP_GROUPS):
        w = jnp.where(lower, ws_ref[g], 0.0).astype(BF16)
        bias = bs[:, g:g + 1]
        for ch in range(chunks):
            rs = slice(ch * c, (ch + 1) * c)
            cs = slice(g * d, (g + 1) * d)
            mixed = jnp.dot(w, v[rs, cs], preferred_element_type=F32) + bias
            o_ref[rs, cs] = _gelu(u_ref[rs, cs]) * mixed


def _gmlp_prompt(proj, ln_gain, ln_bias, w_s, b_s_t):
    m = proj.shape[0]
    chunks = 2
    tm = chunks * GMLP_CHUNK
    assert m % tm == 0
    return pl.pallas_call(
        functools.partial(_gmlp_prompt_kernel, chunks=chunks),
        grid=(m // tm,),
        in_specs=[
            pl.BlockSpec((tm, GMLP_W), lambda i: (i, 0)),
            pl.BlockSpec((tm, GMLP_W), lambda i: (i, 1)),
            pl.BlockSpec((1, GMLP_W), lambda i: (0, 0)),
            pl.BlockSpec((1, GMLP_W), lambda i: (0, 0)),
            pl.BlockSpec((GMLP_GROUPS, GMLP_CHUNK, GMLP_CHUNK), lambda i: (0, 0, 0)),
            pl.BlockSpec((GMLP_CHUNK, GMLP_GROUPS), lambda i: (0, 0)),
        ],
        out_specs=pl.BlockSpec((tm, GMLP_W), lambda i: (i, 0)),
        out_shape=jax.ShapeDtypeStruct((m, GMLP_W), F32),
        compiler_params=_params("parallel"),
    )(proj, proj, ln_gain, ln_bias, w_s, b_s_t)


def _gmlp_sample_kernel(u_ref, v_ref, lng_ref, lnb_ref, wx_ref, bx_ref, o_ref, vout_ref, *, bb, seq):
    v = _layernorm(_gelu(v_ref[...]), lng_ref[...], lnb_ref[...])
    vout_ref[...] = v
    v3 = v.reshape(bb, seq, GMLP_W)
    row = lax.broadcasted_iota(jnp.int32, (seq, 1), 0)
    mixed = jnp.broadcast_to(bx_ref[...][None], (bb, seq, GMLP_W))
    for j in range(seq):
        wj = jnp.where(row >= j, wx_ref[j], 0.0)
        mixed = mixed + wj[None] * v3[:, j:j + 1, :]
    o_ref[...] = _gelu(u_ref[...]) * mixed.reshape(bb * seq, GMLP_W)


def _gmlp_sample(proj, batch, seq, ln_gain, ln_bias, w_exp, b_exp):
    bb = 32
    assert batch % bb == 0 and seq == SUBLANES
    rows = bb * seq
    return pl.pallas_call(
        functools.partial(_gmlp_sample_kernel, bb=bb, seq=seq),
        grid=(batch // bb,),
        in_specs=[
            pl.BlockSpec((rows, GMLP_W), lambda i: (i, 0)),
            pl.BlockSpec((rows, GMLP_W), lambda i: (i, 1)),
            pl.BlockSpec((1, GMLP_W), lambda i: (0, 0)),
            pl.BlockSpec((1, GMLP_W), lambda i: (0, 0)),
            pl.BlockSpec((seq, seq, GMLP_W), lambda i: (0, 0, 0)),
            pl.BlockSpec((seq, GMLP_W), lambda i: (0, 0)),
        ],
        out_specs=[
            pl.BlockSpec((rows, GMLP_W), lambda i: (i, 0)),
            pl.BlockSpec((rows, GMLP_W), lambda i: (i, 0)),
        ],
        out_shape=[
            jax.ShapeDtypeStruct((batch * seq, GMLP_W), F32),
            jax.ShapeDtypeStruct((batch * seq, GMLP_W), F32),
        ],
        compiler_params=_params("parallel"),
    )(proj, proj, ln_gain, ln_bias, w_exp, b_exp)


def _softmax(s):
    m = jnp.max(s, axis=-1, keepdims=True)
    e = jnp.exp(s - m)
    return e / jnp.sum(e, axis=-1, keepdims=True)


def _mem_attn_prompt_kernel(q_ref, k_ref, v_ref, o_ref):
    d = HEAD_DIM
    for h in range(X_HEADS):
        sl = slice(h * d, (h + 1) * d)
        s = lax.dot_general(q_ref[:, sl].astype(BF16), k_ref[:, sl].astype(BF16),
                            (((1,), (1,)), ((), ())), preferred_element_type=F32) * (d ** -0.5)
        o_ref[:, sl] = _bdot(_softmax(s), v_ref[:, sl])


def _mem_attn_prompt(proj, q_block, batch, seq, mem_k, mem_v):
    tt = min(512, seq)
    nt = seq // tt
    n_mem = mem_k.shape[1]
    return pl.pallas_call(
        _mem_attn_prompt_kernel,
        grid=(batch, nt),
        in_specs=[
            pl.BlockSpec((tt, X_W), lambda b, t: (b * nt + t, q_block)),
            pl.BlockSpec((None, n_mem, X_W), lambda b, t: (b, 0, 0)),
            pl.BlockSpec((None, n_mem, X_W), lambda b, t: (b, 0, 0)),
        ],
        out_specs=pl.BlockSpec((tt, X_W), lambda b, t: (b * nt + t, 0)),
        out_shape=jax.ShapeDtypeStruct((batch * seq, X_W), F32),
        compiler_params=_params("parallel", "parallel"),
    )(proj, mem_k, mem_v)


def _mem_attn_sample_kernel(q_ref, k_ref, v_ref, o_ref, *, bb, seq):
    d = HEAD_DIM
    for h in range(X_HEADS):
        sl = slice(h * d, (h + 1) * d)
        q = q_ref[:, sl].reshape(bb, seq, d).astype(BF16)
        s = jnp.einsum("bqd,bkd->bqk", q, k_ref[:, :, sl].astype(BF16),
                       preferred_element_type=F32) * (d ** -0.5)
        p = _softmax(s).astype(BF16)
        o = jnp.einsum("bqk,bkd->bqd", p, v_ref[:, :, sl].astype(BF16), preferred_element_type=F32)
        o_ref[:, sl] = o.reshape(bb * seq, d)


def _mem_attn_sample(proj, q_block, batch, seq, mem_k, mem_v, layer):
    bb = 8
    rows = bb * seq
    n_mem = mem_k.shape[2]
    return pl.pallas_call(
        functools.partial(_mem_attn_sample_kernel, bb=bb, seq=seq),
        grid=(batch // bb,),
        in_specs=[
            pl.BlockSpec((rows, X_W), lambda i: (i, q_block)),
            pl.BlockSpec((None, bb, n_mem, X_W), lambda i: (layer, i, 0, 0)),
            pl.BlockSpec((None, bb, n_mem, X_W), lambda i: (layer, i, 0, 0)),
        ],
        out_specs=pl.BlockSpec((rows, X_W), lambda i: (i, 0)),
        out_shape=jax.ShapeDtypeStruct((batch * seq, X_W), F32),
        compiler_params=_params("parallel"),
    )(proj, mem_k, mem_v)


def _delta_in_weight(w_in_a):
    d_model = w_in_a.shape[0]
    qkvz = w_in_a[:, :4 * DELTA_W]
    a = w_in_a[:, 4 * DELTA_W:4 * DELTA_W + DELTA_HEADS]
    b = w_in_a[:, 4 * DELTA_W + DELTA_HEADS:4 * DELTA_W + 2 * DELTA_HEADS]
    xq = w_in_a[:, 4 * DELTA_W + 2 * DELTA_HEADS:]
    parts = [qkvz, xq]
    pad = jnp.zeros((d_model, LANES - 2 * HEAD_GROUP), w_in_a.dtype)
    for g in range(N_HEAD_GROUPS):
        hs = slice(g * HEAD_GROUP, (g + 1) * HEAD_GROUP)
        parts += [a[:, hs], b[:, hs], pad]
    parts.append(jnp.zeros((d_model, AB_W - N_HEAD_GROUPS * LANES), w_in_a.dtype))
    return jnp.concatenate(parts, axis=1).astype(BF16)


def _gate_rows(vec):
    row = jnp.zeros((N_HEAD_GROUPS, LANES), F32)
    row = row.at[:, :HEAD_GROUP].set(vec.astype(F32).reshape(N_HEAD_GROUPS, HEAD_GROUP))
    return jnp.concatenate([row.reshape(1, -1), jnp.zeros((1, AB_W - N_HEAD_GROUPS * LANES), F32)], axis=1)


def _trunk(x, batch, seq, is_prompt, mem_k, mem_v, conv_bufs, delta_states, p):
    new_conv, new_delta, new_v = [], [], []
    gains = p["gains"]
    depth = p["w_out"].shape[0]
    for i in range(depth):
        j = i // N_MIXERS
        x = _ffn(x, gains, p["w_ffn_gu"], p["w_ffn_dn"], i, 0, 0, 1)
        if i % N_MIXERS == 0:
            proj = _norm_matmul(x, gains, i * 6 + 2, p["w_in_a"][j], 512)
            q_block = 4 * DELTA_W // X_W
            if is_prompt:
                mix, s_new = _delta_prompt(proj, batch, seq, p["conv_w"][j], p["alog_rows"][j],
                                           p["dtb_rows"][j], p["delta_norm_gain"][j])
            else:
                mix, s_new = _delta_sample(proj, batch, seq, conv_bufs[j], delta_states[j], p["conv_w"][j],
                                           p["alog_rows"][j], p["dtb_rows"][j], p["delta_norm_gain"][j])
            new_conv.append(proj.reshape(batch, seq, -1)[:, seq - (CONV_W - 1):, :3 * DELTA_W])
            new_delta.append(s_new)
        else:
            proj = _norm_matmul(x, gains, i * 6 + 2, p["w_in_b"][j], 512)
            q_block = 2 * GMLP_W // X_W
            if is_prompt:
                mix = _gmlp_prompt(proj, p["gmlp_ln_gain"][j], p["gmlp_ln_bias"][j], p["w_spatial"][j],
                                   p["b_spatial_t"][j])
            else:
                mix, v_rows = _gmlp_sample(proj, batch, seq, p["gmlp_ln_gain"][j], p["gmlp_ln_bias"][j],
                                           p["w_spatial_exp"][j], p["b_spatial_exp"][j])
                new_v.append(v_rows.reshape(batch, seq, GMLP_W))
        if is_prompt:
            mem_out = _mem_attn_prompt(proj, q_block, batch, seq, mem_k[i], mem_v[i])
        else:
            mem_out = _mem_attn_sample(proj, q_block, batch, seq, mem_k, mem_v, i)
        x = _out_proj(x, mix, mem_out, p["w_out"], gains, i)
        x = _ffn(x, gains, p["w_ffn_gu"], p["w_ffn_dn"], i, 1, 4, 5)
    return x, new_conv, new_delta, new_v


def kernel(x_prompt, x_sample, mem_prompt, cache_mem_k, cache_mem_v, state_delta, state_conv, norm_gains, w_ffn_gu, w_ffn_dn, w_in_a, conv_w, a_log, dt_bias, delta_norm_gain, w_in_b, gmlp_ln_gain, gmlp_ln_bias, w_spatial, b_spatial, mem_norm_gain, w_mem_kv, w_out):
    batch, seq, d_model = x_prompt.shape
    dec_batch, dec_seq, _ = x_sample.shape
    depth = w_out.shape[0]
    n_mem = mem_prompt.shape[1]
    n_a = w_in_a.shape[0]
    n_b = w_in_b.shape[0]

    w_s_dec = w_spatial[:, :, :dec_seq, :dec_seq]
    p = {
        "gains": norm_gains.reshape(depth * 6, 1, d_model),
        "w_ffn_gu": w_ffn_gu.astype(BF16),
        "w_ffn_dn": w_ffn_dn.astype(BF16),
        "w_in_a": [_delta_in_weight(w_in_a[j]) for j in range(n_a)],
        "w_in_b": [w_in_b[j].astype(BF16) for j in range(n_b)],
        "w_out": w_out.astype(BF16),
        "conv_w": conv_w,
        "alog_rows": [_gate_rows(a_log[j]) for j in range(n_a)],
        "dtb_rows": [_gate_rows(dt_bias[j]) for j in range(n_a)],
        "delta_norm_gain": delta_norm_gain.reshape(n_a, 1, HEAD_DIM),
        "gmlp_ln_gain": gmlp_ln_gain.reshape(n_b, 1, GMLP_W),
        "gmlp_ln_bias": gmlp_ln_bias.reshape(n_b, 1, GMLP_W),
        "w_spatial": w_spatial,
        "b_spatial_t": jnp.swapaxes(b_spatial, 1, 2),
        "w_spatial_exp": jnp.repeat(jnp.transpose(w_s_dec, (0, 3, 2, 1)), HEAD_DIM, axis=-1),
        "b_spatial_exp": jnp.repeat(jnp.swapaxes(b_spatial[:, :, :dec_seq], 1, 2), HEAD_DIM, axis=-1),
    }

    mem2d = mem_prompt.reshape(batch * n_mem, d_model)
    mem_gains = mem_norm_gain.reshape(depth, 1, d_model)
    w_kv = w_mem_kv.astype(BF16)
    kvs = [_norm_matmul(mem2d, mem_gains, i, w_kv[i], 512) for i in range(depth)]
    mem_k_prompt = jnp.stack([kv[:, :X_W].reshape(batch, n_mem, X_W) for kv in kvs])
    mem_v_prompt = jnp.stack([kv[:, X_W:].reshape(batch, n_mem, X_W) for kv in kvs])

    y_prompt, conv_p, delta_p, _ = _trunk(x_prompt.reshape(batch * seq, d_model), batch, seq, True,
                                          mem_k_prompt, mem_v_prompt, None, None, p)

    conv_buf8 = jnp.pad(state_conv, ((0, 0), (0, 0), (SUBLANES - (CONV_W - 1), 0), (0, 0)))
    y_sample, conv_s, delta_s, v_s = _trunk(
        x_sample.reshape(dec_batch * dec_seq, d_model), dec_batch, dec_seq, False,
        cache_mem_k.reshape(depth, dec_batch, n_mem, X_W), cache_mem_v.reshape(depth, dec_batch, n_mem, X_W),
        conv_buf8, state_delta, p)

    return (y_prompt.reshape(batch, seq, d_model),
            y_sample.reshape(dec_batch, dec_seq, d_model),
            mem_k_prompt.reshape(depth, batch, n_mem, X_HEADS, HEAD_DIM),
            mem_v_prompt.reshape(depth, batch, n_mem, X_HEADS, HEAD_DIM),
            jnp.stack(delta_p), jnp.stack(conv_p), jnp.stack(delta_s), jnp.stack(conv_s), jnp.stack(v_s))
```

```python
import functools

import jax
import jax.numpy as jnp
from jax import lax
from jax.experimental import pallas as pl
from jax.experimental.pallas import tpu as pltpu

F32 = jnp.float32
BF16 = jnp.bfloat16

EPS = 1e-6
HEAD_DIM = 128
X_HEADS = 4
X_W = X_HEADS * HEAD_DIM
DELTA_HEADS = 12
DELTA_W = DELTA_HEADS * HEAD_DIM
CONV_W = 4
DELTA_CHUNK = 64
GMLP_GROUPS = 12
GMLP_W = GMLP_GROUPS * HEAD_DIM
GMLP_CHUNK = 128
N_MIXERS = 2

LANES = 128
SUBLANES = 8
VMEM_LIMIT = 56 * 1024 * 1024

HEAD_GROUP = 4
N_HEAD_GROUPS = DELTA_HEADS // HEAD_GROUP
FFN_ROWS = 1024
FFN_COLS = 512
PROJ_ROWS = 1024
AB_W = 512
IN_A_W = 4 * DELTA_W + X_W + AB_W


def _params(*sem):
    return pltpu.CompilerParams(dimension_semantics=sem, vmem_limit_bytes=VMEM_LIMIT)


def _rms(x, gain):
    ms = jnp.mean(x * x, axis=-1, keepdims=True)
    return x * lax.rsqrt(ms + EPS) * gain


def _silu(x):
    return x * jax.nn.sigmoid(x)


def _softplus(x):
    return jnp.maximum(x, 0.0) + jnp.log1p(jnp.exp(-jnp.abs(x)))


def _bdot(a, b):
    return jnp.dot(a.astype(BF16), b.astype(BF16), preferred_element_type=F32)


def _ffn_kernel(x_ref, gpre_ref, gpost_ref, wg_ref, wu_ref, wd_ref, o_ref, hn_ref):
    f = pl.program_id(1)

    @pl.when(f == 0)
    def _():
        hn_ref[...] = _rms(x_ref[...], gpre_ref[...]).astype(BF16)
        o_ref[...] = jnp.zeros_like(o_ref)

    h = hn_ref[...]
    gate = jnp.dot(h, wg_ref[...], preferred_element_type=F32)
    up = jnp.dot(h, wu_ref[...], preferred_element_type=F32)
    act = (_silu(gate) * up).astype(BF16)
    o_ref[...] += jnp.dot(act, wd_ref[...], preferred_element_type=F32)

    @pl.when(f == pl.num_programs(1) - 1)
    def _():
        o_ref[...] = x_ref[...] + 0.5 * _rms(o_ref[...], gpost_ref[...])


def _ffn(x, gains, w_gu, w_dn, layer, slot, pre, post):
    m, d = x.shape
    ffn = w_dn.shape[2]
    tm = min(FFN_ROWS, m)
    tf = FFN_COLS
    nf = ffn // tf
    assert m % tm == 0 and ffn % tf == 0
    return pl.pallas_call(
        _ffn_kernel,
        grid=(m // tm, nf),
        in_specs=[
            pl.BlockSpec((tm, d), lambda i, f: (i, 0), pipeline_mode=pl.Buffered(1)),
            pl.BlockSpec((None, 1, d), lambda i, f: (layer * 6 + pre, 0, 0)),
            pl.BlockSpec((None, 1, d), lambda i, f: (layer * 6 + post, 0, 0)),
            pl.BlockSpec((None, None, d, tf), lambda i, f: (layer, slot, 0, f)),
            pl.BlockSpec((None, None, d, tf), lambda i, f: (layer, slot, 0, nf + f)),
            pl.BlockSpec((None, None, tf, d), lambda i, f: (layer, slot, f, 0)),
        ],
        out_specs=pl.BlockSpec((tm, d), lambda i, f: (i, 0)),
        out_shape=jax.ShapeDtypeStruct((m, d), F32),
        scratch_shapes=[pltpu.VMEM((tm, d), BF16)],
        compiler_params=_params("parallel", "arbitrary"),
    )(x, gains, gains, w_gu, w_gu, w_dn)


def _norm_matmul_kernel(x_ref, g_ref, w_ref, o_ref, hn_ref):
    @pl.when(pl.program_id(1) == 0)
    def _():
        hn_ref[...] = _rms(x_ref[...], g_ref[...]).astype(BF16)

    o_ref[...] = jnp.dot(hn_ref[...], w_ref[...], preferred_element_type=F32)


def _norm_matmul(x, gains, gain_idx, w, tn):
    m, d = x.shape
    n = w.shape[1]
    tm = min(PROJ_ROWS, m)
    assert m % tm == 0 and n % tn == 0
    return pl.pallas_call(
        _norm_matmul_kernel,
        grid=(m // tm, n // tn),
        in_specs=[
            pl.BlockSpec((tm, d), lambda i, j: (i, 0)),
            pl.BlockSpec((None, 1, d), lambda i, j: (gain_idx, 0, 0)),
            pl.BlockSpec((d, tn), lambda i, j: (0, j)),
        ],
        out_specs=pl.BlockSpec((tm, tn), lambda i, j: (i, j)),
        out_shape=jax.ShapeDtypeStruct((m, n), F32),
        scratch_shapes=[pltpu.VMEM((tm, d), BF16)],
        compiler_params=_params("parallel", "arbitrary"),
    )(x, gains, w)


def _out_proj_kernel(x_ref, mix_ref, mem_ref, wmix_ref, wmem_ref, g_ref, o_ref):
    y = _bdot(mix_ref[...], wmix_ref[...]) + _bdot(mem_ref[...], wmem_ref[...])
    o_ref[...] = x_ref[...] + _rms(y, g_ref[...])


def _out_proj(x, mix, mem, w_out, gains, layer):
    m, d = x.shape
    wm = mix.shape[1]
    wx = mem.shape[1]
    tm = min(512, m)
    assert wm % wx == 0
    return pl.pallas_call(
        _out_proj_kernel,
        grid=(m // tm,),
        in_specs=[
            pl.BlockSpec((tm, d), lambda i: (i, 0)),
            pl.BlockSpec((tm, wm), lambda i: (i, 0)),
            pl.BlockSpec((tm, wx), lambda i: (i, 0)),
            pl.BlockSpec((None, wm, d), lambda i: (layer, 0, 0)),
            pl.BlockSpec((None, wx, d), lambda i: (layer, wm // wx, 0)),
            pl.BlockSpec((None, 1, d), lambda i: (layer * 6 + 3, 0, 0)),
        ],
        out_specs=pl.BlockSpec((tm, d), lambda i: (i, 0)),
        out_shape=jax.ShapeDtypeStruct((m, d), F32),
        compiler_params=_params("parallel"),
    )(x, mix, mem, w_out, w_out, gains)


def _unit_lower_inverse(lmat, c):
    ri = lax.broadcasted_iota(jnp.int32, (c, c), 0)
    ci = lax.broadcasted_iota(jnp.int32, (c, c), 1)
    eye = (ri == ci).astype(F32)
    base = min(c, 16)
    same = (ri // base) == (ci // base)
    dm = jnp.where(same, lmat, 0.0)
    x = eye - dm
    p = _split(dm)
    width = 1
    while 2 * width < base:
        p = _split(_mm3(p, p))
        x = x + _mm3(_split(x), p)
        width *= 2
    size = base
    while size < c:
        inner = ((ri // (2 * size)) == (ci // (2 * size))) & ((ri // size) != (ci // size))
        xs = _split(x)
        x = x - _mm3(_split(_mm3(xs, _split(jnp.where(inner, lmat, 0.0)))), xs)
        size *= 2
    return x


def _split(x):
    hi = x.astype(BF16)
    return hi, (x - hi.astype(F32)).astype(BF16)


def _mm3(a, b):
    def mm(p, q):
        return jnp.einsum("nij,njk->nik", p, q, preferred_element_type=F32)

    return mm(a[0], b[0]) + (mm(a[0], b[1]) + mm(a[1], b[0]))


def _delta_chunk_terms(q, k, v, gcol, bcol, c):
    d = q.shape[-1]
    ri = lax.broadcasted_iota(jnp.int32, (c, c), 0)
    ci = lax.broadcasted_iota(jnp.int32, (c, c), 1)
    eye = (ri == ci).astype(F32)
    strict = ci < ri
    causal = ci <= ri
    grow = jnp.sum(gcol * eye, axis=1, keepdims=True)
    diff = gcol - grow
    kb = k.astype(BF16)
    kk = jnp.einsum("ncd,nmd->ncm", kb, kb, preferred_element_type=F32)
    lmat = jnp.where(strict, bcol * kk * jnp.exp(jnp.where(strict, diff, 0.0)), 0.0)
    tinv = _unit_lower_inverse(lmat, c)
    egc = jnp.exp(gcol)
    rhs = jnp.concatenate([v * bcol, k * (bcol * egc)], axis=-1)
    sol = _mm3(_split(tinv), _split(rhs))
    u, w = sol[..., :d], sol[..., d:]
    qk = jnp.einsum("ncd,nmd->ncm", q.astype(BF16), kb, preferred_element_type=F32)
    a_intra = jnp.where(causal, qk * jnp.exp(jnp.where(causal, diff, 0.0)), 0.0)
    glast = gcol[:, c - 1:c, :]
    q_dec = q * egc
    k_dec = k * jnp.exp(glast - gcol)
    return u, w.astype(BF16), a_intra.astype(BF16), q_dec.astype(BF16), k_dec.astype(BF16), jnp.exp(glast)


def _l2norm(x):
    return x * lax.rsqrt(jnp.sum(x * x, axis=-1, keepdims=True) + EPS)


def _block_cumsum(g, c):
    rows, lanes = g.shape
    n = rows // c
    ri = lax.broadcasted_iota(jnp.int32, (n, c, c), 1)
    ci = lax.broadcasted_iota(jnp.int32, (n, c, c), 2)
    tri = (ci <= ri).astype(BF16)
    g3 = g.reshape(n, c, lanes)
    hi = g3.astype(BF16)
    r1 = g3 - hi.astype(F32)
    mid = r1.astype(BF16)
    lo = (r1 - mid.astype(F32)).astype(BF16)

    def mm(p):
        return jnp.einsum("nij,njk->nik", tri, p, preferred_element_type=F32)

    return (mm(hi) + (mm(mid) + mm(lo))).reshape(rows, lanes)


def _shifted(x, prev, shift, row):
    xs = pltpu.roll(x, shift, x.ndim - 2)
    if x.ndim == 2:
        ps = pltpu.roll(prev, shift, 0)
        reps = x.shape[0] // SUBLANES
        ps = jnp.broadcast_to(ps[None], (reps,) + ps.shape).reshape(x.shape)
        return jnp.where(row < shift, ps, xs)
    for t in range(shift):
        src = prev.shape[1] - shift + t
        xs = jnp.where(row == t, prev[:, src:src + 1, :], xs)
    return xs


def _short_conv_silu(x, prev, w, row):
    y = x * w[CONV_W - 1:CONV_W]
    for shift in range(1, CONV_W):
        tap = CONV_W - 1 - shift
        y = y + _shifted(x, prev, shift, row) * w[tap:tap + 1]
    return _silu(y)


def _delta_prompt_kernel(q_ref, k_ref, v_ref, z_ref, ab_ref, wq_ref, wk_ref, wv_ref, alog_ref, dtb_ref,
                         og_ref, o_ref, sout_ref, s_ref, pq_ref, pk_ref, pv_ref, *, tt):
    t = pl.program_id(2)
    c = DELTA_CHUNK
    n = tt // c
    d = HEAD_DIM

    @pl.when(t == 0)
    def _():
        s_ref[...] = jnp.zeros_like(s_ref)
        pq_ref[...] = jnp.zeros_like(pq_ref)
        pk_ref[...] = jnp.zeros_like(pk_ref)
        pv_ref[...] = jnp.zeros_like(pv_ref)

    row = lax.broadcasted_iota(jnp.int32, (tt, 1), 0)

    def conv(x_ref, p_ref, w_ref):
        x = x_ref[...]
        y = _short_conv_silu(x, p_ref[...], w_ref[...], row)
        p_ref[...] = x[tt - SUBLANES:, :]
        return y

    qc = conv(q_ref, pq_ref, wq_ref)
    kc = conv(k_ref, pk_ref, wk_ref)
    vc = conv(v_ref, pv_ref, wv_ref)

    ab = ab_ref[...]
    g_all = -jnp.exp(alog_ref[...]) * _softplus(ab + dtb_ref[...])
    beta_all = jax.nn.sigmoid(ab)
    gc_all = _block_cumsum(g_all, c)

    terms = []
    for hh in range(HEAD_GROUP):
        sl = slice(hh * d, (hh + 1) * d)
        qh = (_l2norm(qc[:, sl]) * (d ** -0.5)).reshape(n, c, d)
        kh = _l2norm(kc[:, sl]).reshape(n, c, d)
        vh = vc[:, sl].reshape(n, c, d)
        gcol = gc_all[:, hh:hh + 1].reshape(n, c, 1)
        bcol = beta_all[:, HEAD_GROUP + hh:HEAD_GROUP + hh + 1].reshape(n, c, 1)
        terms.append(_delta_chunk_terms(qh, kh, vh, gcol, bcol, c))

    gain = og_ref[...]
    states = [s_ref[hh] for hh in range(HEAD_GROUP)]
    for ch in range(n):
        for hh in range(HEAD_GROUP):
            u, w, a_intra, q_dec, k_dec, g_last = terms[hh]
            s = states[hh]
            sb = s.astype(BF16)
            v_new = u[ch] - jnp.dot(w[ch], sb, preferred_element_type=F32)
            vb = v_new.astype(BF16)
            o = (jnp.dot(q_dec[ch], sb, preferred_element_type=F32)
                 + jnp.dot(a_intra[ch], vb, preferred_element_type=F32))
            states[hh] = s * g_last[ch] + lax.dot_general(
                k_dec[ch], vb, (((0,), (0,)), ((), ())), preferred_element_type=F32)
            zz = z_ref[ch * c:(ch + 1) * c, hh * d:(hh + 1) * d]
            o_ref[ch * c:(ch + 1) * c, hh * d:(hh + 1) * d] = _rms(o, gain) * _silu(zz)
    for hh in range(HEAD_GROUP):
        s_ref[hh] = states[hh]
        sout_ref[hh] = states[hh]


def _delta_prompt(proj, batch, seq, conv_w, alog_rows, dtb_rows, out_gain):
    tt = min(512, seq)
    nt = seq // tt
    gw = HEAD_GROUP * HEAD_DIM
    ng = N_HEAD_GROUPS
    assert seq % tt == 0 and tt % DELTA_CHUNK == 0

    def col(base):
        return pl.BlockSpec((tt, gw), lambda b, g, t: (b * nt + t, base * ng + g))

    def wcol(base):
        return pl.BlockSpec((CONV_W, gw), lambda b, g, t: (0, base * ng + g))

    ab_base = (4 * DELTA_W + X_W) // LANES
    return pl.pallas_call(
        functools.partial(_delta_prompt_kernel, tt=tt),
        grid=(batch, ng, nt),
        in_specs=[
            col(0), col(1), col(2), col(3),
            pl.BlockSpec((tt, LANES), lambda b, g, t: (b * nt + t, ab_base + g)),
            wcol(0), wcol(1), wcol(2),
            pl.BlockSpec((1, LANES), lambda b, g, t: (0, g)),
            pl.BlockSpec((1, LANES), lambda b, g, t: (0, g)),
            pl.BlockSpec((1, HEAD_DIM), lambda b, g, t: (0, 0)),
        ],
        out_specs=[
            pl.BlockSpec((tt, gw), lambda b, g, t: (b * nt + t, g)),
            pl.BlockSpec((None, HEAD_GROUP, HEAD_DIM, HEAD_DIM), lambda b, g, t: (b, g, 0, 0)),
        ],
        out_shape=[
            jax.ShapeDtypeStruct((batch * seq, DELTA_W), F32),
            jax.ShapeDtypeStruct((batch, DELTA_HEADS, HEAD_DIM, HEAD_DIM), F32),
        ],
        scratch_shapes=[
            pltpu.VMEM((HEAD_GROUP, HEAD_DIM, HEAD_DIM), F32),
            pltpu.VMEM((SUBLANES, gw), F32),
            pltpu.VMEM((SUBLANES, gw), F32),
            pltpu.VMEM((SUBLANES, gw), F32),
        ],
        compiler_params=_params("parallel", "parallel", "arbitrary"),
    )(proj, proj, proj, proj, proj, conv_w, conv_w, conv_w, alog_rows, dtb_rows, out_gain)


def _delta_sample_kernel(qkv_ref, z_ref, ab_ref, buf_ref, s0_ref, w_ref, alog_ref, dtb_ref, og_ref,
                         o_ref, sout_ref, *, bb, seq):
    d = HEAD_DIM
    c = seq
    x = qkv_ref[...].reshape(bb, seq, 3 * DELTA_W)
    row = lax.broadcasted_iota(jnp.int32, (1, seq, 1), 1)
    y = _short_conv_silu(x, buf_ref[...], w_ref[...], row)

    ab = ab_ref[...]
    g_all = -jnp.exp(alog_ref[...]) * _softplus(ab + dtb_ref[...])
    beta_all = jax.nn.sigmoid(ab)
    gc_all = _block_cumsum(g_all, c)
    gain = og_ref[...]

    for h in range(DELTA_HEADS):
        lane_a = (h // HEAD_GROUP) * LANES + h % HEAD_GROUP
        lane_b = lane_a + HEAD_GROUP
        qh = _l2norm(y[:, :, h * d:(h + 1) * d]) * (d ** -0.5)
        kh = _l2norm(y[:, :, DELTA_W + h * d:DELTA_W + (h + 1) * d])
        vh = y[:, :, 2 * DELTA_W + h * d:2 * DELTA_W + (h + 1) * d]
        gcol = gc_all[:, lane_a:lane_a + 1].reshape(bb, c, 1)
        bcol = beta_all[:, lane_b:lane_b + 1].reshape(bb, c, 1)
        u, w, a_intra, q_dec, k_dec, g_last = _delta_chunk_terms(qh, kh, vh, gcol, bcol, c)
        s = s0_ref[:, h]
        sb = s.astype(BF16)
        v_new = u - jnp.einsum("bcd,bde->bce", w, sb, preferred_element_type=F32)
        vb = v_new.astype(BF16)
        o = (jnp.einsum("bcd,bde->bce", q_dec, sb, preferred_element_type=F32)
             + jnp.einsum("bij,bje->bie", a_intra, vb, preferred_element_type=F32))
        sout_ref[:, h] = s * g_last + jnp.einsum("bcd,bce->bde", k_dec, vb, preferred_element_type=F32)
        zz = z_ref[:, h * d:(h + 1) * d].reshape(bb, seq, d)
        o_ref[:, h * d:(h + 1) * d] = (_rms(o, gain) * _silu(zz)).reshape(bb * seq, d)


def _delta_sample(proj, batch, seq, conv_bufs, states, j, conv_w, alog_rows, dtb_rows, out_gain):
    bb = 8
    assert batch % bb == 0 and seq == SUBLANES
    rows = bb * seq
    qkv_w = 3 * DELTA_W
    return pl.pallas_call(
        functools.partial(_delta_sample_kernel, bb=bb, seq=seq),
        grid=(batch // bb,),
        in_specs=[
            pl.BlockSpec((rows, qkv_w), lambda i: (i, 0)),
            pl.BlockSpec((rows, DELTA_W), lambda i: (i, qkv_w // DELTA_W)),
            pl.BlockSpec((rows, AB_W), lambda i: (i, (4 * DELTA_W + X_W) // AB_W)),
            pl.BlockSpec((None, bb, CONV_W - 1, qkv_w), lambda i: (j, i, 0, 0)),
            pl.BlockSpec((None, bb, DELTA_HEADS, HEAD_DIM, HEAD_DIM), lambda i: (j, i, 0, 0, 0)),
            pl.BlockSpec((CONV_W, qkv_w), lambda i: (0, 0)),
            pl.BlockSpec((1, AB_W), lambda i: (0, 0)),
            pl.BlockSpec((1, AB_W), lambda i: (0, 0)),
            pl.BlockSpec((1, HEAD_DIM), lambda i: (0, 0)),
        ],
        out_specs=[
            pl.BlockSpec((rows, DELTA_W), lambda i: (i, 0)),
            pl.BlockSpec((bb, DELTA_HEADS, HEAD_DIM, HEAD_DIM), lambda i: (i, 0, 0, 0)),
        ],
        out_shape=[
            jax.ShapeDtypeStruct((batch * seq, DELTA_W), F32),
            jax.ShapeDtypeStruct((batch, DELTA_HEADS, HEAD_DIM, HEAD_DIM), F32),
        ],
        compiler_params=_params("parallel"),
    )(proj, proj, proj, conv_bufs, states, conv_w, alog_rows, dtb_rows, out_gain)


def _gelu(x):
    return 0.5 * x * (1.0 + lax.erf(x * (2.0 ** -0.5)))


def _layernorm(x, gain, bias):
    mu = jnp.mean(x, axis=-1, keepdims=True)
    xc = x - mu
    var = jnp.mean(xc * xc, axis=-1, keepdims=True)
    return xc * lax.rsqrt(var + EPS) * gain + bias


def _gmlp_prompt_kernel(u_ref, v_ref, lng_ref, lnb_ref, ws_ref, bs_ref, o_ref, *, chunks):
    c = GMLP_CHUNK
    d = HEAD_DIM
    ri = lax.broadcasted_iota(jnp.int32, (c, c), 0)
    ci = lax.broadcasted_iota(jnp.int32, (c, c), 1)
    lower = ci <= ri
    v = _layernorm(_gelu(v_ref[...]), lng_ref[...], lnb_ref[...]).astype(BF16)
    bs = bs_ref[...]
    for g in range(GMLP_GROUPS):
        w = jnp.where(lower, ws_ref[g], 0.0).astype(BF16)
        bias = bs[:, g:g + 1]
        for ch in range(chunks):
            rs = slice(ch * c, (ch + 1) * c)
            cs = slice(g * d, (g + 1) * d)
            mixed = jnp.dot(w, v[rs, cs], preferred_element_type=F32) + bias
            o_ref[rs, cs] = _gelu(u_ref[rs, cs]) * mixed


def _gmlp_prompt(proj, ln_gain, ln_bias, w_s, b_s_t):
    m = proj.shape[0]
    chunks = 2
    tm = chunks * GMLP_CHUNK
    assert m % tm == 0
    return pl.pallas_call(
        functools.partial(_gmlp_prompt_kernel, chunks=chunks),
        grid=(m // tm,),
        in_specs=[
            pl.BlockSpec((tm, GMLP_W), lambda i: (i, 0)),
            pl.BlockSpec((tm, GMLP_W), lambda i: (i, 1)),
            pl.BlockSpec((1, GMLP_W), lambda i: (0, 0)),
            pl.BlockSpec((1, GMLP_W), lambda i: (0, 0)),
            pl.BlockSpec((GMLP_GROUPS, GMLP_CHUNK, GMLP_CHUNK), lambda i: (0, 0, 0)),
            pl.BlockSpec((GMLP_CHUNK, GMLP_GROUPS), lambda i: (0, 0)),
        ],
        out_specs=pl.BlockSpec((tm, GMLP_W), lambda i: (i, 0)),
        out_shape=jax.ShapeDtypeStruct((m, GMLP_W), F32),
        compiler_params=_params("parallel"),
    )(proj, proj, ln_gain, ln_bias, w_s, b_s_t)


def _gmlp_sample_kernel(u_ref, v_ref, lng_ref, lnb_ref, wx_ref, bx_ref, o_ref, vout_ref, *, bb, seq):
    v = _layernorm(_gelu(v_ref[...]), lng_ref[...], lnb_ref[...])
    vout_ref[...] = v
    v3 = v.reshape(bb, seq, GMLP_W)
    row = lax.broadcasted_iota(jnp.int32, (seq, 1), 0)
    mixed = jnp.broadcast_to(bx_ref[...][None], (bb, seq, GMLP_W))
    for j in range(seq):
        wj = jnp.where(row >= j, wx_ref[j], 0.0)
        mixed = mixed + wj[None] * v3[:, j:j + 1, :]
    o_ref[...] = _gelu(u_ref[...]) * mixed.reshape(bb * seq, GMLP_W)


def _gmlp_sample(proj, batch, seq, ln_gain, ln_bias, w_exp, b_exp):
    bb = 32
    assert batch % bb == 0 and seq == SUBLANES
    rows = bb * seq
    return pl.pallas_call(
        functools.partial(_gmlp_sample_kernel, bb=bb, seq=seq),
        grid=(batch // bb,),
        in_specs=[
            pl.BlockSpec((rows, GMLP_W), lambda i: (i, 0)),
            pl.BlockSpec((rows, GMLP_W), lambda i: (i, 1)),
            pl.BlockSpec((1, GMLP_W), lambda i: (0, 0)),
            pl.BlockSpec((1, GMLP_W), lambda i: (0, 0)),
            pl.BlockSpec((seq, seq, GMLP_W), lambda i: (0, 0, 0)),
            pl.BlockSpec((seq, GMLP_W), lambda i: (0, 0)),
        ],
        out_specs=[
            pl.BlockSpec((rows, GMLP_W), lambda i: (i, 0)),
            pl.BlockSpec((rows, GMLP_W), lambda i: (i, 0)),
        ],
        out_shape=[
            jax.ShapeDtypeStruct((batch * seq, GMLP_W), F32),
            jax.ShapeDtypeStruct((batch * seq, GMLP_W), F32),
        ],
        compiler_params=_params("parallel"),
    )(proj, proj, ln_gain, ln_bias, w_exp, b_exp)


def _softmax(s):
    m = jnp.max(s, axis=-1, keepdims=True)
    e = jnp.exp(s - m)
    return e / jnp.sum(e, axis=-1, keepdims=True)


def _mem_attn_prompt_kernel(q_ref, k_ref, v_ref, o_ref):
    d = HEAD_DIM
    for h in range(X_HEADS):
        sl = slice(h * d, (h + 1) * d)
        s = lax.dot_general(q_ref[:, sl].astype(BF16), k_ref[:, sl].astype(BF16),
                            (((1,), (1,)), ((), ())), preferred_element_type=F32) * (d ** -0.5)
        o_ref[:, sl] = _bdot(_softmax(s), v_ref[:, sl])


def _mem_attn_prompt(proj, q_block, batch, seq, mem_k, mem_v):
    tt = min(512, seq)
    nt = seq // tt
    n_mem = mem_k.shape[1]
    return pl.pallas_call(
        _mem_attn_prompt_kernel,
        grid=(batch, nt),
        in_specs=[
            pl.BlockSpec((tt, X_W), lambda b, t: (b * nt + t, q_block)),
            pl.BlockSpec((None, n_mem, X_W), lambda b, t: (b, 0, 0)),
            pl.BlockSpec((None, n_mem, X_W), lambda b, t: (b, 0, 0)),
        ],
        out_specs=pl.BlockSpec((tt, X_W), lambda b, t: (b * nt + t, 0)),
        out_shape=jax.ShapeDtypeStruct((batch * seq, X_W), F32),
        compiler_params=_params("parallel", "parallel"),
    )(proj, mem_k, mem_v)


def _mem_attn_sample_kernel(q_ref, k_ref, v_ref, o_ref, *, bb, seq):
    d = HEAD_DIM
    nq = X_HEADS * seq
    nk = k_ref.shape[1]
    q = jnp.concatenate([q_ref[:, h * d:(h + 1) * d].reshape(bb, seq, d) for h in range(X_HEADS)], axis=1)
    s = jnp.einsum("bqd,bkd->bqk", q.astype(BF16), k_ref[...].astype(BF16),
                   preferred_element_type=F32) * (d ** -0.5)
    q_head = lax.broadcasted_iota(jnp.int32, (nq, nk), 0) // seq
    k_head = lax.broadcasted_iota(jnp.int32, (nq, nk), 1) % X_HEADS
    p = _softmax(jnp.where(q_head == k_head, s, -1e30)).astype(BF16)
    o = jnp.einsum("bqk,bkd->bqd", p, v_ref[...].astype(BF16), preferred_element_type=F32)
    for h in range(X_HEADS):
        o_ref[:, h * d:(h + 1) * d] = o[:, h * seq:(h + 1) * seq, :].reshape(bb * seq, d)


def _mem_attn_sample(proj, q_block, batch, seq, mem_k, mem_v, layer):
    bb = 8
    rows = bb * seq
    n_rows = mem_k.shape[2]
    return pl.pallas_call(
        functools.partial(_mem_attn_sample_kernel, bb=bb, seq=seq),
        grid=(batch // bb,),
        in_specs=[
            pl.BlockSpec((rows, X_W), lambda i: (i, q_block)),
            pl.BlockSpec((None, bb, n_rows, HEAD_DIM), lambda i: (layer, i, 0, 0)),
            pl.BlockSpec((None, bb, n_rows, HEAD_DIM), lambda i: (layer, i, 0, 0)),
        ],
        out_specs=pl.BlockSpec((rows, X_W), lambda i: (i, 0)),
        out_shape=jax.ShapeDtypeStruct((batch * seq, X_W), F32),
        compiler_params=_params("parallel"),
    )(proj, mem_k, mem_v)


def _delta_in_weight(w_in_a):
    d_model = w_in_a.shape[0]
    qkvz = w_in_a[:, :4 * DELTA_W]
    a = w_in_a[:, 4 * DELTA_W:4 * DELTA_W + DELTA_HEADS]
    b = w_in_a[:, 4 * DELTA_W + DELTA_HEADS:4 * DELTA_W + 2 * DELTA_HEADS]
    xq = w_in_a[:, 4 * DELTA_W + 2 * DELTA_HEADS:]
    parts = [qkvz, xq]
    pad = jnp.zeros((d_model, LANES - 2 * HEAD_GROUP), w_in_a.dtype)
    for g in range(N_HEAD_GROUPS):
        hs = slice(g * HEAD_GROUP, (g + 1) * HEAD_GROUP)
        parts += [a[:, hs], b[:, hs], pad]
    parts.append(jnp.zeros((d_model, AB_W - N_HEAD_GROUPS * LANES), w_in_a.dtype))
    return jnp.concatenate(parts, axis=1).astype(BF16)


def _gate_rows(vec):
    row = jnp.zeros((N_HEAD_GROUPS, LANES), F32)
    row = row.at[:, :HEAD_GROUP].set(vec.astype(F32).reshape(N_HEAD_GROUPS, HEAD_GROUP))
    return jnp.concatenate([row.reshape(1, -1), jnp.zeros((1, AB_W - N_HEAD_GROUPS * LANES), F32)], axis=1)


def _trunk(x, batch, seq, is_prompt, mem_k, mem_v, conv_bufs, delta_states, p):
    new_conv, new_delta, new_v = [], [], []
    gains = p["gains"]
    depth = p["w_out"].shape[0]
    for i in range(depth):
        j = i // N_MIXERS
        x = _ffn(x, gains, p["w_ffn_gu"], p["w_ffn_dn"], i, 0, 0, 1)
        if i % N_MIXERS == 0:
            proj = _norm_matmul(x, gains, i * 6 + 2, p["w_in_a"][j], 1024)
            q_block = 4 * DELTA_W // X_W
            if is_prompt:
                mix, s_new = _delta_prompt(proj, batch, seq, p["conv_w"][j], p["alog_rows"][j],
                                           p["dtb_rows"][j], p["delta_norm_gain"][j])
            else:
                mix, s_new = _delta_sample(proj, batch, seq, conv_bufs, delta_states, j, p["conv_w"][j],
                                           p["alog_rows"][j], p["dtb_rows"][j], p["delta_norm_gain"][j])
            new_conv.append(proj.reshape(batch, seq, -1)[:, seq - (CONV_W - 1):, :3 * DELTA_W])
            new_delta.append(s_new)
        else:
            proj = _norm_matmul(x, gains, i * 6 + 2, p["w_in_b"][j], 896)
            q_block = 2 * GMLP_W // X_W
            if is_prompt:
                mix = _gmlp_prompt(proj, p["gmlp_ln_gain"][j], p["gmlp_ln_bias"][j], p["w_spatial"][j],
                                   p["b_spatial_t"][j])
            else:
                mix, v_rows = _gmlp_sample(proj, batch, seq, p["gmlp_ln_gain"][j], p["gmlp_ln_bias"][j],
                                           p["w_spatial_exp"][j], p["b_spatial_exp"][j])
                new_v.append(v_rows.reshape(batch, seq, GMLP_W))
        if is_prompt:
            mem_out = _mem_attn_prompt(proj, q_block, batch, seq, mem_k[i], mem_v[i])
        else:
            mem_out = _mem_attn_sample(proj, q_block, batch, seq, mem_k, mem_v, i)
        x = _out_proj(x, mix, mem_out, p["w_out"], gains, i)
        x = _ffn(x, gains, p["w_ffn_gu"], p["w_ffn_dn"], i, 1, 4, 5)
    return x, new_conv, new_delta, new_v


def kernel(x_prompt, x_sample, mem_prompt, cache_mem_k, cache_mem_v, state_delta, state_conv, norm_gains, w_ffn_gu, w_ffn_dn, w_in_a, conv_w, a_log, dt_bias, delta_norm_gain, w_in_b, gmlp_ln_gain, gmlp_ln_bias, w_spatial, b_spatial, mem_norm_gain, w_mem_kv, w_out):
    batch, seq, d_model = x_prompt.shape
    dec_batch, dec_seq, _ = x_sample.shape
    depth = w_out.shape[0]
    n_mem = mem_prompt.shape[1]
    n_a = w_in_a.shape[0]
    n_b = w_in_b.shape[0]

    w_s_dec = w_spatial[:, :, :dec_seq, :dec_seq]
    p = {
        "gains": norm_gains.reshape(depth * 6, 1, d_model),
        "w_ffn_gu": w_ffn_gu.astype(BF16),
        "w_ffn_dn": w_ffn_dn.astype(BF16),
        "w_in_a": [_delta_in_weight(w_in_a[j]) for j in range(n_a)],
        "w_in_b": [w_in_b[j].astype(BF16) for j in range(n_b)],
        "w_out": w_out.astype(BF16),
        "conv_w": conv_w,
        "alog_rows": [_gate_rows(a_log[j]) for j in range(n_a)],
        "dtb_rows": [_gate_rows(dt_bias[j]) for j in range(n_a)],
        "delta_norm_gain": delta_norm_gain.reshape(n_a, 1, HEAD_DIM),
        "gmlp_ln_gain": gmlp_ln_gain.reshape(n_b, 1, GMLP_W),
        "gmlp_ln_bias": gmlp_ln_bias.reshape(n_b, 1, GMLP_W),
        "w_spatial": w_spatial,
        "b_spatial_t": jnp.swapaxes(b_spatial, 1, 2),
        "w_spatial_exp": jnp.repeat(jnp.transpose(w_s_dec, (0, 3, 2, 1)), HEAD_DIM, axis=-1),
        "b_spatial_exp": jnp.repeat(jnp.swapaxes(b_spatial[:, :, :dec_seq], 1, 2), HEAD_DIM, axis=-1),
    }

    mem2d = mem_prompt.reshape(batch * n_mem, d_model)
    mem_gains = mem_norm_gain.reshape(depth, 1, d_model)
    w_kv = w_mem_kv.astype(BF16)
    kvs = [_norm_matmul(mem2d, mem_gains, i, w_kv[i], 1024) for i in range(depth)]
    mem_k_prompt = jnp.stack([kv[:, :X_W].reshape(batch, n_mem, X_W) for kv in kvs])
    mem_v_prompt = jnp.stack([kv[:, X_W:].reshape(batch, n_mem, X_W) for kv in kvs])

    y_prompt, conv_p, delta_p, _ = _trunk(x_prompt.reshape(batch * seq, d_model), batch, seq, True,
                                          mem_k_prompt, mem_v_prompt, None, None, p)

    y_sample, conv_s, delta_s, v_s = _trunk(
        x_sample.reshape(dec_batch * dec_seq, d_model), dec_batch, dec_seq, False,
        cache_mem_k.reshape(depth, dec_batch, n_mem * X_HEADS, HEAD_DIM),
        cache_mem_v.reshape(depth, dec_batch, n_mem * X_HEADS, HEAD_DIM),
        state_conv, state_delta, p)

    return (y_prompt.reshape(batch, seq, d_model),
            y_sample.reshape(dec_batch, dec_seq, d_model),
            mem_k_prompt.reshape(depth, batch, n_mem, X_HEADS, HEAD_DIM),
            mem_v_prompt.reshape(depth, batch, n_mem, X_HEADS, HEAD_DIM),
            jnp.stack(delta_p), jnp.stack(conv_p), jnp.stack(delta_s), jnp.stack(conv_s), jnp.stack(v_s))
```

```python
import functools

import jax
import jax.numpy as jnp
from jax import lax
from jax.experimental import pallas as pl
from jax.experimental.pallas import tpu as pltpu

F32 = jnp.float32
BF16 = jnp.bfloat16

EPS = 1e-6
HEAD_DIM = 128
X_HEADS = 4
X_W = X_HEADS * HEAD_DIM
DELTA_HEADS = 12
DELTA_W = DELTA_HEADS * HEAD_DIM
CONV_W = 4
DELTA_CHUNK = 64
GMLP_GROUPS = 12
GMLP_W = GMLP_GROUPS * HEAD_DIM
GMLP_CHUNK = 128
N_MIXERS = 2

LANES = 128
SUBLANES = 8
VMEM_LIMIT = 56 * 1024 * 1024
FFN_CAST_VMEM_LIMIT = 60 * 1024 * 1024

HEAD_GROUP = 4
N_HEAD_GROUPS = DELTA_HEADS // HEAD_GROUP
FFN_ROWS = 1024
FFN_COLS = 512
PROJ_ROWS = 1024
AB_W = 512
IN_A_W = 4 * DELTA_W + X_W + AB_W


def _params(*sem, vmem=VMEM_LIMIT):
    return pltpu.CompilerParams(dimension_semantics=sem, vmem_limit_bytes=vmem)


def _rms(x, gain):
    ms = jnp.mean(x * x, axis=-1, keepdims=True)
    return x * lax.rsqrt(ms + EPS) * gain


def _sigmoid(x):
    return 0.5 * jnp.tanh(0.5 * x) + 0.5


def _silu(x):
    return x * _sigmoid(x)


def _softplus(x):
    return jnp.maximum(x, 0.0) + jnp.log1p(jnp.exp(-jnp.abs(x)))


def _bdot(a, b):
    return jnp.dot(a.astype(BF16), b.astype(BF16), preferred_element_type=F32)


def _ffn_cast_kernel(x_ref, gpre_ref, gpost_ref, wg_ref, wu_ref, wd_ref, cgu_ref, cdn_ref,
                     o_ref, ogu_ref, odn_ref, hn_ref):
    ogu_ref[...] = cgu_ref[...].astype(BF16)
    odn_ref[...] = cdn_ref[...].astype(BF16)
    _ffn_kernel(x_ref, gpre_ref, gpost_ref, wg_ref, wu_ref, wd_ref, o_ref, hn_ref)


def _ffn_kernel(x_ref, gpre_ref, gpost_ref, wg_ref, wu_ref, wd_ref, o_ref, hn_ref):
    f = pl.program_id(1)

    @pl.when(f == 0)
    def _():
        hn_ref[...] = _rms(x_ref[...], gpre_ref[...]).astype(BF16)
        o_ref[...] = jnp.zeros_like(o_ref)

    h = hn_ref[...]
    gate = jnp.dot(h, wg_ref[...], preferred_element_type=F32)
    up = jnp.dot(h, wu_ref[...], preferred_element_type=F32)
    act = (_silu(gate) * up).astype(BF16)
    o_ref[...] += jnp.dot(act, wd_ref[...], preferred_element_type=F32)

    @pl.when(f == pl.num_programs(1) - 1)
    def _():
        o_ref[...] = x_ref[...] + 0.5 * _rms(o_ref[...], gpost_ref[...])


def _ffn(x, gains, gain_base, pre, post, w_gu, w_dn, cast_next=None):
    m, d = x.shape
    ffn = w_dn.shape[0]
    tm = min(FFN_ROWS, m)
    tf = FFN_COLS
    nf = ffn // tf
    nm = m // tm
    assert m % tm == 0 and ffn % tf == 0
    in_specs = [
        pl.BlockSpec((tm, d), lambda i, f: (i, 0), pipeline_mode=pl.Buffered(1)),
        pl.BlockSpec((None, 1, d), lambda i, f: (gain_base + pre, 0, 0)),
        pl.BlockSpec((None, 1, d), lambda i, f: (gain_base + post, 0, 0)),
        pl.BlockSpec((d, tf), lambda i, f: (0, f)),
        pl.BlockSpec((d, tf), lambda i, f: (0, nf + f)),
        pl.BlockSpec((tf, d), lambda i, f: (f, 0)),
    ]
    out_specs = pl.BlockSpec((tm, d), lambda i, f: (i, 0))
    out_shape = jax.ShapeDtypeStruct((m, d), F32)
    scratch = [pltpu.VMEM((tm, d), BF16)]
    if cast_next is None:
        return pl.pallas_call(
            _ffn_kernel, grid=(nm, nf), in_specs=in_specs, out_specs=out_specs, out_shape=out_shape,
            scratch_shapes=scratch, compiler_params=_params("parallel", "arbitrary"),
        )(x, gains, gains, w_gu, w_gu, w_dn)
    gu_all, dn_all, layer, slot = cast_next
    gu_rows, gu_cols = d // nm, 2 * ffn // nf
    dn_rows = ffn // (nm * nf)
    assert gu_rows * nm == d and gu_cols * nf == 2 * ffn and dn_rows * nm * nf == ffn
    assert gu_rows % 16 == 0 and gu_cols % LANES == 0 and dn_rows % 16 == 0
    return pl.pallas_call(
        _ffn_cast_kernel,
        grid=(nm, nf),
        in_specs=in_specs + [
            pl.BlockSpec((None, None, gu_rows, gu_cols), lambda i, f: (layer, slot, i, f)),
            pl.BlockSpec((None, None, dn_rows, d), lambda i, f: (layer, slot, i * nf + f, 0)),
        ],
        out_specs=[
            out_specs,
            pl.BlockSpec((gu_rows, gu_cols), lambda i, f: (i, f)),
            pl.BlockSpec((dn_rows, d), lambda i, f: (i * nf + f, 0)),
        ],
        out_shape=[out_shape, jax.ShapeDtypeStruct((d, 2 * ffn), BF16), jax.ShapeDtypeStruct((ffn, d), BF16)],
        scratch_shapes=scratch,
        compiler_params=_params("parallel", "arbitrary", vmem=FFN_CAST_VMEM_LIMIT),
    )(x, gains, gains, w_gu, w_gu, w_dn, gu_all, dn_all)


def _norm_matmul_kernel(x_ref, g_ref, w_ref, o_ref, hn_ref):
    @pl.when(pl.program_id(1) == 0)
    def _():
        hn_ref[...] = _rms(x_ref[...], g_ref[...]).astype(BF16)

    o_ref[...] = jnp.dot(hn_ref[...], w_ref[...], preferred_element_type=F32)


def _norm_matmul(x, gains, gain_idx, w, tn):
    m, d = x.shape
    n = w.shape[1]
    tm = min(PROJ_ROWS, m)
    assert m % tm == 0 and n % tn == 0
    return pl.pallas_call(
        _norm_matmul_kernel,
        grid=(m // tm, n // tn),
        in_specs=[
            pl.BlockSpec((tm, d), lambda i, j: (i, 0)),
            pl.BlockSpec((None, 1, d), lambda i, j: (gain_idx, 0, 0)),
            pl.BlockSpec((d, tn), lambda i, j: (0, j)),
        ],
        out_specs=pl.BlockSpec((tm, tn), lambda i, j: (i, j)),
        out_shape=jax.ShapeDtypeStruct((m, n), F32),
        scratch_shapes=[pltpu.VMEM((tm, d), BF16)],
        compiler_params=_params("parallel", "arbitrary"),
    )(x, gains, w)


def _out_proj_kernel(x_ref, mix_ref, mem_ref, wmix_ref, wmem_ref, g_ref, o_ref):
    y = _bdot(mix_ref[...], wmix_ref[...]) + _bdot(mem_ref[...], wmem_ref[...])
    o_ref[...] = x_ref[...] + _rms(y, g_ref[...])


def _out_proj(x, mix, mem, w_out, gains, layer):
    m, d = x.shape
    wm = mix.shape[1]
    wx = mem.shape[1]
    tm = min(512, m)
    assert wm % wx == 0
    return pl.pallas_call(
        _out_proj_kernel,
        grid=(m // tm,),
        in_specs=[
            pl.BlockSpec((tm, d), lambda i: (i, 0)),
            pl.BlockSpec((tm, wm), lambda i: (i, 0)),
            pl.BlockSpec((tm, wx), lambda i: (i, 0)),
            pl.BlockSpec((None, wm, d), lambda i: (layer, 0, 0)),
            pl.BlockSpec((None, wx, d), lambda i: (layer, wm // wx, 0)),
            pl.BlockSpec((None, 1, d), lambda i: (layer * 6 + 3, 0, 0)),
        ],
        out_specs=pl.BlockSpec((tm, d), lambda i: (i, 0)),
        out_shape=jax.ShapeDtypeStruct((m, d), F32),
        compiler_params=_params("parallel"),
    )(x, mix, mem, w_out, w_out, gains)


def _unit_lower_inverse(lmat, c, span=None):
    span = c if span is None else span
    ri = lax.broadcasted_iota(jnp.int32, (c, c), 0)
    ci = lax.broadcasted_iota(jnp.int32, (c, c), 1)
    eye = (ri == ci).astype(F32)
    base = min(span, 16)
    same = (ri // base) == (ci // base)
    dm = jnp.where(same, lmat, 0.0)
    x = eye - dm
    p = _split(dm)
    width = 1
    while 2 * width < base:
        p = _split(_mm3(p, p))
        x = x + _mm3(_split(x), p)
        width *= 2
    size = base
    while size < span:
        inner = ((ri // (2 * size)) == (ci // (2 * size))) & ((ri // size) != (ci // size))
        xs = _split(x)
        x = x - _mm3(_split(_mm3(xs, _split(jnp.where(inner, lmat, 0.0)))), xs)
        size *= 2
    return x


def _split(x):
    hi = x.astype(BF16)
    return hi, (x - hi.astype(F32)).astype(BF16)


def _mm3(a, b):
    def mm(p, q):
        return jnp.einsum("nij,njk->nik", p, q, preferred_element_type=F32)

    width = b[0].shape[-1]
    if a[0].shape[-1] % LANES == 0 and width % LANES == 0:
        lhs = jnp.concatenate([a[0], a[1]], axis=-1)
        rhs = jnp.concatenate([jnp.concatenate([b[0], b[1]], axis=-1),
                               jnp.concatenate([b[0], jnp.zeros_like(b[0])], axis=-1)], axis=-2)
        r = mm(lhs, rhs)
        return r[..., :width] + r[..., width:]
    return mm(a[0], b[0]) + (mm(a[0], b[1]) + mm(a[1], b[0]))


def _delta_pair_terms(q, k, v, gcol, bcol):
    n, cp, d = q.shape
    c = DELTA_CHUNK
    wu, a_intra, egc = _wy_solve(q, k, v, gcol, bcol, c)
    wu = wu.astype(BF16)
    awu = jnp.einsum("nij,nje->nie", a_intra, wu, preferred_element_type=F32)
    qt = (q * egc - awu[..., :d]).astype(BF16)
    o0 = awu[..., d:]
    k64 = k.reshape(2 * n, c, d)
    g64 = gcol.reshape(2 * n, c, 1)
    glast = g64[:, c - 1:c, :]
    k_dec = (k64 * jnp.exp(glast - g64)).astype(BF16)
    mb = jnp.einsum("ncd,nce->nde", k_dec, wu.reshape(2 * n, c, 2 * d), preferred_element_type=F32)
    return (qt.reshape(2 * n, c, d), mb[..., :d].astype(BF16), mb[..., d:], o0.reshape(2 * n, c, d),
            jnp.exp(glast))


def _wy_solve(q, k, v, gcol, bcol, span):
    rows = q.shape[1]
    ri = lax.broadcasted_iota(jnp.int32, (rows, rows), 0)
    ci = lax.broadcasted_iota(jnp.int32, (rows, rows), 1)
    eye = (ri == ci).astype(F32)
    same = (ri // span) == (ci // span)
    strict = same & (ci < ri)
    causal = same & (ci <= ri)
    grow = jnp.sum(gcol * eye, axis=1, keepdims=True)
    diff = gcol - grow
    kb = k.astype(BF16)
    kk = jnp.einsum("ncd,nmd->ncm", kb, kb, preferred_element_type=F32)
    lmat = jnp.where(strict, bcol * kk * jnp.exp(jnp.where(strict, diff, 0.0)), 0.0)
    tinv = _unit_lower_inverse(lmat, rows, span)
    egc = jnp.exp(gcol)
    rhs = jnp.concatenate([k * (bcol * egc), v * bcol], axis=-1)
    wu = _mm3(_split(tinv), _split(rhs))
    qk = jnp.einsum("ncd,nmd->ncm", q.astype(BF16), kb, preferred_element_type=F32)
    a_intra = jnp.where(causal, qk * jnp.exp(jnp.where(causal, diff, 0.0)), 0.0).astype(BF16)
    return wu, a_intra, egc


def _l2norm(x):
    return x * lax.rsqrt(jnp.sum(x * x, axis=-1, keepdims=True) + EPS)


def _block_cumsum(g, c):
    rows, lanes = g.shape
    n = rows // c
    ri = lax.broadcasted_iota(jnp.int32, (n, c, c), 1)
    ci = lax.broadcasted_iota(jnp.int32, (n, c, c), 2)
    tri = (ci <= ri).astype(BF16)
    g3 = g.reshape(n, c, lanes)
    hi = g3.astype(BF16)
    r1 = g3 - hi.astype(F32)
    mid = r1.astype(BF16)
    lo = (r1 - mid.astype(F32)).astype(BF16)

    def mm(p):
        return jnp.einsum("nij,njk->nik", tri, p, preferred_element_type=F32)

    return (mm(hi) + (mm(mid) + mm(lo))).reshape(rows, lanes)


def _shifted(x, prev, shift, row):
    xs = pltpu.roll(x, shift, x.ndim - 2)
    if x.ndim == 2:
        ps = pltpu.roll(prev, shift, 0)
        reps = x.shape[0] // SUBLANES
        ps = jnp.broadcast_to(ps[None], (reps,) + ps.shape).reshape(x.shape)
        return jnp.where(row < shift, ps, xs)
    for t in range(shift):
        src = prev.shape[1] - shift + t
        xs = jnp.where(row == t, prev[:, src:src + 1, :], xs)
    return xs


def _short_conv_silu(x, prev, w, row):
    y = x * w[CONV_W - 1:CONV_W]
    for shift in range(1, CONV_W):
        tap = CONV_W - 1 - shift
        y = y + _shifted(x, prev, shift, row) * w[tap:tap + 1]
    return _silu(y)


def _delta_prompt_kernel(q_ref, k_ref, v_ref, z_ref, ab_ref, wq_ref, wk_ref, wv_ref, alog_ref, dtb_ref,
                         og_ref, o_ref, sout_ref, s_ref, pq_ref, pk_ref, pv_ref, *, tt):
    t = pl.program_id(2)
    c = DELTA_CHUNK
    n = tt // c
    d = HEAD_DIM

    @pl.when(t == 0)
    def _():
        s_ref[...] = jnp.zeros_like(s_ref)
        pq_ref[:SUBLANES, :] = jnp.zeros((SUBLANES, pq_ref.shape[1]), F32)
        pk_ref[:SUBLANES, :] = jnp.zeros((SUBLANES, pk_ref.shape[1]), F32)
        pv_ref[:SUBLANES, :] = jnp.zeros((SUBLANES, pv_ref.shape[1]), F32)

    def conv(x_ref, p_ref, w_ref):
        p_ref[SUBLANES:, :] = x_ref[...]
        w = w_ref[...]
        y = x_ref[...] * w[CONV_W - 1:CONV_W]
        for shift in range(1, CONV_W):
            tap = CONV_W - 1 - shift
            y = y + p_ref[SUBLANES - shift:SUBLANES - shift + tt, :] * w[tap:tap + 1]
        p_ref[:SUBLANES, :] = x_ref[tt - SUBLANES:, :]
        return _silu(y)

    qc = conv(q_ref, pq_ref, wq_ref)
    kc = conv(k_ref, pk_ref, wk_ref)
    vc = conv(v_ref, pv_ref, wv_ref)

    ab = ab_ref[...]
    g_all = -jnp.exp(alog_ref[...]) * _softplus(ab + dtb_ref[...])
    beta_all = _sigmoid(ab)
    gc_all = _block_cumsum(g_all, c)

    terms = []
    for hh in range(HEAD_GROUP):
        sl = slice(hh * d, (hh + 1) * d)
        qh = (_l2norm(qc[:, sl]) * (d ** -0.5)).reshape(n // 2, 2 * c, d)
        kh = _l2norm(kc[:, sl]).reshape(n // 2, 2 * c, d)
        vh = vc[:, sl].reshape(n // 2, 2 * c, d)
        gcol = gc_all[:, hh:hh + 1].reshape(n // 2, 2 * c, 1)
        bcol = beta_all[:, HEAD_GROUP + hh:HEAD_GROUP + hh + 1].reshape(n // 2, 2 * c, 1)
        terms.append(_delta_pair_terms(qh, kh, vh, gcol, bcol))

    gain = og_ref[...]
    states = [s_ref[hh] for hh in range(HEAD_GROUP)]
    for ch in range(n):
        for hh in range(HEAD_GROUP):
            qt, m, bmat, o0, g_last = terms[hh]
            s = states[hh]
            r = jnp.dot(jnp.concatenate([qt[ch], m[ch]], axis=0), s.astype(BF16),
                        preferred_element_type=F32)
            o = r[:c] + o0[ch]
            states[hh] = s * g_last[ch] + (bmat[ch] - r[c:])
            zz = z_ref[ch * c:(ch + 1) * c, hh * d:(hh + 1) * d]
            o_ref[ch * c:(ch + 1) * c, hh * d:(hh + 1) * d] = _rms(o, gain) * _silu(zz)
    for hh in range(HEAD_GROUP):
        s_ref[hh] = states[hh]
        sout_ref[hh] = states[hh]


def _delta_prompt(proj, batch, seq, conv_w, alog_rows, dtb_rows, out_gain):
    tt = min(512, seq)
    nt = seq // tt
    gw = HEAD_GROUP * HEAD_DIM
    ng = N_HEAD_GROUPS
    assert seq % tt == 0 and tt % DELTA_CHUNK == 0

    def col(base):
        return pl.BlockSpec((tt, gw), lambda b, g, t: (b * nt + t, base * ng + g))

    def wcol(base):
        return pl.BlockSpec((CONV_W, gw), lambda b, g, t: (0, base * ng + g))

    ab_base = (4 * DELTA_W + X_W) // LANES
    return pl.pallas_call(
        functools.partial(_delta_prompt_kernel, tt=tt),
        grid=(batch, ng, nt),
        in_specs=[
            col(0), col(1), col(2), col(3),
            pl.BlockSpec((tt, LANES), lambda b, g, t: (b * nt + t, ab_base + g)),
            wcol(0), wcol(1), wcol(2),
            pl.BlockSpec((1, LANES), lambda b, g, t: (0, g)),
            pl.BlockSpec((1, LANES), lambda b, g, t: (0, g)),
            pl.BlockSpec((1, HEAD_DIM), lambda b, g, t: (0, 0)),
        ],
        out_specs=[
            pl.BlockSpec((tt, gw), lambda b, g, t: (b * nt + t, g)),
            pl.BlockSpec((None, HEAD_GROUP, HEAD_DIM, HEAD_DIM), lambda b, g, t: (b, g, 0, 0)),
        ],
        out_shape=[
            jax.ShapeDtypeStruct((batch * seq, DELTA_W), F32),
            jax.ShapeDtypeStruct((batch, DELTA_HEADS, HEAD_DIM, HEAD_DIM), F32),
        ],
        scratch_shapes=[
            pltpu.VMEM((HEAD_GROUP, HEAD_DIM, HEAD_DIM), F32),
            pltpu.VMEM((SUBLANES + tt, gw), F32),
            pltpu.VMEM((SUBLANES + tt, gw), F32),
            pltpu.VMEM((SUBLANES + tt, gw), F32),
        ],
        compiler_params=_params("parallel", "parallel", "arbitrary"),
    )(proj, proj, proj, proj, proj, conv_w, conv_w, conv_w, alog_rows, dtb_rows, out_gain)


def _delta_sample_kernel(q_ref, k_ref, v_ref, z_ref, ab_ref, bq_ref, bk_ref, bv_ref, s0_ref, wq_ref, wk_ref,
                         wv_ref, alog_ref, dtb_ref, og_ref, o_ref, sout_ref, *, bb, seq):
    d = HEAD_DIM
    rows = bb * seq
    row = lax.broadcasted_iota(jnp.int32, (1, seq, 1), 1)

    def conv(x_ref, buf_ref, w_ref):
        x = x_ref[...].reshape(bb, seq, x_ref.shape[1])
        return _short_conv_silu(x, buf_ref[...], w_ref[...], row).reshape(rows, x_ref.shape[1])

    qc = conv(q_ref, bq_ref, wq_ref)
    kc = conv(k_ref, bk_ref, wk_ref)
    vc = conv(v_ref, bv_ref, wv_ref)

    ab = ab_ref[...]
    g_all = -jnp.exp(alog_ref[...]) * _softplus(ab + dtb_ref[...])
    beta_all = _sigmoid(ab)
    gc_all = _block_cumsum(g_all, seq)

    heads = range(HEAD_GROUP)
    q = jnp.stack([_l2norm(qc[:, hh * d:(hh + 1) * d]) * (d ** -0.5) for hh in heads])
    k = jnp.stack([_l2norm(kc[:, hh * d:(hh + 1) * d]) for hh in heads])
    v = jnp.stack([vc[:, hh * d:(hh + 1) * d] for hh in heads])
    gcol = jnp.stack([gc_all[:, hh:hh + 1] for hh in heads])
    bcol = jnp.stack([beta_all[:, HEAD_GROUP + hh:HEAD_GROUP + hh + 1] for hh in heads])
    wu, a_intra, egc = _wy_solve(q, k, v, gcol, bcol, seq)
    q_dec = (q * egc).astype(BF16)
    g4 = gcol.reshape(HEAD_GROUP, bb, seq, 1)
    glast = g4[:, :, seq - 1:seq, :]
    k_dec = (k.reshape(HEAD_GROUP, bb, seq, d) * jnp.exp(glast - g4)).astype(BF16)
    g_last = jnp.exp(glast)

    gain = og_ref[...]
    for hh in heads:
        s = s0_ref[:, hh]
        w = wu[hh, :, :d].astype(BF16).reshape(bb, seq, d)
        u = wu[hh, :, d:].reshape(bb, seq, d)
        wq = jnp.concatenate([w, q_dec[hh].reshape(bb, seq, d)], axis=1)
        r = jnp.einsum("bcd,bde->bce", wq, s.astype(BF16), preferred_element_type=F32)
        v_new = u - r[:, :seq]
        vb = v_new.astype(BF16)
        o = r[:, seq:].reshape(rows, d) + jnp.dot(a_intra[hh], vb.reshape(rows, d), preferred_element_type=F32)
        sout_ref[:, hh] = s * g_last[hh] + jnp.einsum("bcd,bce->bde", k_dec[hh], vb,
                                                      preferred_element_type=F32)
        o_ref[:, hh * d:(hh + 1) * d] = _rms(o, gain) * _silu(z_ref[:, hh * d:(hh + 1) * d])


def _delta_sample(proj, batch, seq, conv_bufs, states, j, conv_w, alog_rows, dtb_rows, out_gain):
    bb = LANES // seq
    assert batch % bb == 0 and seq == SUBLANES
    rows = bb * seq
    gw = HEAD_GROUP * HEAD_DIM
    ng = N_HEAD_GROUPS

    def col(base):
        return pl.BlockSpec((rows, gw), lambda i, g: (i, base * ng + g))

    def bufcol(base):
        return pl.BlockSpec((None, bb, CONV_W - 1, gw), lambda i, g: (j, i, 0, base * ng + g))

    def wcol(base):
        return pl.BlockSpec((CONV_W, gw), lambda i, g: (0, base * ng + g))

    ab_base = (4 * DELTA_W + X_W) // LANES
    state_spec = pl.BlockSpec((None, bb, HEAD_GROUP, HEAD_DIM, HEAD_DIM), lambda i, g: (j, i, g, 0, 0))
    return pl.pallas_call(
        functools.partial(_delta_sample_kernel, bb=bb, seq=seq),
        grid=(batch // bb, ng),
        in_specs=[
            col(0), col(1), col(2), col(3),
            pl.BlockSpec((rows, LANES), lambda i, g: (i, ab_base + g)),
            bufcol(0), bufcol(1), bufcol(2),
            state_spec,
            wcol(0), wcol(1), wcol(2),
            pl.BlockSpec((1, LANES), lambda i, g: (0, g)),
            pl.BlockSpec((1, LANES), lambda i, g: (0, g)),
            pl.BlockSpec((1, HEAD_DIM), lambda i, g: (0, 0)),
        ],
        out_specs=[
            pl.BlockSpec((rows, gw), lambda i, g: (i, g)),
            pl.BlockSpec((bb, HEAD_GROUP, HEAD_DIM, HEAD_DIM), lambda i, g: (i, g, 0, 0)),
        ],
        out_shape=[
            jax.ShapeDtypeStruct((batch * seq, DELTA_W), F32),
            jax.ShapeDtypeStruct((batch, DELTA_HEADS, HEAD_DIM, HEAD_DIM), F32),
        ],
        compiler_params=_params("parallel", "parallel"),
    )(proj, proj, proj, proj, proj, conv_bufs, conv_bufs, conv_bufs, states, conv_w, conv_w, conv_w,
      alog_rows, dtb_rows, out_gain)


def _gelu(x):
    return 0.5 * x * (1.0 + lax.erf(x * (2.0 ** -0.5)))


def _layernorm(x, gain, bias):
    mu = jnp.mean(x, axis=-1, keepdims=True)
    xc = x - mu
    var = jnp.mean(xc * xc, axis=-1, keepdims=True)
    return xc * lax.rsqrt(var + EPS) * gain + bias


def _gmlp_prompt_kernel(u_ref, v_ref, lng_ref, lnb_ref, ws_ref, bs_ref, o_ref, *, chunks):
    c = GMLP_CHUNK
    d = HEAD_DIM
    ri = lax.broadcasted_iota(jnp.int32, (c, c), 0)
    ci = lax.broadcasted_iota(jnp.int32, (c, c), 1)
    lower = ci <= ri
    v = _layernorm(_gelu(v_ref[...]), lng_ref[...], lnb_ref[...]).astype(BF16)
    bs = bs_ref[...]
    for g in range(GMLP_GROUPS):
        w = jnp.where(lower, ws_ref[g], 0.0).astype(BF16)
        bias = bs[:, g:g + 1]
        for ch in range(chunks):
            rs = slice(ch * c, (ch + 1) * c)
            cs = slice(g * d, (g + 1) * d)
            mixed = jnp.dot(w, v[rs, cs], preferred_element_type=F32) + bias
            o_ref[rs, cs] = _gelu(u_ref[rs, cs]) * mixed


def _gmlp_prompt(proj, ln_gain, ln_bias, w_s, b_s_t):
    m = proj.shape[0]
    chunks = 2
    tm = chunks * GMLP_CHUNK
    assert m % tm == 0
    return pl.pallas_call(
        functools.partial(_gmlp_prompt_kernel, chunks=chunks),
        grid=(m // tm,),
        in_specs=[
            pl.BlockSpec((tm, GMLP_W), lambda i: (i, 0)),
            pl.BlockSpec((tm, GMLP_W), lambda i: (i, 1)),
            pl.BlockSpec((1, GMLP_W), lambda i: (0, 0)),
            pl.BlockSpec((1, GMLP_W), lambda i: (0, 0)),
            pl.BlockSpec((GMLP_GROUPS, GMLP_CHUNK, GMLP_CHUNK), lambda i: (0, 0, 0)),
            pl.BlockSpec((GMLP_CHUNK, GMLP_GROUPS), lambda i: (0, 0)),
        ],
        out_specs=pl.BlockSpec((tm, GMLP_W), lambda i: (i, 0)),
        out_shape=jax.ShapeDtypeStruct((m, GMLP_W), F32),
        compiler_params=_params("parallel"),
    )(proj, proj, ln_gain, ln_bias, w_s, b_s_t)


def _gmlp_sample_kernel(u_ref, v_ref, lng_ref, lnb_ref, wx_ref, bx_ref, o_ref, vout_ref, *, bb, seq):
    v = _layernorm(_gelu(v_ref[...]), lng_ref[...], lnb_ref[...])
    vout_ref[...] = v
    v3 = v.reshape(bb, seq, GMLP_W)
    row = lax.broadcasted_iota(jnp.int32, (seq, 1), 0)
    mixed = jnp.broadcast_to(bx_ref[...][None], (bb, seq, GMLP_W))
    for j in range(seq):
        wj = jnp.where(row >= j, wx_ref[j], 0.0)
        mixed = mixed + wj[None] * v3[:, j:j + 1, :]
    o_ref[...] = _gelu(u_ref[...]) * mixed.reshape(bb * seq, GMLP_W)


def _gmlp_sample(proj, batch, seq, ln_gain, ln_bias, w_exp, b_exp):
    bb = 32
    assert batch % bb == 0 and seq == SUBLANES
    rows = bb * seq
    return pl.pallas_call(
        functools.partial(_gmlp_sample_kernel, bb=bb, seq=seq),
        grid=(batch // bb,),
        in_specs=[
            pl.BlockSpec((rows, GMLP_W), lambda i: (i, 0)),
            pl.BlockSpec((rows, GMLP_W), lambda i: (i, 1)),
            pl.BlockSpec((1, GMLP_W), lambda i: (0, 0)),
            pl.BlockSpec((1, GMLP_W), lambda i: (0, 0)),
            pl.BlockSpec((seq, seq, GMLP_W), lambda i: (0, 0, 0)),
            pl.BlockSpec((seq, GMLP_W), lambda i: (0, 0)),
        ],
        out_specs=[
            pl.BlockSpec((rows, GMLP_W), lambda i: (i, 0)),
            pl.BlockSpec((rows, GMLP_W), lambda i: (i, 0)),
        ],
        out_shape=[
            jax.ShapeDtypeStruct((batch * seq, GMLP_W), F32),
            jax.ShapeDtypeStruct((batch * seq, GMLP_W), F32),
        ],
        compiler_params=_params("parallel"),
    )(proj, proj, ln_gain, ln_bias, w_exp, b_exp)


def _softmax(s):
    m = jnp.max(s, axis=-1, keepdims=True)
    e = jnp.exp(s - m)
    return e / jnp.sum(e, axis=-1, keepdims=True)


def _mem_attn_prompt_kernel(q_ref, k_ref, v_ref, o_ref):
    d = HEAD_DIM
    for h in range(X_HEADS):
        sl = slice(h * d, (h + 1) * d)
        s = lax.dot_general(q_ref[:, sl].astype(BF16), k_ref[:, sl].astype(BF16),
                            (((1,), (1,)), ((), ())), preferred_element_type=F32) * (d ** -0.5)
        o_ref[:, sl] = _bdot(_softmax(s), v_ref[:, sl])


def _mem_attn_prompt(proj, q_block, batch, seq, mem_k, mem_v):
    tt = min(512, seq)
    nt = seq // tt
    n_mem = mem_k.shape[1]
    return pl.pallas_call(
        _mem_attn_prompt_kernel,
        grid=(batch, nt),
        in_specs=[
            pl.BlockSpec((tt, X_W), lambda b, t: (b * nt + t, q_block)),
            pl.BlockSpec((None, n_mem, X_W), lambda b, t: (b, 0, 0)),
            pl.BlockSpec((None, n_mem, X_W), lambda b, t: (b, 0, 0)),
        ],
        out_specs=pl.BlockSpec((tt, X_W), lambda b, t: (b * nt + t, 0)),
        out_shape=jax.ShapeDtypeStruct((batch * seq, X_W), F32),
        compiler_params=_params("parallel", "parallel"),
    )(proj, mem_k, mem_v)


def _mem_attn_sample_kernel(q_ref, k_ref, v_ref, o_ref, *, bb, seq):
    d = HEAD_DIM
    nq = X_HEADS * seq
    nk = k_ref.shape[1]
    q = jnp.concatenate([q_ref[:, h * d:(h + 1) * d].reshape(bb, seq, d) for h in range(X_HEADS)], axis=1)
    s = jnp.einsum("bqd,bkd->bqk", q.astype(BF16), k_ref[...].astype(BF16),
                   preferred_element_type=F32) * (d ** -0.5)
    q_head = lax.broadcasted_iota(jnp.int32, (nq, nk), 0) // seq
    k_head = lax.broadcasted_iota(jnp.int32, (nq, nk), 1) % X_HEADS
    p = _softmax(jnp.where(q_head == k_head, s, -1e30)).astype(BF16)
    o = jnp.einsum("bqk,bkd->bqd", p, v_ref[...].astype(BF16), preferred_element_type=F32)
    for h in range(X_HEADS):
        o_ref[:, h * d:(h + 1) * d] = o[:, h * seq:(h + 1) * seq, :].reshape(bb * seq, d)


def _mem_attn_sample(proj, q_block, batch, seq, mem_k, mem_v, layer):
    bb = 8
    rows = bb * seq
    n_rows = mem_k.shape[2]
    return pl.pallas_call(
        functools.partial(_mem_attn_sample_kernel, bb=bb, seq=seq),
        grid=(batch // bb,),
        in_specs=[
            pl.BlockSpec((rows, X_W), lambda i: (i, q_block)),
            pl.BlockSpec((None, bb, n_rows, HEAD_DIM), lambda i: (layer, i, 0, 0)),
            pl.BlockSpec((None, bb, n_rows, HEAD_DIM), lambda i: (layer, i, 0, 0)),
        ],
        out_specs=pl.BlockSpec((rows, X_W), lambda i: (i, 0)),
        out_shape=jax.ShapeDtypeStruct((batch * seq, X_W), F32),
        compiler_params=_params("parallel"),
    )(proj, mem_k, mem_v)


def _delta_in_weight(w_in_a):
    d_model = w_in_a.shape[0]
    qkvz = w_in_a[:, :4 * DELTA_W]
    a = w_in_a[:, 4 * DELTA_W:4 * DELTA_W + DELTA_HEADS]
    b = w_in_a[:, 4 * DELTA_W + DELTA_HEADS:4 * DELTA_W + 2 * DELTA_HEADS]
    xq = w_in_a[:, 4 * DELTA_W + 2 * DELTA_HEADS:]
    parts = [qkvz, xq]
    pad = jnp.zeros((d_model, LANES - 2 * HEAD_GROUP), w_in_a.dtype)
    for g in range(N_HEAD_GROUPS):
        hs = slice(g * HEAD_GROUP, (g + 1) * HEAD_GROUP)
        parts += [a[:, hs], b[:, hs], pad]
    parts.append(jnp.zeros((d_model, AB_W - N_HEAD_GROUPS * LANES), w_in_a.dtype))
    return jnp.concatenate(parts, axis=1).astype(BF16)


def _gate_rows(vec):
    row = jnp.zeros((N_HEAD_GROUPS, LANES), F32)
    row = row.at[:, :HEAD_GROUP].set(vec.astype(F32).reshape(N_HEAD_GROUPS, HEAD_GROUP))
    return jnp.concatenate([row.reshape(1, -1), jnp.zeros((1, AB_W - N_HEAD_GROUPS * LANES), F32)], axis=1)


def _ffn_step(x, p, layer, slot, pre, post, round_next):
    w_gu, w_dn = p["ffn_bf16"][(layer, slot)]
    nxt = (layer, 1) if slot == 0 else (layer + 1, 0)
    if round_next and nxt[0] < p["w_out"].shape[0]:
        x, gu_next, dn_next = _ffn(x, p["gains"], layer * 6, pre, post, w_gu, w_dn,
                                   cast_next=(p["w_ffn_gu"], p["w_ffn_dn"]) + nxt)
        p["ffn_bf16"][nxt] = (gu_next, dn_next)
        return x
    return _ffn(x, p["gains"], layer * 6, pre, post, w_gu, w_dn)


def _trunk(x, batch, seq, is_prompt, mem_k, mem_v, conv_bufs, delta_states, p):
    new_conv, new_delta, new_v = [], [], []
    gains = p["gains"]
    depth = p["w_out"].shape[0]
    for i in range(depth):
        j = i // N_MIXERS
        x = _ffn_step(x, p, i, 0, 0, 1, is_prompt)
        if i % N_MIXERS == 0:
            proj = _norm_matmul(x, gains, i * 6 + 2, p["w_in_a"][j], 1024)
            q_block = 4 * DELTA_W // X_W
            if is_prompt:
                mix, s_new = _delta_prompt(proj, batch, seq, p["conv_w"][j], p["alog_rows"][j],
                                           p["dtb_rows"][j], p["delta_norm_gain"][j])
            else:
                mix, s_new = _delta_sample(proj, batch, seq, conv_bufs, delta_states, j, p["conv_w"][j],
                                           p["alog_rows"][j], p["dtb_rows"][j], p["delta_norm_gain"][j])
            new_conv.append(proj.reshape(batch, seq, -1)[:, seq - (CONV_W - 1):, :3 * DELTA_W])
            new_delta.append(s_new)
        else:
            proj = _norm_matmul(x, gains, i * 6 + 2, p["w_in_b"][j], 896)
            q_block = 2 * GMLP_W // X_W
            if is_prompt:
                mix = _gmlp_prompt(proj, p["gmlp_ln_gain"][j], p["gmlp_ln_bias"][j], p["w_spatial"][j],
                                   p["b_spatial_t"][j])
            else:
                mix, v_rows = _gmlp_sample(proj, batch, seq, p["gmlp_ln_gain"][j], p["gmlp_ln_bias"][j],
                                           p["w_spatial_exp"][j], p["b_spatial_exp"][j])
                new_v.append(v_rows.reshape(batch, seq, GMLP_W))
        if is_prompt:
            mem_out = _mem_attn_prompt(proj, q_block, batch, seq, mem_k[i], mem_v[i])
        else:
            mem_out = _mem_attn_sample(proj, q_block, batch, seq, mem_k, mem_v, i)
        x = _out_proj(x, mix, mem_out, p["w_out"], gains, i)
        x = _ffn_step(x, p, i, 1, 4, 5, is_prompt)
    return x, new_conv, new_delta, new_v


def kernel(x_prompt, x_sample, mem_prompt, cache_mem_k, cache_mem_v, state_delta, state_conv, norm_gains, w_ffn_gu, w_ffn_dn, w_in_a, conv_w, a_log, dt_bias, delta_norm_gain, w_in_b, gmlp_ln_gain, gmlp_ln_bias, w_spatial, b_spatial, mem_norm_gain, w_mem_kv, w_out):
    batch, seq, d_model = x_prompt.shape
    dec_batch, dec_seq, _ = x_sample.shape
    depth = w_out.shape[0]
    n_mem = mem_prompt.shape[1]
    n_a = w_in_a.shape[0]
    n_b = w_in_b.shape[0]

    w_s_dec = w_spatial[:, :, :dec_seq, :dec_seq]
    p = {
        "gains": norm_gains.reshape(depth * 6, 1, d_model),
        "w_ffn_gu": w_ffn_gu,
        "w_ffn_dn": w_ffn_dn,
        "ffn_bf16": {(0, 0): (w_ffn_gu[0, 0].astype(BF16), w_ffn_dn[0, 0].astype(BF16))},
        "w_in_a": [_delta_in_weight(w_in_a[j]) for j in range(n_a)],
        "w_in_b": [w_in_b[j].astype(BF16) for j in range(n_b)],
        "w_out": w_out.astype(BF16),
        "conv_w": conv_w,
        "alog_rows": [_gate_rows(a_log[j]) for j in range(n_a)],
        "dtb_rows": [_gate_rows(dt_bias[j]) for j in range(n_a)],
        "delta_norm_gain": delta_norm_gain.reshape(n_a, 1, HEAD_DIM),
        "gmlp_ln_gain": gmlp_ln_gain.reshape(n_b, 1, GMLP_W),
        "gmlp_ln_bias": gmlp_ln_bias.reshape(n_b, 1, GMLP_W),
        "w_spatial": w_spatial,
        "b_spatial_t": jnp.swapaxes(b_spatial, 1, 2),
        "w_spatial_exp": jnp.repeat(jnp.transpose(w_s_dec, (0, 3, 2, 1)), HEAD_DIM, axis=-1),
        "b_spatial_exp": jnp.repeat(jnp.swapaxes(b_spatial[:, :, :dec_seq], 1, 2), HEAD_DIM, axis=-1),
    }

    mem2d = mem_prompt.reshape(batch * n_mem, d_model)
    mem_gains = mem_norm_gain.reshape(depth, 1, d_model)
    w_kv = w_mem_kv.astype(BF16)
    kvs = [_norm_matmul(mem2d, mem_gains, i, w_kv[i], 1024) for i in range(depth)]
    mem_k_prompt = jnp.stack([kv[:, :X_W].reshape(batch, n_mem, X_W) for kv in kvs])
    mem_v_prompt = jnp.stack([kv[:, X_W:].reshape(batch, n_mem, X_W) for kv in kvs])

    y_prompt, conv_p, delta_p, _ = _trunk(x_prompt.reshape(batch * seq, d_model), batch, seq, True,
                                          mem_k_prompt, mem_v_prompt, None, None, p)

    y_sample, conv_s, delta_s, v_s = _trunk(
        x_sample.reshape(dec_batch * dec_seq, d_model), dec_batch, dec_seq, False,
        cache_mem_k.reshape(depth, dec_batch, n_mem * X_HEADS, HEAD_DIM),
        cache_mem_v.reshape(depth, dec_batch, n_mem * X_HEADS, HEAD_DIM),
        state_conv, state_delta, p)

    return (y_prompt.reshape(batch, seq, d_model),
            y_sample.reshape(dec_batch, dec_seq, d_model),
            mem_k_prompt.reshape(depth, batch, n_mem, X_HEADS, HEAD_DIM),
            mem_v_prompt.reshape(depth, batch, n_mem, X_HEADS, HEAD_DIM),
            jnp.stack(delta_p), jnp.stack(conv_p), jnp.stack(delta_s), jnp.stack(conv_s), jnp.stack(v_s))
```

```python
import functools

import jax
import jax.numpy as jnp
from jax import lax
from jax.experimental import pallas as pl
from jax.experimental.pallas import tpu as pltpu

F32 = jnp.float32
BF16 = jnp.bfloat16

EPS = 1e-6
HEAD_DIM = 128
X_HEADS = 4
X_W = X_HEADS * HEAD_DIM
DELTA_HEADS = 12
DELTA_W = DELTA_HEADS * HEAD_DIM
CONV_W = 4
DELTA_CHUNK = 64
GMLP_GROUPS = 12
GMLP_W = GMLP_GROUPS * HEAD_DIM
GMLP_CHUNK = 128
N_MIXERS = 2

LANES = 128
SUBLANES = 8
VMEM_LIMIT = 56 * 1024 * 1024
FFN_CAST_VMEM_LIMIT = 60 * 1024 * 1024

HEAD_GROUP = 4
N_HEAD_GROUPS = DELTA_HEADS // HEAD_GROUP
FFN_ROWS = 1024
FFN_COLS = 512
PROJ_ROWS = 1024
AB_W = 512
IN_A_W = 4 * DELTA_W + X_W + AB_W


def _params(*sem, vmem=VMEM_LIMIT):
    return pltpu.CompilerParams(dimension_semantics=sem, vmem_limit_bytes=vmem)


def _rms(x, gain):
    ms = jnp.mean(x * x, axis=-1, keepdims=True)
    return x * lax.rsqrt(ms + EPS) * gain


def _sigmoid(x):
    return 0.5 * jnp.tanh(0.5 * x) + 0.5


def _silu(x):
    return x * _sigmoid(x)


def _softplus(x):
    return jnp.maximum(x, 0.0) + jnp.log1p(jnp.exp(-jnp.abs(x)))


def _bdot(a, b):
    return jnp.dot(a.astype(BF16), b.astype(BF16), preferred_element_type=F32)


def _ffn_cast_kernel(x_ref, gpre_ref, gpost_ref, wg_ref, wu_ref, wd_ref, cgu_ref, cdn_ref,
                     o_ref, ogu_ref, odn_ref, hn_ref, act_ref):
    ogu_ref[...] = cgu_ref[...].astype(BF16)
    odn_ref[...] = cdn_ref[...].astype(BF16)
    _ffn_kernel(x_ref, gpre_ref, gpost_ref, wg_ref, wu_ref, wd_ref, o_ref, hn_ref, act_ref)


def _ffn_kernel(x_ref, gpre_ref, gpost_ref, wg_ref, wu_ref, wd_ref, o_ref, hn_ref, act_ref):
    f = pl.program_id(1)
    last = pl.num_programs(1) - 1

    def gate_up(slot):
        h = hn_ref[...]
        gate = jnp.dot(h, wg_ref[...], preferred_element_type=F32)
        up = jnp.dot(h, wu_ref[...], preferred_element_type=F32)
        act_ref[slot] = (_silu(gate) * up).astype(BF16)

    def down(slot):
        o_ref[...] += jnp.dot(act_ref[slot], wd_ref[...], preferred_element_type=F32)

    @pl.when(f == 0)
    def _():
        hn_ref[...] = _rms(x_ref[...], gpre_ref[...]).astype(BF16)
        o_ref[...] = jnp.zeros_like(o_ref)
        gate_up(0)

    @pl.when((f > 0) & (f < last))
    def _():
        down((f - 1) % 2)
        gate_up(f % 2)

    @pl.when(f == last)
    def _():
        down((f - 1) % 2)
        o_ref[...] = x_ref[...] + 0.5 * _rms(o_ref[...], gpost_ref[...])


def _ffn(x, gains, gain_base, pre, post, w_gu, w_dn, cast_next=None):
    m, d = x.shape
    ffn = w_dn.shape[0]
    tm = min(FFN_ROWS, m)
    tf = FFN_COLS
    nf = ffn // tf
    nm = m // tm
    assert m % tm == 0 and ffn % tf == 0
    def up_blk(f):
        return jnp.minimum(f, nf - 1)

    def dn_blk(f):
        return jnp.maximum(f - 1, 0)

    in_specs = [
        pl.BlockSpec((tm, d), lambda i, f: (i, 0), pipeline_mode=pl.Buffered(1)),
        pl.BlockSpec((None, 1, d), lambda i, f: (gain_base + pre, 0, 0)),
        pl.BlockSpec((None, 1, d), lambda i, f: (gain_base + post, 0, 0)),
        pl.BlockSpec((d, tf), lambda i, f: (0, up_blk(f))),
        pl.BlockSpec((d, tf), lambda i, f: (0, nf + up_blk(f))),
        pl.BlockSpec((tf, d), lambda i, f: (dn_blk(f), 0)),
    ]
    out_specs = pl.BlockSpec((tm, d), lambda i, f: (i, 0))
    out_shape = jax.ShapeDtypeStruct((m, d), F32)
    scratch = [pltpu.VMEM((tm, d), BF16), pltpu.VMEM((2, tm, tf), BF16)]
    if cast_next is None:
        return pl.pallas_call(
            _ffn_kernel, grid=(nm, nf + 1), in_specs=in_specs, out_specs=out_specs, out_shape=out_shape,
            scratch_shapes=scratch, compiler_params=_params("parallel", "arbitrary"),
        )(x, gains, gains, w_gu, w_gu, w_dn)
    gu_all, dn_all, layer, slot = cast_next
    gu_rows, gu_cols = d // nm, 2 * ffn // nf
    dn_rows = ffn // (nm * nf)
    assert gu_rows * nm == d and gu_cols * nf == 2 * ffn and dn_rows * nm * nf == ffn
    assert gu_rows % 16 == 0 and gu_cols % LANES == 0 and dn_rows % 16 == 0
    return pl.pallas_call(
        _ffn_cast_kernel,
        grid=(nm, nf + 1),
        in_specs=in_specs + [
            pl.BlockSpec((None, None, gu_rows, gu_cols), lambda i, f: (layer, slot, i, up_blk(f))),
            pl.BlockSpec((None, None, dn_rows, d), lambda i, f: (layer, slot, i * nf + up_blk(f), 0)),
        ],
        out_specs=[
            out_specs,
            pl.BlockSpec((gu_rows, gu_cols), lambda i, f: (i, up_blk(f))),
            pl.BlockSpec((dn_rows, d), lambda i, f: (i * nf + up_blk(f), 0)),
        ],
        out_shape=[out_shape, jax.ShapeDtypeStruct((d, 2 * ffn), BF16), jax.ShapeDtypeStruct((ffn, d), BF16)],
        scratch_shapes=scratch,
        compiler_params=_params("parallel", "arbitrary", vmem=FFN_CAST_VMEM_LIMIT),
    )(x, gains, gains, w_gu, w_gu, w_dn, gu_all, dn_all)


def _norm_matmul_kernel(x_ref, g_ref, w_ref, o_ref, hn_ref):
    @pl.when(pl.program_id(1) == 0)
    def _():
        hn_ref[...] = _rms(x_ref[...], g_ref[...]).astype(BF16)

    o_ref[...] = jnp.dot(hn_ref[...], w_ref[...], preferred_element_type=F32)


def _norm_matmul(x, gains, gain_idx, w, tn):
    m, d = x.shape
    n = w.shape[1]
    tm = min(PROJ_ROWS, m)
    assert m % tm == 0 and n % tn == 0
    return pl.pallas_call(
        _norm_matmul_kernel,
        grid=(m // tm, n // tn),
        in_specs=[
            pl.BlockSpec((tm, d), lambda i, j: (i, 0)),
            pl.BlockSpec((None, 1, d), lambda i, j: (gain_idx, 0, 0)),
            pl.BlockSpec((d, tn), lambda i, j: (0, j)),
        ],
        out_specs=pl.BlockSpec((tm, tn), lambda i, j: (i, j)),
        out_shape=jax.ShapeDtypeStruct((m, n), F32),
        scratch_shapes=[pltpu.VMEM((tm, d), BF16)],
        compiler_params=_params("parallel", "arbitrary"),
    )(x, gains, w)


def _out_proj_kernel(x_ref, mix_ref, mem_ref, wmix_ref, wmem_ref, g_ref, o_ref):
    y = _bdot(mix_ref[...], wmix_ref[...]) + _bdot(mem_ref[...], wmem_ref[...])
    o_ref[...] = x_ref[...] + _rms(y, g_ref[...])


def _out_proj(x, mix, mem, w_out, gains, layer):
    m, d = x.shape
    wm = mix.shape[1]
    wx = mem.shape[1]
    tm = min(512, m)
    assert wm % wx == 0
    return pl.pallas_call(
        _out_proj_kernel,
        grid=(m // tm,),
        in_specs=[
            pl.BlockSpec((tm, d), lambda i: (i, 0)),
            pl.BlockSpec((tm, wm), lambda i: (i, 0)),
            pl.BlockSpec((tm, wx), lambda i: (i, 0)),
            pl.BlockSpec((None, wm, d), lambda i: (layer, 0, 0)),
            pl.BlockSpec((None, wx, d), lambda i: (layer, wm // wx, 0)),
            pl.BlockSpec((None, 1, d), lambda i: (layer * 6 + 3, 0, 0)),
        ],
        out_specs=pl.BlockSpec((tm, d), lambda i: (i, 0)),
        out_shape=jax.ShapeDtypeStruct((m, d), F32),
        compiler_params=_params("parallel"),
    )(x, mix, mem, w_out, w_out, gains)


def _unit_lower_inverse(lmat, c, span=None):
    span = c if span is None else span
    base = 4
    ri = lax.broadcasted_iota(jnp.int32, (c, c), 0)
    ci = lax.broadcasted_iota(jnp.int32, (c, c), 1)
    eye = (ri == ci).astype(F32)
    l_hi, l_lo = _split(lmat)

    def masked(keep):
        return jnp.where(keep, l_hi, jnp.zeros_like(l_hi)), jnp.where(keep, l_lo, jnp.zeros_like(l_lo))

    same = (ri // base) == (ci // base)
    dm = masked(same)
    x = eye - jnp.where(same, lmat, 0.0)
    x = x + _mm3(_split(x), _split(_mm3(dm, dm)))
    size = base
    while size < span:
        inner = ((ri // (2 * size)) == (ci // (2 * size))) & ((ri // size) != (ci // size))
        xs = _split(x)
        x = x - _mm3(_split(_mm3(xs, masked(inner))), xs)
        size *= 2
    return x


def _split(x):
    hi = x.astype(BF16)
    return hi, (x - hi.astype(F32)).astype(BF16)


def _mm3(a, b):
    def mm(p, q):
        return jnp.einsum("nij,njk->nik", p, q, preferred_element_type=F32)

    width = b[0].shape[-1]
    if a[0].shape[-1] % LANES == 0 and width % LANES == 0:
        lhs = jnp.concatenate([a[0], a[1]], axis=-1)
        rhs = jnp.concatenate([jnp.concatenate([b[0], b[1]], axis=-1),
                               jnp.concatenate([b[0], jnp.zeros_like(b[0])], axis=-1)], axis=-2)
        r = mm(lhs, rhs)
        return r[..., :width] + r[..., width:]
    return mm(a[0], b[0]) + (mm(a[0], b[1]) + mm(a[1], b[0]))


def _delta_pair_terms(q, k, v, gcol, bcol):
    n, cp, d = q.shape
    c = DELTA_CHUNK
    wu, a_intra, egc = _wy_solve(q, k, v, gcol, bcol, c)
    wu = wu.astype(BF16)
    awu = jnp.einsum("nij,nje->nie", a_intra, wu, preferred_element_type=F32)
    qt = (q * egc - awu[..., :d]).astype(BF16)
    o0 = awu[..., d:]
    k64 = k.reshape(2 * n, c, d)
    g64 = gcol.reshape(2 * n, c, 1)
    glast = g64[:, c - 1:c, :]
    k_dec = (k64 * jnp.exp(glast - g64)).astype(BF16)
    mb = jnp.einsum("ncd,nce->nde", k_dec, wu.reshape(2 * n, c, 2 * d), preferred_element_type=F32)
    return (qt.reshape(2 * n, c, d), mb[..., :d].astype(BF16), mb[..., d:], o0.reshape(2 * n, c, d),
            jnp.exp(glast))


def _wy_solve(q, k, v, gcol, bcol, span):
    rows = q.shape[1]
    ri = lax.broadcasted_iota(jnp.int32, (rows, rows), 0)
    ci = lax.broadcasted_iota(jnp.int32, (rows, rows), 1)
    eye = (ri == ci).astype(F32)
    same = (ri // span) == (ci // span)
    strict = same & (ci < ri)
    causal = same & (ci <= ri)
    grow = jnp.sum(gcol * eye, axis=1, keepdims=True)
    diff = gcol - grow
    kb = k.astype(BF16)
    kk = jnp.einsum("ncd,nmd->ncm", kb, kb, preferred_element_type=F32)
    lmat = jnp.where(strict, bcol * kk * jnp.exp(jnp.where(strict, diff, 0.0)), 0.0)
    tinv = _unit_lower_inverse(lmat, rows, span)
    egc = jnp.exp(gcol)
    rhs = jnp.concatenate([k * (bcol * egc), v * bcol], axis=-1)
    wu = _mm3(_split(tinv), _split(rhs))
    qk = jnp.einsum("ncd,nmd->ncm", q.astype(BF16), kb, preferred_element_type=F32)
    a_intra = jnp.where(causal, qk * jnp.exp(jnp.where(causal, diff, 0.0)), 0.0).astype(BF16)
    return wu, a_intra, egc


def _l2norm(x):
    return x * lax.rsqrt(jnp.sum(x * x, axis=-1, keepdims=True) + EPS)


def _block_cumsum(g, c):
    rows, lanes = g.shape
    n = rows // c
    ri = lax.broadcasted_iota(jnp.int32, (n, c, c), 1)
    ci = lax.broadcasted_iota(jnp.int32, (n, c, c), 2)
    tri = (ci <= ri).astype(BF16)
    g3 = g.reshape(n, c, lanes)
    hi = g3.astype(BF16)
    r1 = g3 - hi.astype(F32)
    mid = r1.astype(BF16)
    lo = (r1 - mid.astype(F32)).astype(BF16)

    def mm(p):
        return jnp.einsum("nij,njk->nik", tri, p, preferred_element_type=F32)

    return (mm(hi) + (mm(mid) + mm(lo))).reshape(rows, lanes)


def _shifted(x, prev, shift, row):
    xs = pltpu.roll(x, shift, x.ndim - 2)
    if x.ndim == 2:
        ps = pltpu.roll(prev, shift, 0)
        reps = x.shape[0] // SUBLANES
        ps = jnp.broadcast_to(ps[None], (reps,) + ps.shape).reshape(x.shape)
        return jnp.where(row < shift, ps, xs)
    for t in range(shift):
        src = prev.shape[1] - shift + t
        xs = jnp.where(row == t, prev[:, src:src + 1, :], xs)
    return xs


def _short_conv_silu(x, prev, w, row):
    y = x * w[CONV_W - 1:CONV_W]
    for shift in range(1, CONV_W):
        tap = CONV_W - 1 - shift
        y = y + _shifted(x, prev, shift, row) * w[tap:tap + 1]
    return _silu(y)


def _delta_prompt_kernel(q_ref, k_ref, v_ref, z_ref, ab_ref, wq_ref, wk_ref, wv_ref, alog_ref, dtb_ref,
                         og_ref, o_ref, sout_ref, s_ref, pq_ref, pk_ref, pv_ref, bq_ref, bk_ref, bv_ref, *, tt):
    t = pl.program_id(2)
    c = DELTA_CHUNK
    n = tt // c
    d = HEAD_DIM

    @pl.when(t == 0)
    def _():
        s_ref[...] = jnp.zeros_like(s_ref)
        for ref in (pq_ref, pk_ref, pv_ref, bq_ref, bk_ref, bv_ref):
            ref[:SUBLANES, :] = jnp.zeros((SUBLANES, ref.shape[1]), F32)

    def conv(x_ref, p_ref, b_ref, w_ref):
        x = x_ref[...]
        w = w_ref[...]
        p_ref[SUBLANES:, :] = x
        x1 = p_ref[SUBLANES - 1:SUBLANES - 1 + tt, :]
        b_ref[SUBLANES:, :] = x * w[1:2] + x1 * w[0:1]
        y = x * w[3:4] + x1 * w[2:3] + b_ref[SUBLANES - 2:SUBLANES - 2 + tt, :]
        p_ref[:SUBLANES, :] = x_ref[tt - SUBLANES:, :]
        b_ref[:SUBLANES, :] = b_ref[tt:tt + SUBLANES, :]
        return _silu(y)

    assert CONV_W == 4
    qc = conv(q_ref, pq_ref, bq_ref, wq_ref)
    kc = conv(k_ref, pk_ref, bk_ref, wk_ref)
    vc = conv(v_ref, pv_ref, bv_ref, wv_ref)

    ab = ab_ref[...]
    g_all = -jnp.exp(alog_ref[...]) * _softplus(ab + dtb_ref[...])
    beta_all = _sigmoid(ab)
    gc_all = _block_cumsum(g_all, c)

    heads = range(HEAD_GROUP)

    def pairs(cols):
        return jnp.concatenate([x.reshape(n // 2, 2 * c, x.shape[-1]) for x in cols], axis=0)

    q = pairs([_l2norm(qc[:, hh * d:(hh + 1) * d]) * (d ** -0.5) for hh in heads])
    k = pairs([_l2norm(kc[:, hh * d:(hh + 1) * d]) for hh in heads])
    v = pairs([vc[:, hh * d:(hh + 1) * d] for hh in heads])
    gcol = pairs([gc_all[:, hh:hh + 1] for hh in heads])
    bcol = pairs([beta_all[:, HEAD_GROUP + hh:HEAD_GROUP + hh + 1] for hh in heads])
    qt, m, bmat, o0, g_last = _delta_pair_terms(q, k, v, gcol, bcol)

    gain = og_ref[...]
    states = [s_ref[hh] for hh in heads]
    for ch in range(n):
        for hh in heads:
            i = hh * n + ch
            s = states[hh]
            r = jnp.dot(jnp.concatenate([qt[i], m[i]], axis=0), s.astype(BF16),
                        preferred_element_type=F32)
            o = r[:c] + o0[i]
            states[hh] = s * g_last[i] + (bmat[i] - r[c:])
            zz = z_ref[ch * c:(ch + 1) * c, hh * d:(hh + 1) * d]
            o_ref[ch * c:(ch + 1) * c, hh * d:(hh + 1) * d] = _rms(o, gain) * _silu(zz)
    for hh in range(HEAD_GROUP):
        s_ref[hh] = states[hh]
        sout_ref[hh] = states[hh]


def _delta_prompt(proj, batch, seq, conv_w, alog_rows, dtb_rows, out_gain):
    tt = min(512, seq)
    nt = seq // tt
    gw = HEAD_GROUP * HEAD_DIM
    ng = N_HEAD_GROUPS
    assert seq % tt == 0 and tt % DELTA_CHUNK == 0

    def col(base):
        return pl.BlockSpec((tt, gw), lambda b, g, t: (b * nt + t, base * ng + g))

    def wcol(base):
        return pl.BlockSpec((CONV_W, gw), lambda b, g, t: (0, base * ng + g))

    ab_base = (4 * DELTA_W + X_W) // LANES
    return pl.pallas_call(
        functools.partial(_delta_prompt_kernel, tt=tt),
        grid=(batch, ng, nt),
        in_specs=[
            col(0), col(1), col(2), col(3),
            pl.BlockSpec((tt, LANES), lambda b, g, t: (b * nt + t, ab_base + g)),
            wcol(0), wcol(1), wcol(2),
            pl.BlockSpec((1, LANES), lambda b, g, t: (0, g)),
            pl.BlockSpec((1, LANES), lambda b, g, t: (0, g)),
            pl.BlockSpec((1, HEAD_DIM), lambda b, g, t: (0, 0)),
        ],
        out_specs=[
            pl.BlockSpec((tt, gw), lambda b, g, t: (b * nt + t, g)),
            pl.BlockSpec((None, HEAD_GROUP, HEAD_DIM, HEAD_DIM), lambda b, g, t: (b, g, 0, 0)),
        ],
        out_shape=[
            jax.ShapeDtypeStruct((batch * seq, DELTA_W), F32),
            jax.ShapeDtypeStruct((batch, DELTA_HEADS, HEAD_DIM, HEAD_DIM), F32),
        ],
        scratch_shapes=[
            pltpu.VMEM((HEAD_GROUP, HEAD_DIM, HEAD_DIM), F32),
        ] + [pltpu.VMEM((SUBLANES + tt, gw), F32)] * 6,
        compiler_params=_params("parallel", "parallel", "arbitrary"),
    )(proj, proj, proj, proj, proj, conv_w, conv_w, conv_w, alog_rows, dtb_rows, out_gain)


def _delta_sample_kernel(q_ref, k_ref, v_ref, z_ref, ab_ref, bq_ref, bk_ref, bv_ref, s0_ref, wq_ref, wk_ref,
                         wv_ref, alog_ref, dtb_ref, og_ref, o_ref, sout_ref, *, bb, seq):
    d = HEAD_DIM
    rows = bb * seq
    row = lax.broadcasted_iota(jnp.int32, (1, seq, 1), 1)

    def conv(x_ref, buf_ref, w_ref):
        x = x_ref[...].reshape(bb, seq, x_ref.shape[1])
        return _short_conv_silu(x, buf_ref[...], w_ref[...], row).reshape(rows, x_ref.shape[1])

    qc = conv(q_ref, bq_ref, wq_ref)
    kc = conv(k_ref, bk_ref, wk_ref)
    vc = conv(v_ref, bv_ref, wv_ref)

    ab = ab_ref[...]
    g_all = -jnp.exp(alog_ref[...]) * _softplus(ab + dtb_ref[...])
    beta_all = _sigmoid(ab)
    gc_all = _block_cumsum(g_all, seq)

    heads = range(HEAD_GROUP)
    q = jnp.stack([_l2norm(qc[:, hh * d:(hh + 1) * d]) * (d ** -0.5) for hh in heads])
    k = jnp.stack([_l2norm(kc[:, hh * d:(hh + 1) * d]) for hh in heads])
    v = jnp.stack([vc[:, hh * d:(hh + 1) * d] for hh in heads])
    gcol = jnp.stack([gc_all[:, hh:hh + 1] for hh in heads])
    bcol = jnp.stack([beta_all[:, HEAD_GROUP + hh:HEAD_GROUP + hh + 1] for hh in heads])
    wu, a_intra, egc = _wy_solve(q, k, v, gcol, bcol, seq)
    q_dec = (q * egc).astype(BF16)
    g4 = gcol.reshape(HEAD_GROUP, bb, seq, 1)
    glast = g4[:, :, seq - 1:seq, :]
    k_dec = (k.reshape(HEAD_GROUP, bb, seq, d) * jnp.exp(glast - g4)).astype(BF16)
    g_last = jnp.exp(glast)

    gain = og_ref[...]
    for hh in heads:
        s = s0_ref[:, hh]
        w = wu[hh, :, :d].astype(BF16).reshape(bb, seq, d)
        u = wu[hh, :, d:].reshape(bb, seq, d)
        wq = jnp.concatenate([w, q_dec[hh].reshape(bb, seq, d)], axis=1)
        r = jnp.einsum("bcd,bde->bce", wq, s.astype(BF16), preferred_element_type=F32)
        v_new = u - r[:, :seq]
        vb = v_new.astype(BF16)
        o = r[:, seq:].reshape(rows, d) + jnp.dot(a_intra[hh], vb.reshape(rows, d), preferred_element_type=F32)
        sout_ref[:, hh] = s * g_last[hh] + jnp.einsum("bcd,bce->bde", k_dec[hh], vb,
                                                      preferred_element_type=F32)
        o_ref[:, hh * d:(hh + 1) * d] = _rms(o, gain) * _silu(z_ref[:, hh * d:(hh + 1) * d])


def _delta_sample(proj, batch, seq, conv_bufs, states, j, conv_w, alog_rows, dtb_rows, out_gain):
    bb = LANES // seq
    assert batch % bb == 0 and seq == SUBLANES
    rows = bb * seq
    gw = HEAD_GROUP * HEAD_DIM
    ng = N_HEAD_GROUPS

    def col(base):
        return pl.BlockSpec((rows, gw), lambda i, g: (i, base * ng + g))

    def bufcol(base):
        return pl.BlockSpec((None, bb, CONV_W - 1, gw), lambda i, g: (j, i, 0, base * ng + g))

    def wcol(base):
        return pl.BlockSpec((CONV_W, gw), lambda i, g: (0, base * ng + g))

    ab_base = (4 * DELTA_W + X_W) // LANES
    state_spec = pl.BlockSpec((None, bb, HEAD_GROUP, HEAD_DIM, HEAD_DIM), lambda i, g: (j, i, g, 0, 0))
    return pl.pallas_call(
        functools.partial(_delta_sample_kernel, bb=bb, seq=seq),
        grid=(batch // bb, ng),
        in_specs=[
            col(0), col(1), col(2), col(3),
            pl.BlockSpec((rows, LANES), lambda i, g: (i, ab_base + g)),
            bufcol(0), bufcol(1), bufcol(2),
            state_spec,
            wcol(0), wcol(1), wcol(2),
            pl.BlockSpec((1, LANES), lambda i, g: (0, g)),
            pl.BlockSpec((1, LANES), lambda i, g: (0, g)),
            pl.BlockSpec((1, HEAD_DIM), lambda i, g: (0, 0)),
        ],
        out_specs=[
            pl.BlockSpec((rows, gw), lambda i, g: (i, g)),
            pl.BlockSpec((bb, HEAD_GROUP, HEAD_DIM, HEAD_DIM), lambda i, g: (i, g, 0, 0)),
        ],
        out_shape=[
            jax.ShapeDtypeStruct((batch * seq, DELTA_W), F32),
            jax.ShapeDtypeStruct((batch, DELTA_HEADS, HEAD_DIM, HEAD_DIM), F32),
        ],
        compiler_params=_params("parallel", "parallel"),
    )(proj, proj, proj, proj, proj, conv_bufs, conv_bufs, conv_bufs, states, conv_w, conv_w, conv_w,
      alog_rows, dtb_rows, out_gain)


def _gelu(x):
    return 0.5 * x * (1.0 + lax.erf(x * (2.0 ** -0.5)))


def _layernorm(x, gain, bias):
    mu = jnp.mean(x, axis=-1, keepdims=True)
    xc = x - mu
    var = jnp.mean(xc * xc, axis=-1, keepdims=True)
    return xc * lax.rsqrt(var + EPS) * gain + bias


def _gmlp_prompt_kernel(u_ref, v_ref, lng_ref, lnb_ref, ws_ref, bs_ref, o_ref, *, chunks):
    c = GMLP_CHUNK
    d = HEAD_DIM
    ri = lax.broadcasted_iota(jnp.int32, (c, c), 0)
    ci = lax.broadcasted_iota(jnp.int32, (c, c), 1)
    lower = ci <= ri
    v = _layernorm(_gelu(v_ref[...]), lng_ref[...], lnb_ref[...]).astype(BF16)
    bs = bs_ref[...]
    for g in range(GMLP_GROUPS):
        w = jnp.where(lower, ws_ref[g], 0.0).astype(BF16)
        bias = bs[:, g:g + 1]
        for ch in range(chunks):
            rs = slice(ch * c, (ch + 1) * c)
            cs = slice(g * d, (g + 1) * d)
            mixed = jnp.dot(w, v[rs, cs], preferred_element_type=F32) + bias
            o_ref[rs, cs] = _gelu(u_ref[rs, cs]) * mixed


def _gmlp_prompt(proj, ln_gain, ln_bias, w_s, b_s_t):
    m = proj.shape[0]
    chunks = 2
    tm = chunks * GMLP_CHUNK
    assert m % tm == 0
    return pl.pallas_call(
        functools.partial(_gmlp_prompt_kernel, chunks=chunks),
        grid=(m // tm,),
        in_specs=[
            pl.BlockSpec((tm, GMLP_W), lambda i: (i, 0)),
            pl.BlockSpec((tm, GMLP_W), lambda i: (i, 1)),
            pl.BlockSpec((1, GMLP_W), lambda i: (0, 0)),
            pl.BlockSpec((1, GMLP_W), lambda i: (0, 0)),
            pl.BlockSpec((GMLP_GROUPS, GMLP_CHUNK, GMLP_CHUNK), lambda i: (0, 0, 0)),
            pl.BlockSpec((GMLP_CHUNK, GMLP_GROUPS), lambda i: (0, 0)),
        ],
        out_specs=pl.BlockSpec((tm, GMLP_W), lambda i: (i, 0)),
        out_shape=jax.ShapeDtypeStruct((m, GMLP_W), F32),
        compiler_params=_params("parallel"),
    )(proj, proj, ln_gain, ln_bias, w_s, b_s_t)


def _gmlp_sample_kernel(u_ref, v_ref, lng_ref, lnb_ref, wx_ref, bx_ref, o_ref, vout_ref, *, bb, seq):
    v = _layernorm(_gelu(v_ref[...]), lng_ref[...], lnb_ref[...])
    vout_ref[...] = v
    v3 = v.reshape(bb, seq, GMLP_W)
    row = lax.broadcasted_iota(jnp.int32, (seq, 1), 0)
    mixed = jnp.broadcast_to(bx_ref[...][None], (bb, seq, GMLP_W))
    for j in range(seq):
        wj = jnp.where(row >= j, wx_ref[j], 0.0)
        mixed = mixed + wj[None] * v3[:, j:j + 1, :]
    o_ref[...] = _gelu(u_ref[...]) * mixed.reshape(bb * seq, GMLP_W)


def _gmlp_sample(proj, batch, seq, ln_gain, ln_bias, w_exp, b_exp):
    bb = 32
    assert batch % bb == 0 and seq == SUBLANES
    rows = bb * seq
    return pl.pallas_call(
        functools.partial(_gmlp_sample_kernel, bb=bb, seq=seq),
        grid=(batch // bb,),
        in_specs=[
            pl.BlockSpec((rows, GMLP_W), lambda i: (i, 0)),
            pl.BlockSpec((rows, GMLP_W), lambda i: (i, 1)),
            pl.BlockSpec((1, GMLP_W), lambda i: (0, 0)),
            pl.BlockSpec((1, GMLP_W), lambda i: (0, 0)),
            pl.BlockSpec((seq, seq, GMLP_W), lambda i: (0, 0, 0)),
            pl.BlockSpec((seq, GMLP_W), lambda i: (0, 0)),
        ],
        out_specs=[
            pl.BlockSpec((rows, GMLP_W), lambda i: (i, 0)),
            pl.BlockSpec((rows, GMLP_W), lambda i: (i, 0)),
        ],
        out_shape=[
            jax.ShapeDtypeStruct((batch * seq, GMLP_W), F32),
            jax.ShapeDtypeStruct((batch * seq, GMLP_W), F32),
        ],
        compiler_params=_params("parallel"),
    )(proj, proj, ln_gain, ln_bias, w_exp, b_exp)


def _softmax(s):
    m = jnp.max(s, axis=-1, keepdims=True)
    e = jnp.exp(s - m)
    return e / jnp.sum(e, axis=-1, keepdims=True)


def _mem_attn_prompt_kernel(q_ref, k_ref, v_ref, o_ref):
    d = HEAD_DIM
    for h in range(X_HEADS):
        sl = slice(h * d, (h + 1) * d)
        s = lax.dot_general(q_ref[:, sl].astype(BF16), k_ref[:, sl].astype(BF16),
                            (((1,), (1,)), ((), ())), preferred_element_type=F32) * (d ** -0.5)
        o_ref[:, sl] = _bdot(_softmax(s), v_ref[:, sl])


def _mem_attn_prompt(proj, q_block, batch, seq, mem_k, mem_v):
    tt = min(512, seq)
    nt = seq // tt
    n_mem = mem_k.shape[1]
    return pl.pallas_call(
        _mem_attn_prompt_kernel,
        grid=(batch, nt),
        in_specs=[
            pl.BlockSpec((tt, X_W), lambda b, t: (b * nt + t, q_block)),
            pl.BlockSpec((None, n_mem, X_W), lambda b, t: (b, 0, 0)),
            pl.BlockSpec((None, n_mem, X_W), lambda b, t: (b, 0, 0)),
        ],
        out_specs=pl.BlockSpec((tt, X_W), lambda b, t: (b * nt + t, 0)),
        out_shape=jax.ShapeDtypeStruct((batch * seq, X_W), F32),
        compiler_params=_params("parallel", "parallel"),
    )(proj, mem_k, mem_v)


def _mem_attn_sample_kernel(q_ref, k_ref, v_ref, o_ref, *, bb, seq):
    d = HEAD_DIM
    nq = X_HEADS * seq
    nk = k_ref.shape[1]
    q = jnp.concatenate([q_ref[:, h * d:(h + 1) * d].reshape(bb, seq, d) for h in range(X_HEADS)], axis=1)
    s = jnp.einsum("bqd,bkd->bqk", q.astype(BF16), k_ref[...].astype(BF16),
                   preferred_element_type=F32) * (d ** -0.5)
    q_head = lax.broadcasted_iota(jnp.int32, (nq, nk), 0) // seq
    k_head = lax.broadcasted_iota(jnp.int32, (nq, nk), 1) % X_HEADS
    p = _softmax(jnp.where(q_head == k_head, s, -1e30)).astype(BF16)
    o = jnp.einsum("bqk,bkd->bqd", p, v_ref[...].astype(BF16), preferred_element_type=F32)
    for h in range(X_HEADS):
        o_ref[:, h * d:(h + 1) * d] = o[:, h * seq:(h + 1) * seq, :].reshape(bb * seq, d)


def _mem_attn_sample(proj, q_block, batch, seq, mem_k, mem_v, layer):
    bb = 8
    rows = bb * seq
    n_rows = mem_k.shape[2]
    return pl.pallas_call(
        functools.partial(_mem_attn_sample_kernel, bb=bb, seq=seq),
        grid=(batch // bb,),
        in_specs=[
            pl.BlockSpec((rows, X_W), lambda i: (i, q_block)),
            pl.BlockSpec((None, bb, n_rows, HEAD_DIM), lambda i: (layer, i, 0, 0)),
            pl.BlockSpec((None, bb, n_rows, HEAD_DIM), lambda i: (layer, i, 0, 0)),
        ],
        out_specs=pl.BlockSpec((rows, X_W), lambda i: (i, 0)),
        out_shape=jax.ShapeDtypeStruct((batch * seq, X_W), F32),
        compiler_params=_params("parallel"),
    )(proj, mem_k, mem_v)


def _delta_in_weight(w_in_a):
    d_model = w_in_a.shape[0]
    qkvz = w_in_a[:, :4 * DELTA_W]
    a = w_in_a[:, 4 * DELTA_W:4 * DELTA_W + DELTA_HEADS]
    b = w_in_a[:, 4 * DELTA_W + DELTA_HEADS:4 * DELTA_W + 2 * DELTA_HEADS]
    xq = w_in_a[:, 4 * DELTA_W + 2 * DELTA_HEADS:]
    parts = [qkvz, xq]
    pad = jnp.zeros((d_model, LANES - 2 * HEAD_GROUP), w_in_a.dtype)
    for g in range(N_HEAD_GROUPS):
        hs = slice(g * HEAD_GROUP, (g + 1) * HEAD_GROUP)
        parts += [a[:, hs], b[:, hs], pad]
    parts.append(jnp.zeros((d_model, AB_W - N_HEAD_GROUPS * LANES), w_in_a.dtype))
    return jnp.concatenate(parts, axis=1).astype(BF16)


def _gate_rows(vec):
    row = jnp.zeros((N_HEAD_GROUPS, LANES), F32)
    row = row.at[:, :HEAD_GROUP].set(vec.astype(F32).reshape(N_HEAD_GROUPS, HEAD_GROUP))
    return jnp.concatenate([row.reshape(1, -1), jnp.zeros((1, AB_W - N_HEAD_GROUPS * LANES), F32)], axis=1)


def _ffn_step(x, p, layer, slot, pre, post, round_next):
    w_gu, w_dn = p["ffn_bf16"][(layer, slot)]
    nxt = (layer, 1) if slot == 0 else (layer + 1, 0)
    if round_next and nxt[0] < p["w_out"].shape[0]:
        x, gu_next, dn_next = _ffn(x, p["gains"], layer * 6, pre, post, w_gu, w_dn,
                                   cast_next=(p["w_ffn_gu"], p["w_ffn_dn"]) + nxt)
        p["ffn_bf16"][nxt] = (gu_next, dn_next)
        return x
    return _ffn(x, p["gains"], layer * 6, pre, post, w_gu, w_dn)


def _trunk(x, batch, seq, is_prompt, mem_k, mem_v, conv_bufs, delta_states, p):
    new_conv, new_delta, new_v = [], [], []
    gains = p["gains"]
    depth = p["w_out"].shape[0]
    for i in range(depth):
        j = i // N_MIXERS
        x = _ffn_step(x, p, i, 0, 0, 1, is_prompt)
        if i % N_MIXERS == 0:
            proj = _norm_matmul(x, gains, i * 6 + 2, p["w_in_a"][j], 1024)
            q_block = 4 * DELTA_W // X_W
            if is_prompt:
                mix, s_new = _delta_prompt(proj, batch, seq, p["conv_w"][j], p["alog_rows"][j],
                                           p["dtb_rows"][j], p["delta_norm_gain"][j])
            else:
                mix, s_new = _delta_sample(proj, batch, seq, conv_bufs, delta_states, j, p["conv_w"][j],
                                           p["alog_rows"][j], p["dtb_rows"][j], p["delta_norm_gain"][j])
            new_conv.append(proj.reshape(batch, seq, -1)[:, seq - (CONV_W - 1):, :3 * DELTA_W])
            new_delta.append(s_new)
        else:
            proj = _norm_matmul(x, gains, i * 6 + 2, p["w_in_b"][j], 896)
            q_block = 2 * GMLP_W // X_W
            if is_prompt:
                mix = _gmlp_prompt(proj, p["gmlp_ln_gain"][j], p["gmlp_ln_bias"][j], p["w_spatial"][j],
                                   p["b_spatial_t"][j])
            else:
                mix, v_rows = _gmlp_sample(proj, batch, seq, p["gmlp_ln_gain"][j], p["gmlp_ln_bias"][j],
                                           p["w_spatial_exp"][j], p["b_spatial_exp"][j])
                new_v.append(v_rows.reshape(batch, seq, GMLP_W))
        if is_prompt:
            mem_out = _mem_attn_prompt(proj, q_block, batch, seq, mem_k[i], mem_v[i])
        else:
            mem_out = _mem_attn_sample(proj, q_block, batch, seq, mem_k, mem_v, i)
        x = _out_proj(x, mix, mem_out, p["w_out"], gains, i)
        x = _ffn_step(x, p, i, 1, 4, 5, is_prompt)
    return x, new_conv, new_delta, new_v


def kernel(x_prompt, x_sample, mem_prompt, cache_mem_k, cache_mem_v, state_delta, state_conv, norm_gains, w_ffn_gu, w_ffn_dn, w_in_a, conv_w, a_log, dt_bias, delta_norm_gain, w_in_b, gmlp_ln_gain, gmlp_ln_bias, w_spatial, b_spatial, mem_norm_gain, w_mem_kv, w_out):
    batch, seq, d_model = x_prompt.shape
    dec_batch, dec_seq, _ = x_sample.shape
    depth = w_out.shape[0]
    n_mem = mem_prompt.shape[1]
    n_a = w_in_a.shape[0]
    n_b = w_in_b.shape[0]

    w_s_dec = w_spatial[:, :, :dec_seq, :dec_seq]
    p = {
        "gains": norm_gains.reshape(depth * 6, 1, d_model),
        "w_ffn_gu": w_ffn_gu,
        "w_ffn_dn": w_ffn_dn,
        "ffn_bf16": {(0, 0): (w_ffn_gu[0, 0].astype(BF16), w_ffn_dn[0, 0].astype(BF16))},
        "w_in_a": [_delta_in_weight(w_in_a[j]) for j in range(n_a)],
        "w_in_b": [w_in_b[j].astype(BF16) for j in range(n_b)],
        "w_out": w_out.astype(BF16),
        "conv_w": conv_w,
        "alog_rows": [_gate_rows(a_log[j]) for j in range(n_a)],
        "dtb_rows": [_gate_rows(dt_bias[j]) for j in range(n_a)],
        "delta_norm_gain": delta_norm_gain.reshape(n_a, 1, HEAD_DIM),
        "gmlp_ln_gain": gmlp_ln_gain.reshape(n_b, 1, GMLP_W),
        "gmlp_ln_bias": gmlp_ln_bias.reshape(n_b, 1, GMLP_W),
        "w_spatial": w_spatial,
        "b_spatial_t": jnp.swapaxes(b_spatial, 1, 2),
        "w_spatial_exp": jnp.repeat(jnp.transpose(w_s_dec, (0, 3, 2, 1)), HEAD_DIM, axis=-1),
        "b_spatial_exp": jnp.repeat(jnp.swapaxes(b_spatial[:, :, :dec_seq], 1, 2), HEAD_DIM, axis=-1),
    }

    mem2d = mem_prompt.reshape(batch * n_mem, d_model)
    mem_gains = mem_norm_gain.reshape(depth, 1, d_model)
    w_kv = w_mem_kv.astype(BF16)
    kvs = [_norm_matmul(mem2d, mem_gains, i, w_kv[i], 1024) for i in range(depth)]
    mem_k_prompt = jnp.stack([kv[:, :X_W].reshape(batch, n_mem, X_W) for kv in kvs])
    mem_v_prompt = jnp.stack([kv[:, X_W:].reshape(batch, n_mem, X_W) for kv in kvs])

    y_prompt, conv_p, delta_p, _ = _trunk(x_prompt.reshape(batch * seq, d_model), batch, seq, True,
                                          mem_k_prompt, mem_v_prompt, None, None, p)

    y_sample, conv_s, delta_s, v_s = _trunk(
        x_sample.reshape(dec_batch * dec_seq, d_model), dec_batch, dec_seq, False,
        cache_mem_k.reshape(depth, dec_batch, n_mem * X_HEADS, HEAD_DIM),
        cache_mem_v.reshape(depth, dec_batch, n_mem * X_HEADS, HEAD_DIM),
        state_conv, state_delta, p)

    return (y_prompt.reshape(batch, seq, d_model),
            y_sample.reshape(dec_batch, dec_seq, d_model),
            mem_k_prompt.reshape(depth, batch, n_mem, X_HEADS, HEAD_DIM),
            mem_v_prompt.reshape(depth, batch, n_mem, X_HEADS, HEAD_DIM),
            jnp.stack(delta_p), jnp.stack(conv_p), jnp.stack(delta_s), jnp.stack(conv_s), jnp.stack(v_s))
```

```python
import functools

import jax
import jax.numpy as jnp
from jax import lax
from jax.experimental import pallas as pl
from jax.experimental.pallas import tpu as pltpu

F32 = jnp.float32
BF16 = jnp.bfloat16

EPS = 1e-6
HEAD_DIM = 128
X_HEADS = 4
X_W = X_HEADS * HEAD_DIM
DELTA_HEADS = 12
DELTA_W = DELTA_HEADS * HEAD_DIM
CONV_W = 4
DELTA_CHUNK = 64
GMLP_GROUPS = 12
GMLP_W = GMLP_GROUPS * HEAD_DIM
GMLP_CHUNK = 128
N_MIXERS = 2

LANES = 128
SUBLANES = 8
VMEM_LIMIT = 56 * 1024 * 1024
FFN_VMEM_LIMIT = 62 * 1024 * 1024

HEAD_GROUP = 4
N_HEAD_GROUPS = DELTA_HEADS // HEAD_GROUP
FFN_ROWS = 1024
FFN_COLS = 512
PROJ_ROWS = 1024
PROJ_COLS = 1792
AB_W = 512
IN_A_W = 4 * DELTA_W + X_W + AB_W


def _params(*sem, vmem=VMEM_LIMIT):
    return pltpu.CompilerParams(dimension_semantics=sem, vmem_limit_bytes=vmem)


def _rms(x, gain):
    ms = jnp.mean(x * x, axis=-1, keepdims=True)
    return x * lax.rsqrt(ms + EPS) * gain


def _sigmoid(x):
    return 0.5 * jnp.tanh(0.5 * x) + 0.5


def _silu(x):
    return x * _sigmoid(x)


def _softplus(x):
    return jnp.maximum(x, 0.0) + jnp.log1p(jnp.exp(-jnp.abs(x)))


def _bdot(a, b):
    return jnp.dot(a.astype(BF16), b.astype(BF16), preferred_element_type=F32)


def _ffn_cast_kernel(x_ref, gpre_ref, gpost_ref, wg_ref, wu_ref, wd_ref, cgu_ref, cdn_ref,
                     o_ref, ogu_ref, odn_ref, hn_ref, act_ref):
    ogu_ref[...] = cgu_ref[...].astype(BF16)
    odn_ref[...] = cdn_ref[...].astype(BF16)
    _ffn_kernel(x_ref, gpre_ref, gpost_ref, wg_ref, wu_ref, wd_ref, o_ref, hn_ref, act_ref)


def _ffn_kernel(x_ref, gpre_ref, gpost_ref, wg_ref, wu_ref, wd_ref, o_ref, hn_ref, act_ref):
    f = pl.program_id(1)
    last = pl.num_programs(1) - 1

    def gate_up(slot):
        h = hn_ref[...]
        gate = jnp.dot(h, wg_ref[...], preferred_element_type=F32)
        up = jnp.dot(h, wu_ref[...], preferred_element_type=F32)
        act_ref[slot] = (_silu(gate) * up).astype(BF16)

    def down(slot):
        o_ref[...] += jnp.dot(act_ref[slot], wd_ref[...], preferred_element_type=F32)

    @pl.when(f == 0)
    def _():
        hn_ref[...] = _rms(x_ref[...], gpre_ref[...]).astype(BF16)
        o_ref[...] = jnp.zeros_like(o_ref)
        gate_up(0)

    @pl.when((f > 0) & (f < last))
    def _():
        down((f - 1) % 2)
        gate_up(f % 2)

    @pl.when(f == last)
    def _():
        down((f - 1) % 2)
        o_ref[...] = x_ref[...] + 0.5 * _rms(o_ref[...], gpost_ref[...])


def _ffn(x, gains, gain_base, pre, post, w_gu, w_dn, cast_next=None):
    m, d = x.shape
    ffn = w_dn.shape[0]
    tm = min(FFN_ROWS, m)
    tf = FFN_COLS
    nf = ffn // tf
    nm = m // tm
    assert m % tm == 0 and ffn % tf == 0
    def up_blk(f):
        return jnp.minimum(f, nf - 1)

    def dn_blk(f):
        return jnp.maximum(f - 1, 0)

    in_specs = [
        pl.BlockSpec((tm, d), lambda i, f: (i, 0)),
        pl.BlockSpec((None, 1, d), lambda i, f: (gain_base + pre, 0, 0)),
        pl.BlockSpec((None, 1, d), lambda i, f: (gain_base + post, 0, 0)),
        pl.BlockSpec((d, tf), lambda i, f: (0, up_blk(f))),
        pl.BlockSpec((d, tf), lambda i, f: (0, nf + up_blk(f))),
        pl.BlockSpec((tf, d), lambda i, f: (dn_blk(f), 0)),
    ]
    out_specs = pl.BlockSpec((tm, d), lambda i, f: (i, 0))
    out_shape = jax.ShapeDtypeStruct((m, d), F32)
    scratch = [pltpu.VMEM((tm, d), BF16), pltpu.VMEM((2, tm, tf), BF16)]
    if cast_next is None:
        return pl.pallas_call(
            _ffn_kernel, grid=(nm, nf + 1), in_specs=in_specs, out_specs=out_specs, out_shape=out_shape,
            scratch_shapes=scratch, compiler_params=_params("parallel", "arbitrary", vmem=FFN_VMEM_LIMIT),
        )(x, gains, gains, w_gu, w_gu, w_dn)
    gu_all, dn_all, layer, slot = cast_next
    gu_rows, gu_cols = d // nm, 2 * ffn // nf
    dn_rows = ffn // (nm * nf)
    assert gu_rows * nm == d and gu_cols * nf == 2 * ffn and dn_rows * nm * nf == ffn
    assert gu_rows % 16 == 0 and gu_cols % LANES == 0 and dn_rows % 16 == 0
    return pl.pallas_call(
        _ffn_cast_kernel,
        grid=(nm, nf + 1),
        in_specs=in_specs + [
            pl.BlockSpec((None, None, gu_rows, gu_cols), lambda i, f: (layer, slot, i, up_blk(f))),
            pl.BlockSpec((None, None, dn_rows, d), lambda i, f: (layer, slot, i * nf + up_blk(f), 0)),
        ],
        out_specs=[
            out_specs,
            pl.BlockSpec((gu_rows, gu_cols), lambda i, f: (i, up_blk(f))),
            pl.BlockSpec((dn_rows, d), lambda i, f: (i * nf + up_blk(f), 0)),
        ],
        out_shape=[out_shape, jax.ShapeDtypeStruct((d, 2 * ffn), BF16), jax.ShapeDtypeStruct((ffn, d), BF16)],
        scratch_shapes=scratch,
        compiler_params=_params("parallel", "arbitrary", vmem=FFN_VMEM_LIMIT),
    )(x, gains, gains, w_gu, w_gu, w_dn, gu_all, dn_all)


def _norm_matmul_kernel(x_ref, g_ref, w_ref, o_ref, hn_ref):
    @pl.when(pl.program_id(1) == 0)
    def _():
        hn_ref[...] = _rms(x_ref[...], g_ref[...]).astype(BF16)

    o_ref[...] = jnp.dot(hn_ref[...], w_ref[...], preferred_element_type=F32)


def _norm_matmul(x, gains, gain_idx, w, tn):
    m, d = x.shape
    n = w.shape[1]
    tm = min(PROJ_ROWS, m)
    assert m % tm == 0 and n % tn == 0
    return pl.pallas_call(
        _norm_matmul_kernel,
        grid=(m // tm, n // tn),
        in_specs=[
            pl.BlockSpec((tm, d), lambda i, j: (i, 0)),
            pl.BlockSpec((None, 1, d), lambda i, j: (gain_idx, 0, 0)),
            pl.BlockSpec((d, tn), lambda i, j: (0, j)),
        ],
        out_specs=pl.BlockSpec((tm, tn), lambda i, j: (i, j)),
        out_shape=jax.ShapeDtypeStruct((m, n), F32),
        scratch_shapes=[pltpu.VMEM((tm, d), BF16)],
        compiler_params=_params("parallel", "arbitrary"),
    )(x, gains, w)


def _out_proj_kernel(x_ref, mix_ref, mem_ref, wmix_ref, wmem_ref, g_ref, o_ref):
    y = _bdot(mix_ref[...], wmix_ref[...]) + _bdot(mem_ref[...], wmem_ref[...])
    o_ref[...] = x_ref[...] + _rms(y, g_ref[...])


def _out_proj(x, mix, mem, w_out, gains, layer):
    m, d = x.shape
    wm = mix.shape[1]
    wx = mem.shape[1]
    tm = min(512, m)
    assert wm % wx == 0
    return pl.pallas_call(
        _out_proj_kernel,
        grid=(m // tm,),
        in_specs=[
            pl.BlockSpec((tm, d), lambda i: (i, 0)),
            pl.BlockSpec((tm, wm), lambda i: (i, 0)),
            pl.BlockSpec((tm, wx), lambda i: (i, 0)),
            pl.BlockSpec((None, wm, d), lambda i: (layer, 0, 0)),
            pl.BlockSpec((None, wx, d), lambda i: (layer, wm // wx, 0)),
            pl.BlockSpec((None, 1, d), lambda i: (layer * 6 + 3, 0, 0)),
        ],
        out_specs=pl.BlockSpec((tm, d), lambda i: (i, 0)),
        out_shape=jax.ShapeDtypeStruct((m, d), F32),
        compiler_params=_params("parallel"),
    )(x, mix, mem, w_out, w_out, gains)


def _unit_lower_inverse(lmat, c, span=None):
    span = c if span is None else span
    base = 4
    ri = lax.broadcasted_iota(jnp.int32, (c, c), 0)
    ci = lax.broadcasted_iota(jnp.int32, (c, c), 1)
    eye = (ri == ci).astype(F32)
    l_hi, l_lo = _split(lmat)

    def masked(keep):
        return jnp.where(keep, l_hi, jnp.zeros_like(l_hi)), jnp.where(keep, l_lo, jnp.zeros_like(l_lo))

    same = (ri // base) == (ci // base)
    dm = masked(same)
    x = eye - jnp.where(same, lmat, 0.0)
    x = x + _mm3(_split(x), _split(_mm3(dm, dm)))
    size = base
    while size < span:
        inner = ((ri // (2 * size)) == (ci // (2 * size))) & ((ri // size) != (ci // size))
        xs = _split(x)
        x = x - _mm3(_split(_mm3(xs, masked(inner))), xs)
        size *= 2
    return x


def _split(x):
    hi = x.astype(BF16)
    return hi, (x - hi.astype(F32)).astype(BF16)


def _mm3(a, b):
    def mm(p, q):
        return jnp.einsum("nij,njk->nik", p, q, preferred_element_type=F32)

    width = b[0].shape[-1]
    if a[0].shape[-1] % LANES == 0 and width % LANES == 0:
        lhs = jnp.concatenate([a[0], a[1]], axis=-1)
        rhs = jnp.concatenate([jnp.concatenate([b[0], b[1]], axis=-1),
                               jnp.concatenate([b[0], jnp.zeros_like(b[0])], axis=-1)], axis=-2)
        r = mm(lhs, rhs)
        return r[..., :width] + r[..., width:]
    return mm(a[0], b[0]) + (mm(a[0], b[1]) + mm(a[1], b[0]))


def _delta_pair_terms(q, k, v, gcol, bcol):
    n, cp, d = q.shape
    c = DELTA_CHUNK
    wu, a_intra, egc = _wy_solve(q, k, v, gcol, bcol, c)
    wu = wu.astype(BF16)
    awu = jnp.einsum("nij,nje->nie", a_intra, wu, preferred_element_type=F32)
    qt = (q * egc - awu[..., :d]).astype(BF16)
    o0 = awu[..., d:]
    k64 = k.reshape(2 * n, c, d)
    g64 = gcol.reshape(2 * n, c, 1)
    glast = g64[:, c - 1:c, :]
    k_dec = (k64 * jnp.exp(glast - g64)).astype(BF16)
    mb = jnp.einsum("ncd,nce->nde", k_dec, wu.reshape(2 * n, c, 2 * d), preferred_element_type=F32)
    return (qt.reshape(2 * n, c, d), mb[..., :d].astype(BF16), mb[..., d:], o0.reshape(2 * n, c, d),
            jnp.exp(glast))


def _wy_solve(q, k, v, gcol, bcol, span):
    rows = q.shape[1]
    ri = lax.broadcasted_iota(jnp.int32, (rows, rows), 0)
    ci = lax.broadcasted_iota(jnp.int32, (rows, rows), 1)
    eye = (ri == ci).astype(F32)
    same = (ri // span) == (ci // span)
    strict = same & (ci < ri)
    causal = same & (ci <= ri)
    grow = jnp.sum(gcol * eye, axis=1, keepdims=True)
    diff = gcol - grow
    kb = k.astype(BF16)
    kk = jnp.einsum("ncd,nmd->ncm", kb, kb, preferred_element_type=F32)
    lmat = jnp.where(strict, bcol * kk * jnp.exp(jnp.where(strict, diff, 0.0)), 0.0)
    tinv = _unit_lower_inverse(lmat, rows, span)
    egc = jnp.exp(gcol)
    rhs = jnp.concatenate([k * (bcol * egc), v * bcol], axis=-1)
    wu = _mm3(_split(tinv), _split(rhs))
    qk = jnp.einsum("ncd,nmd->ncm", q.astype(BF16), kb, preferred_element_type=F32)
    a_intra = jnp.where(causal, qk * jnp.exp(jnp.where(causal, diff, 0.0)), 0.0).astype(BF16)
    return wu, a_intra, egc


def _l2norm(x):
    return x * lax.rsqrt(jnp.sum(x * x, axis=-1, keepdims=True) + EPS)


def _block_cumsum(g, c):
    rows, lanes = g.shape
    n = rows // c
    ri = lax.broadcasted_iota(jnp.int32, (n, c, c), 1)
    ci = lax.broadcasted_iota(jnp.int32, (n, c, c), 2)
    tri = (ci <= ri).astype(BF16)
    g3 = g.reshape(n, c, lanes)
    hi = g3.astype(BF16)
    r1 = g3 - hi.astype(F32)
    mid = r1.astype(BF16)
    lo = (r1 - mid.astype(F32)).astype(BF16)

    def mm(p):
        return jnp.einsum("nij,njk->nik", tri, p, preferred_element_type=F32)

    return (mm(hi) + (mm(mid) + mm(lo))).reshape(rows, lanes)


def _shifted(x, prev, shift, row):
    xs = pltpu.roll(x, shift, x.ndim - 2)
    if x.ndim == 2:
        ps = pltpu.roll(prev, shift, 0)
        reps = x.shape[0] // SUBLANES
        ps = jnp.broadcast_to(ps[None], (reps,) + ps.shape).reshape(x.shape)
        return jnp.where(row < shift, ps, xs)
    for t in range(shift):
        src = prev.shape[1] - shift + t
        xs = jnp.where(row == t, prev[:, src:src + 1, :], xs)
    return xs


def _short_conv_silu(x, prev, w, row):
    y = x * w[CONV_W - 1:CONV_W]
    for shift in range(1, CONV_W):
        tap = CONV_W - 1 - shift
        y = y + _shifted(x, prev, shift, row) * w[tap:tap + 1]
    return _silu(y)


def _delta_prompt_kernel(q_ref, k_ref, v_ref, z_ref, ab_ref, wq_ref, wk_ref, wv_ref, alog_ref, dtb_ref,
                         og_ref, o_ref, sout_ref, s_ref, pq_ref, pk_ref, pv_ref, bq_ref, bk_ref, bv_ref, *, tt):
    t = pl.program_id(2)
    c = DELTA_CHUNK
    n = tt // c
    d = HEAD_DIM

    @pl.when(t == 0)
    def _():
        s_ref[...] = jnp.zeros_like(s_ref)
        for ref in (pq_ref, pk_ref, pv_ref, bq_ref, bk_ref, bv_ref):
            ref[...] = jnp.zeros_like(ref)

    row = lax.broadcasted_iota(jnp.int32, (tt, 1), 0)

    def conv(x_ref, p_ref, b_ref, w_ref):
        x = x_ref[...]
        w = w_ref[...]
        x1 = _shifted(x, p_ref[...], 1, row)
        b = x * w[1:2] + x1 * w[0:1]
        y = x * w[3:4] + x1 * w[2:3] + _shifted(b, b_ref[...], 2, row)
        p_ref[...] = x[tt - SUBLANES:, :]
        b_ref[...] = b[tt - SUBLANES:, :]
        return _silu(y)

    assert CONV_W == 4
    qc = conv(q_ref, pq_ref, bq_ref, wq_ref)
    kc = conv(k_ref, pk_ref, bk_ref, wk_ref)
    vc = conv(v_ref, pv_ref, bv_ref, wv_ref)

    ab = ab_ref[...]
    g_all = -jnp.exp(alog_ref[...]) * _softplus(ab + dtb_ref[...])
    beta_all = _sigmoid(ab)
    gc_all = _block_cumsum(g_all, c)

    heads = range(HEAD_GROUP)

    def pairs(cols):
        return jnp.concatenate([x.reshape(n // 2, 2 * c, x.shape[-1]) for x in cols], axis=0)

    q = pairs([_l2norm(qc[:, hh * d:(hh + 1) * d]) * (d ** -0.5) for hh in heads])
    k = pairs([_l2norm(kc[:, hh * d:(hh + 1) * d]) for hh in heads])
    v = pairs([vc[:, hh * d:(hh + 1) * d] for hh in heads])
    gcol = pairs([gc_all[:, hh:hh + 1] for hh in heads])
    bcol = pairs([beta_all[:, HEAD_GROUP + hh:HEAD_GROUP + hh + 1] for hh in heads])
    qt, m, bmat, o0, g_last = _delta_pair_terms(q, k, v, gcol, bcol)

    gain = og_ref[...]
    states = [s_ref[hh] for hh in heads]
    for ch in range(n):
        for hh in heads:
            i = hh * n + ch
            s = states[hh]
            r = jnp.dot(jnp.concatenate([qt[i], m[i]], axis=0), s.astype(BF16),
                        preferred_element_type=F32)
            o = r[:c] + o0[i]
            states[hh] = s * g_last[i] + (bmat[i] - r[c:])
            zz = z_ref[ch * c:(ch + 1) * c, hh * d:(hh + 1) * d]
            o_ref[ch * c:(ch + 1) * c, hh * d:(hh + 1) * d] = _rms(o, gain) * _silu(zz)
    for hh in range(HEAD_GROUP):
        s_ref[hh] = states[hh]
        sout_ref[hh] = states[hh]


def _delta_prompt(proj, batch, seq, conv_w, alog_rows, dtb_rows, out_gain):
    tt = min(512, seq)
    nt = seq // tt
    gw = HEAD_GROUP * HEAD_DIM
    ng = N_HEAD_GROUPS
    assert seq % tt == 0 and tt % DELTA_CHUNK == 0

    def col(base):
        return pl.BlockSpec((tt, gw), lambda b, g, t: (b * nt + t, base * ng + g))

    def wcol(base):
        return pl.BlockSpec((CONV_W, gw), lambda b, g, t: (0, base * ng + g))

    ab_base = (4 * DELTA_W + X_W) // LANES
    return pl.pallas_call(
        functools.partial(_delta_prompt_kernel, tt=tt),
        grid=(batch, ng, nt),
        in_specs=[
            col(0), col(1), col(2), col(3),
            pl.BlockSpec((tt, LANES), lambda b, g, t: (b * nt + t, ab_base + g)),
            wcol(0), wcol(1), wcol(2),
            pl.BlockSpec((1, LANES), lambda b, g, t: (0, g)),
            pl.BlockSpec((1, LANES), lambda b, g, t: (0, g)),
            pl.BlockSpec((1, HEAD_DIM), lambda b, g, t: (0, 0)),
        ],
        out_specs=[
            pl.BlockSpec((tt, gw), lambda b, g, t: (b * nt + t, g)),
            pl.BlockSpec((None, HEAD_GROUP, HEAD_DIM, HEAD_DIM), lambda b, g, t: (b, g, 0, 0)),
        ],
        out_shape=[
            jax.ShapeDtypeStruct((batch * seq, DELTA_W), F32),
            jax.ShapeDtypeStruct((batch, DELTA_HEADS, HEAD_DIM, HEAD_DIM), F32),
        ],
        scratch_shapes=[
            pltpu.VMEM((HEAD_GROUP, HEAD_DIM, HEAD_DIM), F32),
        ] + [pltpu.VMEM((SUBLANES, gw), F32)] * 6,
        compiler_params=_params("parallel", "parallel", "arbitrary"),
    )(proj, proj, proj, proj, proj, conv_w, conv_w, conv_w, alog_rows, dtb_rows, out_gain)


def _delta_sample_kernel(q_ref, k_ref, v_ref, z_ref, ab_ref, bq_ref, bk_ref, bv_ref, s0_ref, wq_ref, wk_ref,
                         wv_ref, alog_ref, dtb_ref, og_ref, o_ref, sout_ref, *, bb, seq):
    d = HEAD_DIM
    rows = bb * seq
    row = lax.broadcasted_iota(jnp.int32, (1, seq, 1), 1)

    def conv(x_ref, buf_ref, w_ref):
        x = x_ref[...].reshape(bb, seq, x_ref.shape[1])
        return _short_conv_silu(x, buf_ref[...], w_ref[...], row).reshape(rows, x_ref.shape[1])

    qc = conv(q_ref, bq_ref, wq_ref)
    kc = conv(k_ref, bk_ref, wk_ref)
    vc = conv(v_ref, bv_ref, wv_ref)

    ab = ab_ref[...]
    g_all = -jnp.exp(alog_ref[...]) * _softplus(ab + dtb_ref[...])
    beta_all = _sigmoid(ab)
    gc_all = _block_cumsum(g_all, seq)

    heads = range(HEAD_GROUP)
    q = jnp.stack([_l2norm(qc[:, hh * d:(hh + 1) * d]) * (d ** -0.5) for hh in heads])
    k = jnp.stack([_l2norm(kc[:, hh * d:(hh + 1) * d]) for hh in heads])
    v = jnp.stack([vc[:, hh * d:(hh + 1) * d] for hh in heads])
    gcol = jnp.stack([gc_all[:, hh:hh + 1] for hh in heads])
    bcol = jnp.stack([beta_all[:, HEAD_GROUP + hh:HEAD_GROUP + hh + 1] for hh in heads])
    wu, a_intra, egc = _wy_solve(q, k, v, gcol, bcol, seq)
    q_dec = (q * egc).astype(BF16)
    g4 = gcol.reshape(HEAD_GROUP, bb, seq, 1)
    glast = g4[:, :, seq - 1:seq, :]
    k_dec = (k.reshape(HEAD_GROUP, bb, seq, d) * jnp.exp(glast - g4)).astype(BF16)
    g_last = jnp.exp(glast)

    gain = og_ref[...]
    for hh in heads:
        s = s0_ref[:, hh]
        w = wu[hh, :, :d].astype(BF16).reshape(bb, seq, d)
        u = wu[hh, :, d:].reshape(bb, seq, d)
        wq = jnp.concatenate([w, q_dec[hh].reshape(bb, seq, d)], axis=1)
        r = jnp.einsum("bcd,bde->bce", wq, s.astype(BF16), preferred_element_type=F32)
        v_new = u - r[:, :seq]
        vb = v_new.astype(BF16)
        o = r[:, seq:].reshape(rows, d) + jnp.dot(a_intra[hh], vb.reshape(rows, d), preferred_element_type=F32)
        sout_ref[:, hh] = s * g_last[hh] + jnp.einsum("bcd,bce->bde", k_dec[hh], vb,
                                                      preferred_element_type=F32)
        o_ref[:, hh * d:(hh + 1) * d] = _rms(o, gain) * _silu(z_ref[:, hh * d:(hh + 1) * d])


def _delta_sample(proj, batch, seq, conv_bufs, states, j, conv_w, alog_rows, dtb_rows, out_gain):
    bb = LANES // seq
    assert batch % bb == 0 and seq == SUBLANES
    rows = bb * seq
    gw = HEAD_GROUP * HEAD_DIM
    ng = N_HEAD_GROUPS

    def col(base):
        return pl.BlockSpec((rows, gw), lambda i, g: (i, base * ng + g))

    def bufcol(base):
        return pl.BlockSpec((None, bb, CONV_W - 1, gw), lambda i, g: (j, i, 0, base * ng + g))

    def wcol(base):
        return pl.BlockSpec((CONV_W, gw), lambda i, g: (0, base * ng + g))

    ab_base = (4 * DELTA_W + X_W) // LANES
    state_spec = pl.BlockSpec((None, bb, HEAD_GROUP, HEAD_DIM, HEAD_DIM), lambda i, g: (j, i, g, 0, 0))
    return pl.pallas_call(
        functools.partial(_delta_sample_kernel, bb=bb, seq=seq),
        grid=(batch // bb, ng),
        in_specs=[
            col(0), col(1), col(2), col(3),
            pl.BlockSpec((rows, LANES), lambda i, g: (i, ab_base + g)),
            bufcol(0), bufcol(1), bufcol(2),
            state_spec,
            wcol(0), wcol(1), wcol(2),
            pl.BlockSpec((1, LANES), lambda i, g: (0, g)),
            pl.BlockSpec((1, LANES), lambda i, g: (0, g)),
            pl.BlockSpec((1, HEAD_DIM), lambda i, g: (0, 0)),
        ],
        out_specs=[
            pl.BlockSpec((rows, gw), lambda i, g: (i, g)),
            pl.BlockSpec((bb, HEAD_GROUP, HEAD_DIM, HEAD_DIM), lambda i, g: (i, g, 0, 0)),
        ],
        out_shape=[
            jax.ShapeDtypeStruct((batch * seq, DELTA_W), F32),
            jax.ShapeDtypeStruct((batch, DELTA_HEADS, HEAD_DIM, HEAD_DIM), F32),
        ],
        compiler_params=_params("parallel", "parallel"),
    )(proj, proj, proj, proj, proj, conv_bufs, conv_bufs, conv_bufs, states, conv_w, conv_w, conv_w,
      alog_rows, dtb_rows, out_gain)


def _gelu(x):
    return 0.5 * x * (1.0 + lax.erf(x * (2.0 ** -0.5)))


def _layernorm(x, gain, bias):
    mu = jnp.mean(x, axis=-1, keepdims=True)
    xc = x - mu
    var = jnp.mean(xc * xc, axis=-1, keepdims=True)
    return xc * lax.rsqrt(var + EPS) * gain + bias


def _gmlp_prompt_kernel(u_ref, v_ref, lng_ref, lnb_ref, ws_ref, bs_ref, o_ref, *, chunks):
    c = GMLP_CHUNK
    d = HEAD_DIM
    ri = lax.broadcasted_iota(jnp.int32, (c, c), 0)
    ci = lax.broadcasted_iota(jnp.int32, (c, c), 1)
    lower = ci <= ri
    v = _layernorm(_gelu(v_ref[...]), lng_ref[...], lnb_ref[...]).astype(BF16)
    bs = bs_ref[...]
    for g in range(GMLP_GROUPS):
        w = jnp.where(lower, ws_ref[g], 0.0).astype(BF16)
        bias = bs[:, g:g + 1]
        for ch in range(chunks):
            rs = slice(ch * c, (ch + 1) * c)
            cs = slice(g * d, (g + 1) * d)
            mixed = jnp.dot(w, v[rs, cs], preferred_element_type=F32) + bias
            o_ref[rs, cs] = _gelu(u_ref[rs, cs]) * mixed


def _gmlp_prompt(proj, ln_gain, ln_bias, w_s, b_s_t):
    m = proj.shape[0]
    chunks = 2
    tm = chunks * GMLP_CHUNK
    assert m % tm == 0
    return pl.pallas_call(
        functools.partial(_gmlp_prompt_kernel, chunks=chunks),
        grid=(m // tm,),
        in_specs=[
            pl.BlockSpec((tm, GMLP_W), lambda i: (i, 0)),
            pl.BlockSpec((tm, GMLP_W), lambda i: (i, 1)),
            pl.BlockSpec((1, GMLP_W), lambda i: (0, 0)),
            pl.BlockSpec((1, GMLP_W), lambda i: (0, 0)),
            pl.BlockSpec((GMLP_GROUPS, GMLP_CHUNK, GMLP_CHUNK), lambda i: (0, 0, 0)),
            pl.BlockSpec((GMLP_CHUNK, GMLP_GROUPS), lambda i: (0, 0)),
        ],
        out_specs=pl.BlockSpec((tm, GMLP_W), lambda i: (i, 0)),
        out_shape=jax.ShapeDtypeStruct((m, GMLP_W), F32),
        compiler_params=_params("parallel"),
    )(proj, proj, ln_gain, ln_bias, w_s, b_s_t)


def _gmlp_sample_kernel(u_ref, v_ref, lng_ref, lnb_ref, wx_ref, bx_ref, o_ref, vout_ref, *, bb, seq):
    v = _layernorm(_gelu(v_ref[...]), lng_ref[...], lnb_ref[...])
    vout_ref[...] = v
    v3 = v.reshape(bb, seq, GMLP_W)
    row = lax.broadcasted_iota(jnp.int32, (seq, 1), 0)
    mixed = jnp.broadcast_to(bx_ref[...][None], (bb, seq, GMLP_W))
    for j in range(seq):
        wj = jnp.where(row >= j, wx_ref[j], 0.0)
        mixed = mixed + wj[None] * v3[:, j:j + 1, :]
    o_ref[...] = _gelu(u_ref[...]) * mixed.reshape(bb * seq, GMLP_W)


def _gmlp_sample(proj, batch, seq, ln_gain, ln_bias, w_exp, b_exp):
    bb = 32
    assert batch % bb == 0 and seq == SUBLANES
    rows = bb * seq
    return pl.pallas_call(
        functools.partial(_gmlp_sample_kernel, bb=bb, seq=seq),
        grid=(batch // bb,),
        in_specs=[
            pl.BlockSpec((rows, GMLP_W), lambda i: (i, 0)),
            pl.BlockSpec((rows, GMLP_W), lambda i: (i, 1)),
            pl.BlockSpec((1, GMLP_W), lambda i: (0, 0)),
            pl.BlockSpec((1, GMLP_W), lambda i: (0, 0)),
            pl.BlockSpec((seq, seq, GMLP_W), lambda i: (0, 0, 0)),
            pl.BlockSpec((seq, GMLP_W), lambda i: (0, 0)),
        ],
        out_specs=[
            pl.BlockSpec((rows, GMLP_W), lambda i: (i, 0)),
            pl.BlockSpec((rows, GMLP_W), lambda i: (i, 0)),
        ],
        out_shape=[
            jax.ShapeDtypeStruct((batch * seq, GMLP_W), F32),
            jax.ShapeDtypeStruct((batch * seq, GMLP_W), F32),
        ],
        compiler_params=_params("parallel"),
    )(proj, proj, ln_gain, ln_bias, w_exp, b_exp)


def _softmax(s):
    m = jnp.max(s, axis=-1, keepdims=True)
    e = jnp.exp(s - m)
    return e / jnp.sum(e, axis=-1, keepdims=True)


def _mem_attn_prompt_kernel(q_ref, k_ref, v_ref, o_ref):
    d = HEAD_DIM
    for h in range(X_HEADS):
        sl = slice(h * d, (h + 1) * d)
        s = lax.dot_general(q_ref[:, sl].astype(BF16), k_ref[:, sl].astype(BF16),
                            (((1,), (1,)), ((), ())), preferred_element_type=F32) * (d ** -0.5)
        o_ref[:, sl] = _bdot(_softmax(s), v_ref[:, sl])


def _mem_attn_prompt(proj, q_block, batch, seq, mem_k, mem_v):
    tt = min(512, seq)
    nt = seq // tt
    n_mem = mem_k.shape[1]
    return pl.pallas_call(
        _mem_attn_prompt_kernel,
        grid=(batch, nt),
        in_specs=[
            pl.BlockSpec((tt, X_W), lambda b, t: (b * nt + t, q_block)),
            pl.BlockSpec((None, n_mem, X_W), lambda b, t: (b, 0, 0)),
            pl.BlockSpec((None, n_mem, X_W), lambda b, t: (b, 0, 0)),
        ],
        out_specs=pl.BlockSpec((tt, X_W), lambda b, t: (b * nt + t, 0)),
        out_shape=jax.ShapeDtypeStruct((batch * seq, X_W), F32),
        compiler_params=_params("parallel", "parallel"),
    )(proj, mem_k, mem_v)


def _mem_attn_sample_kernel(q_ref, k_ref, v_ref, o_ref, *, bb, seq):
    d = HEAD_DIM
    nq = X_HEADS * seq
    nk = k_ref.shape[1]
    q = jnp.concatenate([q_ref[:, h * d:(h + 1) * d].reshape(bb, seq, d) for h in range(X_HEADS)], axis=1)
    s = jnp.einsum("bqd,bkd->bqk", q.astype(BF16), k_ref[...].astype(BF16),
                   preferred_element_type=F32) * (d ** -0.5)
    q_head = lax.broadcasted_iota(jnp.int32, (nq, nk), 0) // seq
    k_head = lax.broadcasted_iota(jnp.int32, (nq, nk), 1) % X_HEADS
    p = _softmax(jnp.where(q_head == k_head, s, -1e30)).astype(BF16)
    o = jnp.einsum("bqk,bkd->bqd", p, v_ref[...].astype(BF16), preferred_element_type=F32)
    for h in range(X_HEADS):
        o_ref[:, h * d:(h + 1) * d] = o[:, h * seq:(h + 1) * seq, :].reshape(bb * seq, d)


def _mem_attn_sample(proj, q_block, batch, seq, mem_k, mem_v, layer):
    bb = 8
    rows = bb * seq
    n_rows = mem_k.shape[2]
    return pl.pallas_call(
        functools.partial(_mem_attn_sample_kernel, bb=bb, seq=seq),
        grid=(batch // bb,),
        in_specs=[
            pl.BlockSpec((rows, X_W), lambda i: (i, q_block)),
            pl.BlockSpec((None, bb, n_rows, HEAD_DIM), lambda i: (layer, i, 0, 0)),
            pl.BlockSpec((None, bb, n_rows, HEAD_DIM), lambda i: (layer, i, 0, 0)),
        ],
        out_specs=pl.BlockSpec((rows, X_W), lambda i: (i, 0)),
        out_shape=jax.ShapeDtypeStruct((batch * seq, X_W), F32),
        compiler_params=_params("parallel"),
    )(proj, mem_k, mem_v)


def _delta_in_weight(w_in_a):
    d_model = w_in_a.shape[0]
    qkvz = w_in_a[:, :4 * DELTA_W]
    a = w_in_a[:, 4 * DELTA_W:4 * DELTA_W + DELTA_HEADS]
    b = w_in_a[:, 4 * DELTA_W + DELTA_HEADS:4 * DELTA_W + 2 * DELTA_HEADS]
    xq = w_in_a[:, 4 * DELTA_W + 2 * DELTA_HEADS:]
    parts = [qkvz, xq]
    pad = jnp.zeros((d_model, LANES - 2 * HEAD_GROUP), w_in_a.dtype)
    for g in range(N_HEAD_GROUPS):
        hs = slice(g * HEAD_GROUP, (g + 1) * HEAD_GROUP)
        parts += [a[:, hs], b[:, hs], pad]
    parts.append(jnp.zeros((d_model, AB_W - N_HEAD_GROUPS * LANES), w_in_a.dtype))
    return jnp.concatenate(parts, axis=1).astype(BF16)


def _gate_rows(vec):
    row = jnp.zeros((N_HEAD_GROUPS, LANES), F32)
    row = row.at[:, :HEAD_GROUP].set(vec.astype(F32).reshape(N_HEAD_GROUPS, HEAD_GROUP))
    return jnp.concatenate([row.reshape(1, -1), jnp.zeros((1, AB_W - N_HEAD_GROUPS * LANES), F32)], axis=1)


def _ffn_step(x, p, layer, slot, pre, post, round_next):
    w_gu, w_dn = p["ffn_bf16"][(layer, slot)]
    nxt = (layer, 1) if slot == 0 else (layer + 1, 0)
    if round_next and nxt[0] < p["w_out"].shape[0]:
        x, gu_next, dn_next = _ffn(x, p["gains"], layer * 6, pre, post, w_gu, w_dn,
                                   cast_next=(p["w_ffn_gu"], p["w_ffn_dn"]) + nxt)
        p["ffn_bf16"][nxt] = (gu_next, dn_next)
        return x
    return _ffn(x, p["gains"], layer * 6, pre, post, w_gu, w_dn)


def _trunk(x, batch, seq, is_prompt, mem_k, mem_v, conv_bufs, delta_states, p):
    new_conv, new_delta, new_v = [], [], []
    gains = p["gains"]
    depth = p["w_out"].shape[0]
    for i in range(depth):
        j = i // N_MIXERS
        x = _ffn_step(x, p, i, 0, 0, 1, is_prompt)
        if i % N_MIXERS == 0:
            proj = _norm_matmul(x, gains, i * 6 + 2, p["w_in_a"][j], PROJ_COLS)
            q_block = 4 * DELTA_W // X_W
            if is_prompt:
                mix, s_new = _delta_prompt(proj, batch, seq, p["conv_w"][j], p["alog_rows"][j],
                                           p["dtb_rows"][j], p["delta_norm_gain"][j])
            else:
                mix, s_new = _delta_sample(proj, batch, seq, conv_bufs, delta_states, j, p["conv_w"][j],
                                           p["alog_rows"][j], p["dtb_rows"][j], p["delta_norm_gain"][j])
            new_conv.append(proj.reshape(batch, seq, -1)[:, seq - (CONV_W - 1):, :3 * DELTA_W])
            new_delta.append(s_new)
        else:
            proj = _norm_matmul(x, gains, i * 6 + 2, p["w_in_b"][j], PROJ_COLS)
            q_block = 2 * GMLP_W // X_W
            if is_prompt:
                mix = _gmlp_prompt(proj, p["gmlp_ln_gain"][j], p["gmlp_ln_bias"][j], p["w_spatial"][j],
                                   p["b_spatial_t"][j])
            else:
                mix, v_rows = _gmlp_sample(proj, batch, seq, p["gmlp_ln_gain"][j], p["gmlp_ln_bias"][j],
                                           p["w_spatial_exp"][j], p["b_spatial_exp"][j])
                new_v.append(v_rows.reshape(batch, seq, GMLP_W))
        if is_prompt:
            mem_out = _mem_attn_prompt(proj, q_block, batch, seq, mem_k[i], mem_v[i])
        else:
            mem_out = _mem_attn_sample(proj, q_block, batch, seq, mem_k, mem_v, i)
        x = _out_proj(x, mix, mem_out, p["w_out"], gains, i)
        x = _ffn_step(x, p, i, 1, 4, 5, is_prompt)
    return x, new_conv, new_delta, new_v


def kernel(x_prompt, x_sample, mem_prompt, cache_mem_k, cache_mem_v, state_delta, state_conv, norm_gains, w_ffn_gu, w_ffn_dn, w_in_a, conv_w, a_log, dt_bias, delta_norm_gain, w_in_b, gmlp_ln_gain, gmlp_ln_bias, w_spatial, b_spatial, mem_norm_gain, w_mem_kv, w_out):
    batch, seq, d_model = x_prompt.shape
    dec_batch, dec_seq, _ = x_sample.shape
    depth = w_out.shape[0]
    n_mem = mem_prompt.shape[1]
    n_a = w_in_a.shape[0]
    n_b = w_in_b.shape[0]

    w_s_dec = w_spatial[:, :, :dec_seq, :dec_seq]
    p = {
        "gains": norm_gains.reshape(depth * 6, 1, d_model),
        "w_ffn_gu": w_ffn_gu,
        "w_ffn_dn": w_ffn_dn,
        "ffn_bf16": {(0, 0): (w_ffn_gu[0, 0].astype(BF16), w_ffn_dn[0, 0].astype(BF16))},
        "w_in_a": [_delta_in_weight(w_in_a[j]) for j in range(n_a)],
        "w_in_b": [w_in_b[j].astype(BF16) for j in range(n_b)],
        "w_out": w_out.astype(BF16),
        "conv_w": conv_w,
        "alog_rows": [_gate_rows(a_log[j]) for j in range(n_a)],
        "dtb_rows": [_gate_rows(dt_bias[j]) for j in range(n_a)],
        "delta_norm_gain": delta_norm_gain.reshape(n_a, 1, HEAD_DIM),
        "gmlp_ln_gain": gmlp_ln_gain.reshape(n_b, 1, GMLP_W),
        "gmlp_ln_bias": gmlp_ln_bias.reshape(n_b, 1, GMLP_W),
        "w_spatial": w_spatial,
        "b_spatial_t": jnp.swapaxes(b_spatial, 1, 2),
        "w_spatial_exp": jnp.repeat(jnp.transpose(w_s_dec, (0, 3, 2, 1)), HEAD_DIM, axis=-1),
        "b_spatial_exp": jnp.repeat(jnp.swapaxes(b_spatial[:, :, :dec_seq], 1, 2), HEAD_DIM, axis=-1),
    }

    mem2d = mem_prompt.reshape(batch * n_mem, d_model)
    mem_gains = mem_norm_gain.reshape(depth, 1, d_model)
    w_kv = w_mem_kv.astype(BF16)
    kvs = [_norm_matmul(mem2d, mem_gains, i, w_kv[i], 1024) for i in range(depth)]
    mem_k_prompt = jnp.stack([kv[:, :X_W].reshape(batch, n_mem, X_W) for kv in kvs])
    mem_v_prompt = jnp.stack([kv[:, X_W:].reshape(batch, n_mem, X_W) for kv in kvs])

    y_prompt, conv_p, delta_p, _ = _trunk(x_prompt.reshape(batch * seq, d_model), batch, seq, True,
                                          mem_k_prompt, mem_v_prompt, None, None, p)

    y_sample, conv_s, delta_s, v_s = _trunk(
        x_sample.reshape(dec_batch * dec_seq, d_model), dec_batch, dec_seq, False,
        cache_mem_k.reshape(depth, dec_batch, n_mem * X_HEADS, HEAD_DIM),
        cache_mem_v.reshape(depth, dec_batch, n_mem * X_HEADS, HEAD_DIM),
        state_conv, state_delta, p)

    return (y_prompt.reshape(batch, seq, d_model),
            y_sample.reshape(dec_batch, dec_seq, d_model),
            mem_k_prompt.reshape(depth, batch, n_mem, X_HEADS, HEAD_DIM),
            mem_v_prompt.reshape(depth, batch, n_mem, X_HEADS, HEAD_DIM),
            jnp.stack(delta_p), jnp.stack(conv_p), jnp.stack(delta_s), jnp.stack(conv_s), jnp.stack(v_s))
```

```python
import functools

import jax
import jax.numpy as jnp
from jax import lax
from jax.experimental import pallas as pl
from jax.experimental.pallas import tpu as pltpu

F32 = jnp.float32
BF16 = jnp.bfloat16

EPS = 1e-6
HEAD_DIM = 128
X_HEADS = 4
X_W = X_HEADS * HEAD_DIM
DELTA_HEADS = 12
DELTA_W = DELTA_HEADS * HEAD_DIM
CONV_W = 4
DELTA_CHUNK = 64
GMLP_GROUPS = 12
GMLP_W = GMLP_GROUPS * HEAD_DIM
GMLP_CHUNK = 128
N_MIXERS = 2

LANES = 128
SUBLANES = 8
VMEM_LIMIT = 56 * 1024 * 1024
FFN_VMEM_LIMIT = 62 * 1024 * 1024

HEAD_GROUP = 6
N_HEAD_GROUPS = DELTA_HEADS // HEAD_GROUP
FFN_ROWS = 1024
FFN_COLS = 512
PROJ_ROWS = 1024
PROJ_COLS = 1792
PROJ_COLS_A = 1024
AB_W = 512
TAIL_W = X_W + AB_W


def _params(*sem, vmem=VMEM_LIMIT):
    return pltpu.CompilerParams(dimension_semantics=sem, vmem_limit_bytes=vmem)


def _rms(x, gain):
    ms = jnp.mean(x * x, axis=-1, keepdims=True)
    return x * lax.rsqrt(ms + EPS) * gain


def _sigmoid(x):
    return 0.5 * jnp.tanh(0.5 * x) + 0.5


def _silu(x):
    return x * _sigmoid(x)


def _softplus(x):
    return jnp.maximum(x, 0.0) + jnp.log1p(jnp.exp(-jnp.abs(x)))


def _bdot(a, b):
    return jnp.dot(a.astype(BF16), b.astype(BF16), preferred_element_type=F32)


def _ffn_cast_kernel(x_ref, gpre_ref, gpost_ref, wg_ref, wu_ref, wd_ref, cgu_ref, cdn_ref,
                     o_ref, ogu_ref, odn_ref, hn_ref, act_ref):
    ogu_ref[...] = cgu_ref[...].astype(BF16)
    odn_ref[...] = cdn_ref[...].astype(BF16)
    _ffn_kernel(x_ref, gpre_ref, gpost_ref, wg_ref, wu_ref, wd_ref, o_ref, hn_ref, act_ref)


def _ffn_kernel(x_ref, gpre_ref, gpost_ref, wg_ref, wu_ref, wd_ref, o_ref, hn_ref, act_ref):
    f = pl.program_id(1)
    last = pl.num_programs(1) - 1

    def gate_up(slot):
        h = hn_ref[...]
        gate = jnp.dot(h, wg_ref[...], preferred_element_type=F32)
        up = jnp.dot(h, wu_ref[...], preferred_element_type=F32)
        act_ref[slot] = (_silu(gate) * up).astype(BF16)

    def down(slot):
        o_ref[...] += jnp.dot(act_ref[slot], wd_ref[...], preferred_element_type=F32)

    @pl.when(f == 0)
    def _():
        hn_ref[...] = _rms(x_ref[...], gpre_ref[...]).astype(BF16)
        o_ref[...] = jnp.zeros_like(o_ref)
        gate_up(0)

    @pl.when((f > 0) & (f < last))
    def _():
        down((f - 1) % 2)
        gate_up(f % 2)

    @pl.when(f == last)
    def _():
        down((f - 1) % 2)
        o_ref[...] = x_ref[...] + 0.5 * _rms(o_ref[...], gpost_ref[...])


def _ffn(x, gains, gain_base, pre, post, w_gu, w_dn, cast_next=None):
    m, d = x.shape
    ffn = w_dn.shape[0]
    tm = min(FFN_ROWS, m)
    tf = FFN_COLS
    nf = ffn // tf
    nm = m // tm
    assert m % tm == 0 and ffn % tf == 0
    def up_blk(f):
        return jnp.minimum(f, nf - 1)

    def dn_blk(f):
        return jnp.maximum(f - 1, 0)

    in_specs = [
        pl.BlockSpec((tm, d), lambda i, f: (i, 0)),
        pl.BlockSpec((None, 1, d), lambda i, f: (gain_base + pre, 0, 0)),
        pl.BlockSpec((None, 1, d), lambda i, f: (gain_base + post, 0, 0)),
        pl.BlockSpec((d, tf), lambda i, f: (0, up_blk(f))),
        pl.BlockSpec((d, tf), lambda i, f: (0, nf + up_blk(f))),
        pl.BlockSpec((tf, d), lambda i, f: (dn_blk(f), 0)),
    ]
    out_specs = pl.BlockSpec((tm, d), lambda i, f: (i, 0))
    out_shape = jax.ShapeDtypeStruct((m, d), F32)
    scratch = [pltpu.VMEM((tm, d), BF16), pltpu.VMEM((2, tm, tf), BF16)]
    if cast_next is None:
        return pl.pallas_call(
            _ffn_kernel, grid=(nm, nf + 1), in_specs=in_specs, out_specs=out_specs, out_shape=out_shape,
            scratch_shapes=scratch, compiler_params=_params("parallel", "arbitrary", vmem=FFN_VMEM_LIMIT),
        )(x, gains, gains, w_gu, w_gu, w_dn)
    gu_all, dn_all, layer, slot = cast_next
    gu_rows, gu_cols = d // nm, 2 * ffn // nf
    dn_rows = ffn // (nm * nf)
    assert gu_rows * nm == d and gu_cols * nf == 2 * ffn and dn_rows * nm * nf == ffn
    assert gu_rows % 16 == 0 and gu_cols % LANES == 0 and dn_rows % 16 == 0
    return pl.pallas_call(
        _ffn_cast_kernel,
        grid=(nm, nf + 1),
        in_specs=in_specs + [
            pl.BlockSpec((None, None, gu_rows, gu_cols), lambda i, f: (layer, slot, i, up_blk(f))),
            pl.BlockSpec((None, None, dn_rows, d), lambda i, f: (layer, slot, i * nf + up_blk(f), 0)),
        ],
        out_specs=[
            out_specs,
            pl.BlockSpec((gu_rows, gu_cols), lambda i, f: (i, up_blk(f))),
            pl.BlockSpec((dn_rows, d), lambda i, f: (i * nf + up_blk(f), 0)),
        ],
        out_shape=[out_shape, jax.ShapeDtypeStruct((d, 2 * ffn), BF16), jax.ShapeDtypeStruct((ffn, d), BF16)],
        scratch_shapes=scratch,
        compiler_params=_params("parallel", "arbitrary", vmem=FFN_VMEM_LIMIT),
    )(x, gains, gains, w_gu, w_gu, w_dn, gu_all, dn_all)


def _norm_matmul_kernel(x_ref, g_ref, w_ref, o_ref, hn_ref):
    @pl.when(pl.program_id(1) == 0)
    def _():
        hn_ref[...] = _rms(x_ref[...], g_ref[...]).astype(BF16)

    o_ref[...] = jnp.dot(hn_ref[...], w_ref[...], preferred_element_type=F32)


def _norm_matmul(x, gains, gain_idx, w, tn):
    m, d = x.shape
    n = w.shape[1]
    tm = min(PROJ_ROWS, m)
    assert m % tm == 0 and n % tn == 0
    return pl.pallas_call(
        _norm_matmul_kernel,
        grid=(m // tm, n // tn),
        in_specs=[
            pl.BlockSpec((tm, d), lambda i, j: (i, 0)),
            pl.BlockSpec((None, 1, d), lambda i, j: (gain_idx, 0, 0)),
            pl.BlockSpec((d, tn), lambda i, j: (0, j)),
        ],
        out_specs=pl.BlockSpec((tm, tn), lambda i, j: (i, j)),
        out_shape=jax.ShapeDtypeStruct((m, n), F32),
        scratch_shapes=[pltpu.VMEM((tm, d), BF16)],
        compiler_params=_params("parallel", "arbitrary"),
    )(x, gains, w)


def _norm_matmul_tail_kernel(x_ref, g_ref, w_ref, wt_ref, o_ref, ot_ref, hn_ref):
    j = pl.program_id(1)
    n_main = pl.num_programs(1) - 1

    @pl.when(j == 0)
    def _():
        hn_ref[...] = _rms(x_ref[...], g_ref[...]).astype(BF16)

    @pl.when(j < n_main)
    def _():
        o_ref[...] = jnp.dot(hn_ref[...], w_ref[...], preferred_element_type=F32)

    @pl.when(j == n_main)
    def _():
        ot_ref[...] = jnp.dot(hn_ref[...], wt_ref[...], preferred_element_type=F32)


def _norm_matmul_tail(x, gains, gain_idx, w, w_tail, tn):
    m, d = x.shape
    n, nt = w.shape[1], w_tail.shape[1]
    tm = min(PROJ_ROWS, m)
    n_main = n // tn
    assert m % tm == 0 and n % tn == 0

    def blk(j):
        return jnp.minimum(j, n_main - 1)

    return pl.pallas_call(
        _norm_matmul_tail_kernel,
        grid=(m // tm, n_main + 1),
        in_specs=[
            pl.BlockSpec((tm, d), lambda i, j: (i, 0)),
            pl.BlockSpec((None, 1, d), lambda i, j: (gain_idx, 0, 0)),
            pl.BlockSpec((d, tn), lambda i, j: (0, blk(j))),
            pl.BlockSpec((d, nt), lambda i, j: (0, 0)),
        ],
        out_specs=[
            pl.BlockSpec((tm, tn), lambda i, j: (i, blk(j))),
            pl.BlockSpec((tm, nt), lambda i, j: (i, 0)),
        ],
        out_shape=[jax.ShapeDtypeStruct((m, n), F32), jax.ShapeDtypeStruct((m, nt), F32)],
        scratch_shapes=[pltpu.VMEM((tm, d), BF16)],
        compiler_params=_params("parallel", "arbitrary"),
    )(x, gains, w, w_tail)


def _out_proj_kernel(x_ref, mix_ref, mem_ref, wmix_ref, wmem_ref, g_ref, o_ref):
    y = _bdot(mix_ref[...], wmix_ref[...]) + _bdot(mem_ref[...], wmem_ref[...])
    o_ref[...] = x_ref[...] + _rms(y, g_ref[...])


def _out_proj(x, mix, mem, w_out, gains, layer):
    m, d = x.shape
    wm = mix.shape[1]
    wx = mem.shape[1]
    tm = min(512, m)
    assert wm % wx == 0
    return pl.pallas_call(
        _out_proj_kernel,
        grid=(m // tm,),
        in_specs=[
            pl.BlockSpec((tm, d), lambda i: (i, 0)),
            pl.BlockSpec((tm, wm), lambda i: (i, 0)),
            pl.BlockSpec((tm, wx), lambda i: (i, 0)),
            pl.BlockSpec((None, wm, d), lambda i: (layer, 0, 0)),
            pl.BlockSpec((None, wx, d), lambda i: (layer, wm // wx, 0)),
            pl.BlockSpec((None, 1, d), lambda i: (layer * 6 + 3, 0, 0)),
        ],
        out_specs=pl.BlockSpec((tm, d), lambda i: (i, 0)),
        out_shape=jax.ShapeDtypeStruct((m, d), F32),
        compiler_params=_params("parallel"),
    )(x, mix, mem, w_out, w_out, gains)


def _unit_lower_inverse(lmat, c, span=None):
    span = c if span is None else span
    base = 4
    ri = lax.broadcasted_iota(jnp.int32, (c, c), 0)
    ci = lax.broadcasted_iota(jnp.int32, (c, c), 1)
    eye = (ri == ci).astype(F32)
    l_hi, l_lo = _split(lmat)

    def masked(keep):
        return jnp.where(keep, l_hi, jnp.zeros_like(l_hi)), jnp.where(keep, l_lo, jnp.zeros_like(l_lo))

    same = (ri // base) == (ci // base)
    dm = masked(same)
    x = eye - jnp.where(same, lmat, 0.0)
    x = x + _mm3(_split(x), _split(_mm3(dm, dm)))
    size = base
    while size < span:
        inner = ((ri // (2 * size)) == (ci // (2 * size))) & ((ri // size) != (ci // size))
        xs = _split(x)
        x = x - _mm3(_split(_mm3(xs, masked(inner))), xs)
        size *= 2
    return x


def _split(x):
    hi = x.astype(BF16)
    return hi, (x - hi.astype(F32)).astype(BF16)


def _mm3(a, b):
    def mm(p, q):
        return jnp.einsum("nij,njk->nik", p, q, preferred_element_type=F32)

    width = b[0].shape[-1]
    if a[0].shape[-1] % LANES == 0 and width % LANES == 0:
        lhs = jnp.concatenate([a[0], a[1]], axis=-1)
        rhs = jnp.concatenate([jnp.concatenate([b[0], b[1]], axis=-1),
                               jnp.concatenate([b[0], jnp.zeros_like(b[0])], axis=-1)], axis=-2)
        r = mm(lhs, rhs)
        return r[..., :width] + r[..., width:]
    return mm(a[0], b[0]) + (mm(a[0], b[1]) + mm(a[1], b[0]))


def _delta_pair_terms(q, k, v, gcol, bcol):
    n, cp, d = q.shape
    c = DELTA_CHUNK
    wu, a_intra, egc = _wy_solve(q, k, v, gcol, bcol, c)
    wu = wu.astype(BF16)
    awu = jnp.einsum("nij,nje->nie", a_intra, wu, preferred_element_type=F32)
    qt = (q * egc - awu[..., :d]).astype(BF16)
    o0 = awu[..., d:]
    k64 = k.reshape(2 * n, c, d)
    g64 = gcol.reshape(2 * n, c, 1)
    glast = g64[:, c - 1:c, :]
    k_dec = (k64 * jnp.exp(glast - g64)).astype(BF16)
    mb = jnp.einsum("ncd,nce->nde", k_dec, wu.reshape(2 * n, c, 2 * d), preferred_element_type=F32)
    return (qt.reshape(2 * n, c, d), mb[..., :d].astype(BF16), mb[..., d:], o0.reshape(2 * n, c, d),
            jnp.exp(glast))


def _wy_solve(q, k, v, gcol, bcol, span):
    rows = q.shape[1]
    ri = lax.broadcasted_iota(jnp.int32, (rows, rows), 0)
    ci = lax.broadcasted_iota(jnp.int32, (rows, rows), 1)
    eye = (ri == ci).astype(F32)
    same = (ri // span) == (ci // span)
    strict = same & (ci < ri)
    causal = same & (ci <= ri)
    grow = jnp.sum(gcol * eye, axis=1, keepdims=True)
    diff = gcol - grow
    kb = k.astype(BF16)
    kk = jnp.einsum("ncd,nmd->ncm", kb, kb, preferred_element_type=F32)
    lmat = jnp.where(strict, bcol * kk * jnp.exp(jnp.where(strict, diff, 0.0)), 0.0)
    tinv = _unit_lower_inverse(lmat, rows, span)
    egc = jnp.exp(gcol)
    rhs = jnp.concatenate([k * (bcol * egc), v * bcol], axis=-1)
    wu = _mm3(_split(tinv), _split(rhs))
    qk = jnp.einsum("ncd,nmd->ncm", q.astype(BF16), kb, preferred_element_type=F32)
    a_intra = jnp.where(causal, qk * jnp.exp(jnp.where(causal, diff, 0.0)), 0.0).astype(BF16)
    return wu, a_intra, egc


def _l2norm(x):
    return x * lax.rsqrt(jnp.sum(x * x, axis=-1, keepdims=True) + EPS)


def _block_cumsum(g, c):
    rows, lanes = g.shape
    n = rows // c
    ri = lax.broadcasted_iota(jnp.int32, (n, c, c), 1)
    ci = lax.broadcasted_iota(jnp.int32, (n, c, c), 2)
    tri = (ci <= ri).astype(BF16)
    g3 = g.reshape(n, c, lanes)
    hi = g3.astype(BF16)
    r1 = g3 - hi.astype(F32)
    mid = r1.astype(BF16)
    lo = (r1 - mid.astype(F32)).astype(BF16)

    def mm(p):
        return jnp.einsum("nij,njk->nik", tri, p, preferred_element_type=F32)

    return (mm(hi) + (mm(mid) + mm(lo))).reshape(rows, lanes)


def _shifted(x, prev, shift, row):
    xs = pltpu.roll(x, shift, x.ndim - 2)
    if x.ndim == 2:
        ps = pltpu.roll(prev, shift, 0)
        reps = x.shape[0] // SUBLANES
        ps = jnp.broadcast_to(ps[None], (reps,) + ps.shape).reshape(x.shape)
        return jnp.where(row < shift, ps, xs)
    for t in range(shift):
        src = prev.shape[1] - shift + t
        xs = jnp.where(row == t, prev[:, src:src + 1, :], xs)
    return xs


def _short_conv_silu(x, prev, w, row):
    y = x * w[CONV_W - 1:CONV_W]
    for shift in range(1, CONV_W):
        tap = CONV_W - 1 - shift
        y = y + _shifted(x, prev, shift, row) * w[tap:tap + 1]
    return _silu(y)


def _delta_prompt_kernel(q_ref, k_ref, v_ref, z_ref, ab_ref, wq_ref, wk_ref, wv_ref, alog_ref, dtb_ref,
                         og_ref, o_ref, sout_ref, s_ref, pq_ref, pk_ref, pv_ref, bq_ref, bk_ref, bv_ref, *, tt):
    t = pl.program_id(2)
    c = DELTA_CHUNK
    n = tt // c
    d = HEAD_DIM

    @pl.when(t == 0)
    def _():
        s_ref[...] = jnp.zeros_like(s_ref)
        for ref in (pq_ref, pk_ref, pv_ref, bq_ref, bk_ref, bv_ref):
            ref[...] = jnp.zeros_like(ref)

    row = lax.broadcasted_iota(jnp.int32, (tt, 1), 0)

    def conv(x_ref, p_ref, b_ref, w_ref):
        x = x_ref[...]
        w = w_ref[...]
        x1 = _shifted(x, p_ref[...], 1, row)
        b = x * w[1:2] + x1 * w[0:1]
        y = x * w[3:4] + x1 * w[2:3] + _shifted(b, b_ref[...], 2, row)
        p_ref[...] = x[tt - SUBLANES:, :]
        b_ref[...] = b[tt - SUBLANES:, :]
        return _silu(y)

    assert CONV_W == 4
    qc = conv(q_ref, pq_ref, bq_ref, wq_ref)
    kc = conv(k_ref, pk_ref, bk_ref, wk_ref)
    vc = conv(v_ref, pv_ref, bv_ref, wv_ref)

    ab = ab_ref[...]
    g_all = -jnp.exp(alog_ref[...]) * _softplus(ab + dtb_ref[...])
    beta_all = _sigmoid(ab)
    gc_all = _block_cumsum(g_all, c)

    heads = range(HEAD_GROUP)

    def pairs(cols):
        return jnp.concatenate([x.reshape(n // 2, 2 * c, x.shape[-1]) for x in cols], axis=0)

    q = pairs([_l2norm(qc[:, hh * d:(hh + 1) * d]) * (d ** -0.5) for hh in heads])
    k = pairs([_l2norm(kc[:, hh * d:(hh + 1) * d]) for hh in heads])
    v = pairs([vc[:, hh * d:(hh + 1) * d] for hh in heads])
    gcol = pairs([gc_all[:, hh:hh + 1] for hh in heads])
    bcol = pairs([beta_all[:, HEAD_GROUP + hh:HEAD_GROUP + hh + 1] for hh in heads])
    qt, m, bmat, o0, g_last = _delta_pair_terms(q, k, v, gcol, bcol)

    gain = og_ref[...]
    states = [s_ref[hh] for hh in heads]
    for ch in range(n):
        for hh in heads:
            i = hh * n + ch
            s = states[hh]
            r = jnp.dot(jnp.concatenate([qt[i], m[i]], axis=0), s.astype(BF16),
                        preferred_element_type=F32)
            o = r[:c] + o0[i]
            states[hh] = s * g_last[i] + (bmat[i] - r[c:])
            zz = z_ref[ch * c:(ch + 1) * c, hh * d:(hh + 1) * d]
            o_ref[ch * c:(ch + 1) * c, hh * d:(hh + 1) * d] = _rms(o, gain) * _silu(zz)
    for hh in range(HEAD_GROUP):
        s_ref[hh] = states[hh]
        sout_ref[hh] = states[hh]


def _delta_prompt(proj, tail, batch, seq, conv_w, alog_rows, dtb_rows, out_gain):
    tt = min(512, seq)
    nt = seq // tt
    gw = HEAD_GROUP * HEAD_DIM
    ng = N_HEAD_GROUPS
    assert seq % tt == 0 and tt % DELTA_CHUNK == 0

    def col(base):
        return pl.BlockSpec((tt, gw), lambda b, g, t: (b * nt + t, base * ng + g))

    def wcol(base):
        return pl.BlockSpec((CONV_W, gw), lambda b, g, t: (0, base * ng + g))

    ab_base = X_W // LANES
    return pl.pallas_call(
        functools.partial(_delta_prompt_kernel, tt=tt),
        grid=(batch, ng, nt),
        in_specs=[
            col(0), col(1), col(2), col(3),
            pl.BlockSpec((tt, LANES), lambda b, g, t: (b * nt + t, ab_base + g)),
            wcol(0), wcol(1), wcol(2),
            pl.BlockSpec((1, LANES), lambda b, g, t: (0, g)),
            pl.BlockSpec((1, LANES), lambda b, g, t: (0, g)),
            pl.BlockSpec((1, HEAD_DIM), lambda b, g, t: (0, 0)),
        ],
        out_specs=[
            pl.BlockSpec((tt, gw), lambda b, g, t: (b * nt + t, g)),
            pl.BlockSpec((None, HEAD_GROUP, HEAD_DIM, HEAD_DIM), lambda b, g, t: (b, g, 0, 0)),
        ],
        out_shape=[
            jax.ShapeDtypeStruct((batch * seq, DELTA_W), F32),
            jax.ShapeDtypeStruct((batch, DELTA_HEADS, HEAD_DIM, HEAD_DIM), F32),
        ],
        scratch_shapes=[
            pltpu.VMEM((HEAD_GROUP, HEAD_DIM, HEAD_DIM), F32),
        ] + [pltpu.VMEM((SUBLANES, gw), F32)] * 6,
        compiler_params=_params("parallel", "parallel", "arbitrary"),
    )(proj, proj, proj, proj, tail, conv_w, conv_w, conv_w, alog_rows, dtb_rows, out_gain)


def _delta_sample_kernel(q_ref, k_ref, v_ref, z_ref, ab_ref, bq_ref, bk_ref, bv_ref, s0_ref, wq_ref, wk_ref,
                         wv_ref, alog_ref, dtb_ref, og_ref, o_ref, sout_ref, *, bb, seq):
    d = HEAD_DIM
    rows = bb * seq
    row = lax.broadcasted_iota(jnp.int32, (1, seq, 1), 1)

    def conv(x_ref, buf_ref, w_ref):
        x = x_ref[...].reshape(bb, seq, x_ref.shape[1])
        return _short_conv_silu(x, buf_ref[...], w_ref[...], row).reshape(rows, x_ref.shape[1])

    qc = conv(q_ref, bq_ref, wq_ref)
    kc = conv(k_ref, bk_ref, wk_ref)
    vc = conv(v_ref, bv_ref, wv_ref)

    ab = ab_ref[...]
    g_all = -jnp.exp(alog_ref[...]) * _softplus(ab + dtb_ref[...])
    beta_all = _sigmoid(ab)
    gc_all = _block_cumsum(g_all, seq)

    heads = range(HEAD_GROUP)
    q = jnp.stack([_l2norm(qc[:, hh * d:(hh + 1) * d]) * (d ** -0.5) for hh in heads])
    k = jnp.stack([_l2norm(kc[:, hh * d:(hh + 1) * d]) for hh in heads])
    v = jnp.stack([vc[:, hh * d:(hh + 1) * d] for hh in heads])
    gcol = jnp.stack([gc_all[:, hh:hh + 1] for hh in heads])
    bcol = jnp.stack([beta_all[:, HEAD_GROUP + hh:HEAD_GROUP + hh + 1] for hh in heads])
    wu, a_intra, egc = _wy_solve(q, k, v, gcol, bcol, seq)
    q_dec = (q * egc).astype(BF16)
    g4 = gcol.reshape(HEAD_GROUP, bb, seq, 1)
    glast = g4[:, :, seq - 1:seq, :]
    k_dec = (k.reshape(HEAD_GROUP, bb, seq, d) * jnp.exp(glast - g4)).astype(BF16)
    g_last = jnp.exp(glast)

    gain = og_ref[...]
    for hh in heads:
        s = s0_ref[:, hh]
        w = wu[hh, :, :d].astype(BF16).reshape(bb, seq, d)
        u = wu[hh, :, d:].reshape(bb, seq, d)
        wq = jnp.concatenate([w, q_dec[hh].reshape(bb, seq, d)], axis=1)
        r = jnp.einsum("bcd,bde->bce", wq, s.astype(BF16), preferred_element_type=F32)
        v_new = u - r[:, :seq]
        vb = v_new.astype(BF16)
        o = r[:, seq:].reshape(rows, d) + jnp.dot(a_intra[hh], vb.reshape(rows, d), preferred_element_type=F32)
        sout_ref[:, hh] = s * g_last[hh] + jnp.einsum("bcd,bce->bde", k_dec[hh], vb,
                                                      preferred_element_type=F32)
        o_ref[:, hh * d:(hh + 1) * d] = _rms(o, gain) * _silu(z_ref[:, hh * d:(hh + 1) * d])


def _delta_sample(proj, tail, batch, seq, conv_bufs, states, j, conv_w, alog_rows, dtb_rows, out_gain):
    bb = LANES // seq
    assert batch % bb == 0 and seq == SUBLANES
    rows = bb * seq
    gw = HEAD_GROUP * HEAD_DIM
    ng = N_HEAD_GROUPS

    def col(base):
        return pl.BlockSpec((rows, gw), lambda i, g: (i, base * ng + g))

    def bufcol(base):
        return pl.BlockSpec((None, bb, CONV_W - 1, gw), lambda i, g: (j, i, 0, base * ng + g))

    def wcol(base):
        return pl.BlockSpec((CONV_W, gw), lambda i, g: (0, base * ng + g))

    ab_base = X_W // LANES
    state_spec =pl.BlockSpec((None, bb, HEAD_GROUP, HEAD_DIM, HEAD_DIM), lambda i, g: (j, i, g, 0, 0))
    return pl.pallas_call(
        functools.partial(_delta_sample_kernel, bb=bb, seq=seq),
        grid=(batch // bb, ng),
        in_specs=[
            col(0), col(1), col(2), col(3),
            pl.BlockSpec((rows, LANES), lambda i, g: (i, ab_base + g)),
            bufcol(0), bufcol(1), bufcol(2),
            state_spec,
            wcol(0), wcol(1), wcol(2),
            pl.BlockSpec((1, LANES), lambda i, g: (0, g)),
            pl.BlockSpec((1, LANES), lambda i, g: (0, g)),
            pl.BlockSpec((1, HEAD_DIM), lambda i, g: (0, 0)),
        ],
        out_specs=[
            pl.BlockSpec((rows, gw), lambda i, g: (i, g)),
            pl.BlockSpec((bb, HEAD_GROUP, HEAD_DIM, HEAD_DIM), lambda i, g: (i, g, 0, 0)),
        ],
        out_shape=[
            jax.ShapeDtypeStruct((batch * seq, DELTA_W), F32),
            jax.ShapeDtypeStruct((batch, DELTA_HEADS, HEAD_DIM, HEAD_DIM), F32),
        ],
        compiler_params=_params("parallel", "parallel"),
    )(proj, proj, proj, proj, tail, conv_bufs, conv_bufs, conv_bufs, states, conv_w, conv_w, conv_w,
      alog_rows, dtb_rows, out_gain)


def _gelu(x):
    return 0.5 * x * (1.0 + lax.erf(x * (2.0 ** -0.5)))


def _layernorm(x, gain, bias):
    mu = jnp.mean(x, axis=-1, keepdims=True)
    xc = x - mu
    var = jnp.mean(xc * xc, axis=-1, keepdims=True)
    return xc * lax.rsqrt(var + EPS) * gain + bias


def _gmlp_prompt_kernel(u_ref, v_ref, lng_ref, lnb_ref, ws_ref, bs_ref, o_ref, *, chunks):
    c = GMLP_CHUNK
    d = HEAD_DIM
    ri = lax.broadcasted_iota(jnp.int32, (c, c), 0)
    ci = lax.broadcasted_iota(jnp.int32, (c, c), 1)
    lower = ci <= ri
    v = _layernorm(_gelu(v_ref[...]), lng_ref[...], lnb_ref[...]).astype(BF16)
    bs = bs_ref[...]
    for g in range(GMLP_GROUPS):
        w = jnp.where(lower, ws_ref[g], 0.0).astype(BF16)
        bias = bs[:, g:g + 1]
        for ch in range(chunks):
            rs = slice(ch * c, (ch + 1) * c)
            cs = slice(g * d, (g + 1) * d)
            mixed = jnp.dot(w, v[rs, cs], preferred_element_type=F32) + bias
            o_ref[rs, cs] = _gelu(u_ref[rs, cs]) * mixed


def _gmlp_prompt(proj, ln_gain, ln_bias, w_s, b_s_t):
    m = proj.shape[0]
    chunks = 2
    tm = chunks * GMLP_CHUNK
    assert m % tm == 0
    return pl.pallas_call(
        functools.partial(_gmlp_prompt_kernel, chunks=chunks),
        grid=(m // tm,),
        in_specs=[
            pl.BlockSpec((tm, GMLP_W), lambda i: (i, 0)),
            pl.BlockSpec((tm, GMLP_W), lambda i: (i, 1)),
            pl.BlockSpec((1, GMLP_W), lambda i: (0, 0)),
            pl.BlockSpec((1, GMLP_W), lambda i: (0, 0)),
            pl.BlockSpec((GMLP_GROUPS, GMLP_CHUNK, GMLP_CHUNK), lambda i: (0, 0, 0)),
            pl.BlockSpec((GMLP_CHUNK, GMLP_GROUPS), lambda i: (0, 0)),
        ],
        out_specs=pl.BlockSpec((tm, GMLP_W), lambda i: (i, 0)),
        out_shape=jax.ShapeDtypeStruct((m, GMLP_W), F32),
        compiler_params=_params("parallel"),
    )(proj, proj, ln_gain, ln_bias, w_s, b_s_t)


def _gmlp_sample_kernel(u_ref, v_ref, lng_ref, lnb_ref, wx_ref, bx_ref, o_ref, vout_ref, *, bb, seq):
    v = _layernorm(_gelu(v_ref[...]), lng_ref[...], lnb_ref[...])
    vout_ref[...] = v
    v3 = v.reshape(bb, seq, GMLP_W)
    row = lax.broadcasted_iota(jnp.int32, (seq, 1), 0)
    mixed = jnp.broadcast_to(bx_ref[...][None], (bb, seq, GMLP_W))
    for j in range(seq):
        wj = jnp.where(row >= j, wx_ref[j], 0.0)
        mixed = mixed + wj[None] * v3[:, j:j + 1, :]
    o_ref[...] = _gelu(u_ref[...]) * mixed.reshape(bb * seq, GMLP_W)


def _gmlp_sample(proj, batch, seq, ln_gain, ln_bias, w_exp, b_exp):
    bb = 32
    assert batch % bb == 0 and seq == SUBLANES
    rows = bb * seq
    return pl.pallas_call(
        functools.partial(_gmlp_sample_kernel, bb=bb, seq=seq),
        grid=(batch // bb,),
        in_specs=[
            pl.BlockSpec((rows, GMLP_W), lambda i: (i, 0)),
            pl.BlockSpec((rows, GMLP_W), lambda i: (i, 1)),
            pl.BlockSpec((1, GMLP_W), lambda i: (0, 0)),
            pl.BlockSpec((1, GMLP_W), lambda i: (0, 0)),
            pl.BlockSpec((seq, seq, GMLP_W), lambda i: (0, 0, 0)),
            pl.BlockSpec((seq, GMLP_W), lambda i: (0, 0)),
        ],
        out_specs=[
            pl.BlockSpec((rows, GMLP_W), lambda i: (i, 0)),
            pl.BlockSpec((rows, GMLP_W), lambda i: (i, 0)),
        ],
        out_shape=[
            jax.ShapeDtypeStruct((batch * seq, GMLP_W), F32),
            jax.ShapeDtypeStruct((batch * seq, GMLP_W), F32),
        ],
        compiler_params=_params("parallel"),
    )(proj, proj, ln_gain, ln_bias, w_exp, b_exp)


def _softmax(s):
    m = jnp.max(s, axis=-1, keepdims=True)
    e = jnp.exp(s - m)
    return e / jnp.sum(e, axis=-1, keepdims=True)


def _mem_attn_prompt_kernel(q_ref, k_ref, v_ref, o_ref):
    d = HEAD_DIM
    for h in range(X_HEADS):
        sl = slice(h * d, (h + 1) * d)
        s = lax.dot_general(q_ref[:, sl].astype(BF16), k_ref[:, sl].astype(BF16),
                            (((1,), (1,)), ((), ())), preferred_element_type=F32) * (d ** -0.5)
        o_ref[:, sl] = _bdot(_softmax(s), v_ref[:, sl])


def _mem_attn_prompt(proj, q_block, batch, seq, mem_k, mem_v):
    tt = min(512, seq)
    nt = seq // tt
    n_mem = mem_k.shape[1]
    return pl.pallas_call(
        _mem_attn_prompt_kernel,
        grid=(batch, nt),
        in_specs=[
            pl.BlockSpec((tt, X_W), lambda b, t: (b * nt + t, q_block)),
            pl.BlockSpec((None, n_mem, X_W), lambda b, t: (b, 0, 0)),
            pl.BlockSpec((None, n_mem, X_W), lambda b, t: (b, 0, 0)),
        ],
        out_specs=pl.BlockSpec((tt, X_W), lambda b, t: (b * nt + t, 0)),
        out_shape=jax.ShapeDtypeStruct((batch * seq, X_W), F32),
        compiler_params=_params("parallel", "parallel"),
    )(proj, mem_k, mem_v)


def _mem_attn_sample_kernel(q_ref, k_ref, v_ref, o_ref, *, bb, seq):
    d = HEAD_DIM
    nq = X_HEADS * seq
    nk = k_ref.shape[1]
    q = jnp.concatenate([q_ref[:, h * d:(h + 1) * d].reshape(bb, seq, d) for h in range(X_HEADS)], axis=1)
    s = jnp.einsum("bqd,bkd->bqk", q.astype(BF16), k_ref[...].astype(BF16),
                   preferred_element_type=F32) * (d ** -0.5)
    q_head = lax.broadcasted_iota(jnp.int32, (nq, nk), 0) // seq
    k_head = lax.broadcasted_iota(jnp.int32, (nq, nk), 1) % X_HEADS
    p = _softmax(jnp.where(q_head == k_head, s, -1e30)).astype(BF16)
    o = jnp.einsum("bqk,bkd->bqd", p, v_ref[...].astype(BF16), preferred_element_type=F32)
    for h in range(X_HEADS):
        o_ref[:, h * d:(h + 1) * d] = o[:, h * seq:(h + 1) * seq, :].reshape(bb * seq, d)


def _mem_attn_sample(proj, q_block, batch, seq, mem_k, mem_v, layer):
    bb = 8
    rows = bb * seq
    n_rows = mem_k.shape[2]
    return pl.pallas_call(
        functools.partial(_mem_attn_sample_kernel, bb=bb, seq=seq),
        grid=(batch // bb,),
        in_specs=[
            pl.BlockSpec((rows, X_W), lambda i: (i, q_block)),
            pl.BlockSpec((None, bb, n_rows, HEAD_DIM), lambda i: (layer, i, 0, 0)),
            pl.BlockSpec((None, bb, n_rows, HEAD_DIM), lambda i: (layer, i, 0, 0)),
        ],
        out_specs=pl.BlockSpec((rows, X_W), lambda i: (i, 0)),
        out_shape=jax.ShapeDtypeStruct((batch * seq, X_W), F32),
        compiler_params=_params("parallel"),
    )(proj, mem_k, mem_v)


def _delta_in_weight(w_in_a):
    d_model = w_in_a.shape[0]
    a = w_in_a[:, 4 * DELTA_W:4 * DELTA_W + DELTA_HEADS]
    b = w_in_a[:, 4 * DELTA_W + DELTA_HEADS:4 * DELTA_W + 2 * DELTA_HEADS]
    parts = [w_in_a[:, 4 * DELTA_W + 2 * DELTA_HEADS:]]
    pad = jnp.zeros((d_model, LANES - 2 * HEAD_GROUP), w_in_a.dtype)
    for g in range(N_HEAD_GROUPS):
        hs = slice(g * HEAD_GROUP, (g + 1) * HEAD_GROUP)
        parts += [a[:, hs], b[:, hs], pad]
    parts.append(jnp.zeros((d_model, AB_W - N_HEAD_GROUPS * LANES), w_in_a.dtype))
    return w_in_a[:, :4 * DELTA_W].astype(BF16), jnp.concatenate(parts, axis=1).astype(BF16)


def _gate_rows(vec):
    row = jnp.zeros((N_HEAD_GROUPS, LANES), F32)
    row = row.at[:, :HEAD_GROUP].set(vec.astype(F32).reshape(N_HEAD_GROUPS, HEAD_GROUP))
    return jnp.concatenate([row.reshape(1, -1), jnp.zeros((1, AB_W - N_HEAD_GROUPS * LANES), F32)], axis=1)


def _ffn_step(x, p, layer, slot, pre, post, round_next):
    w_gu, w_dn = p["ffn_bf16"][(layer, slot)]
    nxt = (layer, 1) if slot == 0 else (layer + 1, 0)
    if round_next and nxt[0] < p["w_out"].shape[0]:
        x, gu_next, dn_next = _ffn(x, p["gains"], layer * 6, pre, post, w_gu, w_dn,
                                   cast_next=(p["w_ffn_gu"], p["w_ffn_dn"]) + nxt)
        p["ffn_bf16"][nxt] = (gu_next, dn_next)
        return x
    return _ffn(x, p["gains"], layer * 6, pre, post, w_gu, w_dn)


def _trunk(x, batch, seq, is_prompt, mem_k, mem_v, conv_bufs, delta_states, p):
    new_conv, new_delta, new_v = [], [], []
    gains = p["gains"]
    depth = p["w_out"].shape[0]
    for i in range(depth):
        j = i // N_MIXERS
        x = _ffn_step(x, p, i, 0, 0, 1, is_prompt)
        if i % N_MIXERS == 0:
            w_main, w_tail = p["w_in_a"][j]
            proj, q_src = _norm_matmul_tail(x, gains, i * 6 + 2, w_main, w_tail, PROJ_COLS_A)
            q_block = 0
            if is_prompt:
                mix, s_new = _delta_prompt(proj, q_src, batch, seq, p["conv_w"][j], p["alog_rows"][j],
                                           p["dtb_rows"][j], p["delta_norm_gain"][j])
            else:
                mix, s_new = _delta_sample(proj, q_src, batch, seq, conv_bufs, delta_states, j, p["conv_w"][j],
                                           p["alog_rows"][j], p["dtb_rows"][j], p["delta_norm_gain"][j])
            new_conv.append(proj.reshape(batch, seq, -1)[:, seq - (CONV_W - 1):, :3 * DELTA_W])
            new_delta.append(s_new)
        else:
            proj = q_src = _norm_matmul(x, gains, i * 6 + 2, p["w_in_b"][j], PROJ_COLS)
            q_block = 2 * GMLP_W // X_W
            if is_prompt:
                mix = _gmlp_prompt(proj, p["gmlp_ln_gain"][j], p["gmlp_ln_bias"][j], p["w_spatial"][j],
                                   p["b_spatial_t"][j])
            else:
                mix, v_rows = _gmlp_sample(proj, batch, seq, p["gmlp_ln_gain"][j], p["gmlp_ln_bias"][j],
                                           p["w_spatial_exp"][j], p["b_spatial_exp"][j])
                new_v.append(v_rows.reshape(batch, seq, GMLP_W))
        if is_prompt:
            mem_out = _mem_attn_prompt(q_src, q_block, batch, seq, mem_k[i], mem_v[i])
        else:
            mem_out = _mem_attn_sample(q_src, q_block, batch, seq, mem_k, mem_v, i)
        x = _out_proj(x, mix, mem_out, p["w_out"], gains, i)
        x = _ffn_step(x, p, i, 1, 4, 5, is_prompt)
    return x, new_conv, new_delta, new_v


def kernel(x_prompt, x_sample, mem_prompt, cache_mem_k, cache_mem_v, state_delta, state_conv, norm_gains, w_ffn_gu, w_ffn_dn, w_in_a, conv_w, a_log, dt_bias, delta_norm_gain, w_in_b, gmlp_ln_gain, gmlp_ln_bias, w_spatial, b_spatial, mem_norm_gain, w_mem_kv, w_out):
    batch, seq, d_model = x_prompt.shape
    dec_batch, dec_seq, _ = x_sample.shape
    depth = w_out.shape[0]
    n_mem = mem_prompt.shape[1]
    n_a = w_in_a.shape[0]
    n_b = w_in_b.shape[0]

    w_s_dec = w_spatial[:, :, :dec_seq, :dec_seq]
    p = {
        "gains": norm_gains.reshape(depth * 6, 1, d_model),
        "w_ffn_gu": w_ffn_gu,
        "w_ffn_dn": w_ffn_dn,
        "ffn_bf16": {(0, 0): (w_ffn_gu[0, 0].astype(BF16), w_ffn_dn[0, 0].astype(BF16))},
        "w_in_a": [_delta_in_weight(w_in_a[j]) for j in range(n_a)],
        "w_in_b": [w_in_b[j].astype(BF16) for j in range(n_b)],
        "w_out": w_out.astype(BF16),
        "conv_w": conv_w,
        "alog_rows": [_gate_rows(a_log[j]) for j in range(n_a)],
        "dtb_rows": [_gate_rows(dt_bias[j]) for j in range(n_a)],
        "delta_norm_gain": delta_norm_gain.reshape(n_a, 1, HEAD_DIM),
        "gmlp_ln_gain": gmlp_ln_gain.reshape(n_b, 1, GMLP_W),
        "gmlp_ln_bias": gmlp_ln_bias.reshape(n_b, 1, GMLP_W),
        "w_spatial": w_spatial,
        "b_spatial_t": jnp.swapaxes(b_spatial, 1, 2),
        "w_spatial_exp": jnp.repeat(jnp.transpose(w_s_dec, (0, 3, 2, 1)), HEAD_DIM, axis=-1),
        "b_spatial_exp": jnp.repeat(jnp.swapaxes(b_spatial[:, :, :dec_seq], 1, 2), HEAD_DIM, axis=-1),
    }

    mem2d = mem_prompt.reshape(batch * n_mem, d_model)
    mem_gains = mem_norm_gain.reshape(depth, 1, d_model)
    w_kv = w_mem_kv.astype(BF16)
    kvs = [_norm_matmul(mem2d, mem_gains, i, w_kv[i], 1024) for i in range(depth)]
    mem_k_prompt = jnp.stack([kv[:, :X_W].reshape(batch, n_mem, X_W) for kv in kvs])
    mem_v_prompt = jnp.stack([kv[:, X_W:].reshape(batch, n_mem, X_W) for kv in kvs])

    y_prompt, conv_p, delta_p, _ = _trunk(x_prompt.reshape(batch * seq, d_model), batch, seq, True,
                                          mem_k_prompt, mem_v_prompt, None, None, p)

    y_sample, conv_s, delta_s, v_s = _trunk(
        x_sample.reshape(dec_batch * dec_seq, d_model), dec_batch, dec_seq, False,
        cache_mem_k.reshape(depth, dec_batch, n_mem * X_HEADS, HEAD_DIM),
        cache_mem_v.reshape(depth, dec_batch, n_mem * X_HEADS, HEAD_DIM),
        state_conv, state_delta, p)

    return (y_prompt.reshape(batch, seq, d_model),
            y_sample.reshape(dec_batch, dec_seq, d_model),
            mem_k_prompt.reshape(depth, batch, n_mem, X_HEADS, HEAD_DIM),
            mem_v_prompt.reshape(depth, batch, n_mem, X_HEADS, HEAD_DIM),
            jnp.stack(delta_p), jnp.stack(conv_p), jnp.stack(delta_s), jnp.stack(conv_s), jnp.stack(v_s))
```

```python
import functools

import jax
import jax.numpy as jnp
from jax import lax
from jax.experimental import pallas as pl
from jax.experimental.pallas import tpu as pltpu

F32 = jnp.float32
BF16 = jnp.bfloat16

EPS = 1e-6
HEAD_DIM = 128
X_HEADS = 4
X_W = X_HEADS * HEAD_DIM
DELTA_HEADS = 12
DELTA_W = DELTA_HEADS * HEAD_DIM
CONV_W = 4
DELTA_CHUNK = 64
GMLP_GROUPS = 12
GMLP_W = GMLP_GROUPS * HEAD_DIM
GMLP_CHUNK = 128
N_MIXERS = 2

LANES = 128
SUBLANES = 8
VMEM_LIMIT = 56 * 1024 * 1024
FFN_VMEM_LIMIT = 62 * 1024 * 1024

HEAD_GROUP = 6
N_HEAD_GROUPS = DELTA_HEADS // HEAD_GROUP
FFN_ROWS = 1024
FFN_COLS = 512
PROJ_ROWS = 1024
PROJ_COLS = 1792
PROJ_COLS_A = 1024
AB_W = 512
TAIL_W = X_W + AB_W


def _params(*sem, vmem=VMEM_LIMIT):
    return pltpu.CompilerParams(dimension_semantics=sem, vmem_limit_bytes=vmem)


def _rms(x, gain):
    ms = jnp.mean(x * x, axis=-1, keepdims=True)
    return x * lax.rsqrt(ms + EPS) * gain


def _sigmoid(x):
    return 0.5 * jnp.tanh(0.5 * x) + 0.5


def _silu(x):
    return x * _sigmoid(x)


def _softplus(x):
    return jnp.maximum(x, 0.0) + jnp.log1p(jnp.exp(-jnp.abs(x)))


def _bdot(a, b):
    return jnp.dot(a.astype(BF16), b.astype(BF16), preferred_element_type=F32)


def _ffn_cast_kernel(x_ref, gpre_ref, gpost_ref, wg_ref, wu_ref, wd_ref, cgu_ref, cdn_ref,
                     o_ref, ogu_ref, odn_ref, hn_ref, act_ref):
    ogu_ref[...] = cgu_ref[...].astype(BF16)
    odn_ref[...] = cdn_ref[...].astype(BF16)
    _ffn_kernel(x_ref, gpre_ref, gpost_ref, wg_ref, wu_ref, wd_ref, o_ref, hn_ref, act_ref)


def _ffn_kernel(x_ref, gpre_ref, gpost_ref, wg_ref, wu_ref, wd_ref, o_ref, hn_ref, act_ref):
    f = pl.program_id(1)
    last = pl.num_programs(1) - 1

    def gate_up(slot):
        h = hn_ref[...]
        gate = jnp.dot(h, wg_ref[...], preferred_element_type=F32)
        up = jnp.dot(h, wu_ref[...], preferred_element_type=F32)
        act_ref[slot] = (_silu(gate) * up).astype(BF16)

    def down(slot):
        o_ref[...] += jnp.dot(act_ref[slot], wd_ref[...], preferred_element_type=F32)

    @pl.when(f == 0)
    def _():
        hn_ref[...] = _rms(x_ref[...], gpre_ref[...]).astype(BF16)
        o_ref[...] = jnp.zeros_like(o_ref)
        gate_up(0)

    @pl.when((f > 0) & (f < last))
    def _():
        down((f - 1) % 2)
        gate_up(f % 2)

    @pl.when(f == last)
    def _():
        down((f - 1) % 2)
        o_ref[...] = x_ref[...] + 0.5 * _rms(o_ref[...], gpost_ref[...])


def _ffn(x, gains, gain_base, pre, post, w_gu, w_dn, cast_next=None):
    m, d = x.shape
    ffn = w_dn.shape[0]
    tm = min(FFN_ROWS, m)
    tf = FFN_COLS
    nf = ffn // tf
    nm = m // tm
    assert m % tm == 0 and ffn % tf == 0
    def up_blk(f):
        return jnp.minimum(f, nf - 1)

    def dn_blk(f):
        return jnp.maximum(f - 1, 0)

    in_specs = [
        pl.BlockSpec((tm, d), lambda i, f: (i, 0)),
        pl.BlockSpec((None, 1, d), lambda i, f: (gain_base + pre, 0, 0)),
        pl.BlockSpec((None, 1, d), lambda i, f: (gain_base + post, 0, 0)),
        pl.BlockSpec((d, tf), lambda i, f: (0, up_blk(f))),
        pl.BlockSpec((d, tf), lambda i, f: (0, nf + up_blk(f))),
        pl.BlockSpec((tf, d), lambda i, f: (dn_blk(f), 0)),
    ]
    out_specs = pl.BlockSpec((tm, d), lambda i, f: (i, 0))
    out_shape = jax.ShapeDtypeStruct((m, d), F32)
    scratch = [pltpu.VMEM((tm, d), BF16), pltpu.VMEM((2, tm, tf), BF16)]
    if cast_next is None:
        return pl.pallas_call(
            _ffn_kernel, grid=(nm, nf + 1), in_specs=in_specs, out_specs=out_specs, out_shape=out_shape,
            scratch_shapes=scratch, compiler_params=_params("parallel", "arbitrary", vmem=FFN_VMEM_LIMIT),
        )(x, gains, gains, w_gu, w_gu, w_dn)
    gu_all, dn_all, layer, slot = cast_next
    gu_rows, gu_cols = d // nm, 2 * ffn // nf
    dn_rows = ffn // (nm * nf)
    assert gu_rows * nm == d and gu_cols * nf == 2 * ffn and dn_rows * nm * nf == ffn
    assert gu_rows % 16 == 0 and gu_cols % LANES == 0 and dn_rows % 16 == 0
    return pl.pallas_call(
        _ffn_cast_kernel,
        grid=(nm, nf + 1),
        in_specs=in_specs + [
            pl.BlockSpec((None, None, gu_rows, gu_cols), lambda i, f: (layer, slot, i, up_blk(f))),
            pl.BlockSpec((None, None, dn_rows, d), lambda i, f: (layer, slot, i * nf + up_blk(f), 0)),
        ],
        out_specs=[
            out_specs,
            pl.BlockSpec((gu_rows, gu_cols), lambda i, f: (i, up_blk(f))),
            pl.BlockSpec((dn_rows, d), lambda i, f: (i * nf + up_blk(f), 0)),
        ],
        out_shape=[out_shape, jax.ShapeDtypeStruct((d, 2 * ffn), BF16), jax.ShapeDtypeStruct((ffn, d), BF16)],
        scratch_shapes=scratch,
        compiler_params=_params("parallel", "arbitrary", vmem=FFN_VMEM_LIMIT),
    )(x, gains, gains, w_gu, w_gu, w_dn, gu_all, dn_all)


def _norm_matmul_kernel(x_ref, g_ref, w_ref, o_ref, hn_ref):
    @pl.when(pl.program_id(1) == 0)
    def _():
        hn_ref[...] = _rms(x_ref[...], g_ref[...]).astype(BF16)

    o_ref[...] = jnp.dot(hn_ref[...], w_ref[...], preferred_element_type=F32)


def _norm_matmul(x, gains, gain_idx, w, tn):
    m, d = x.shape
    n = w.shape[1]
    tm = min(PROJ_ROWS, m)
    assert m % tm == 0 and n % tn == 0
    return pl.pallas_call(
        _norm_matmul_kernel,
        grid=(m // tm, n // tn),
        in_specs=[
            pl.BlockSpec((tm, d), lambda i, j: (i, 0)),
            pl.BlockSpec((None, 1, d), lambda i, j: (gain_idx, 0, 0)),
            pl.BlockSpec((d, tn), lambda i, j: (0, j)),
        ],
        out_specs=pl.BlockSpec((tm, tn), lambda i, j: (i, j)),
        out_shape=jax.ShapeDtypeStruct((m, n), F32),
        scratch_shapes=[pltpu.VMEM((tm, d), BF16)],
        compiler_params=_params("parallel", "arbitrary"),
    )(x, gains, w)


def _norm_matmul_tail_kernel(x_ref, g_ref, w_ref, wt_ref, o_ref, ot_ref, hn_ref):
    j = pl.program_id(1)
    n_main = pl.num_programs(1) - 1

    @pl.when(j == 0)
    def _():
        hn_ref[...] = _rms(x_ref[...], g_ref[...]).astype(BF16)

    @pl.when(j < n_main)
    def _():
        o_ref[...] = jnp.dot(hn_ref[...], w_ref[...], preferred_element_type=F32)

    @pl.when(j == n_main)
    def _():
        ot_ref[...] = jnp.dot(hn_ref[...], wt_ref[...], preferred_element_type=F32)


def _norm_matmul_tail(x, gains, gain_idx, w, w_tail, tn):
    m, d = x.shape
    n, nt = w.shape[1], w_tail.shape[1]
    tm = min(PROJ_ROWS, m)
    n_main = n // tn
    assert m % tm == 0 and n % tn == 0

    def blk(j):
        return jnp.minimum(j, n_main - 1)

    return pl.pallas_call(
        _norm_matmul_tail_kernel,
        grid=(m // tm, n_main + 1),
        in_specs=[
            pl.BlockSpec((tm, d), lambda i, j: (i, 0)),
            pl.BlockSpec((None, 1, d), lambda i, j: (gain_idx, 0, 0)),
            pl.BlockSpec((d, tn), lambda i, j: (0, blk(j))),
            pl.BlockSpec((d, nt), lambda i, j: (0, 0)),
        ],
        out_specs=[
            pl.BlockSpec((tm, tn), lambda i, j: (i, blk(j))),
            pl.BlockSpec((tm, nt), lambda i, j: (i, 0)),
        ],
        out_shape=[jax.ShapeDtypeStruct((m, n), F32), jax.ShapeDtypeStruct((m, nt), F32)],
        scratch_shapes=[pltpu.VMEM((tm, d), BF16)],
        compiler_params=_params("parallel", "arbitrary"),
    )(x, gains, w, w_tail)


def _out_proj_kernel(x_ref, mix_ref, mem_ref, wmix_ref, wmem_ref, g_ref, o_ref):
    y = _bdot(mix_ref[...], wmix_ref[...]) + _bdot(mem_ref[...], wmem_ref[...])
    o_ref[...] = x_ref[...] + _rms(y, g_ref[...])


def _out_proj(x, mix, mem, w_out, gains, layer):
    m, d = x.shape
    wm = mix.shape[1]
    wx = mem.shape[1]
    tm = min(512, m)
    assert wm % wx == 0
    return pl.pallas_call(
        _out_proj_kernel,
        grid=(m // tm,),
        in_specs=[
            pl.BlockSpec((tm, d), lambda i: (i, 0)),
            pl.BlockSpec((tm, wm), lambda i: (i, 0)),
            pl.BlockSpec((tm, wx), lambda i: (i, 0)),
            pl.BlockSpec((None, wm, d), lambda i: (layer, 0, 0)),
            pl.BlockSpec((None, wx, d), lambda i: (layer, wm // wx, 0)),
            pl.BlockSpec((None, 1, d), lambda i: (layer * 6 + 3, 0, 0)),
        ],
        out_specs=pl.BlockSpec((tm, d), lambda i: (i, 0)),
        out_shape=jax.ShapeDtypeStruct((m, d), F32),
        compiler_params=_params("parallel"),
    )(x, mix, mem, w_out, w_out, gains)


def _unit_lower_inverse(lmat, c, span=None):
    span = c if span is None else span
    base = 4
    ri = lax.broadcasted_iota(jnp.int32, (c, c), 0)
    ci = lax.broadcasted_iota(jnp.int32, (c, c), 1)
    eye = (ri == ci).astype(F32)
    l_hi, l_lo = _split(lmat)

    def masked(keep):
        return jnp.where(keep, l_hi, jnp.zeros_like(l_hi)), jnp.where(keep, l_lo, jnp.zeros_like(l_lo))

    same = (ri // base) == (ci // base)
    dm = masked(same)
    x = eye - jnp.where(same, lmat, 0.0)
    x = x + _mm3(_split(x), _split(_mm3(dm, dm)))
    size = base
    while size < span:
        inner = ((ri // (2 * size)) == (ci // (2 * size))) & ((ri // size) != (ci // size))
        xs = _split(x)
        x = x - _mm3(_split(_mm3(xs, masked(inner))), xs)
        size *= 2
    return x


def _split(x):
    hi = x.astype(BF16)
    return hi, (x - hi.astype(F32)).astype(BF16)


def _mm3(a, b):
    def mm(p, q):
        return jnp.einsum("nij,njk->nik", p, q, preferred_element_type=F32)

    width = b[0].shape[-1]
    if a[0].shape[-1] % LANES == 0 and width % LANES == 0:
        lhs = jnp.concatenate([a[0], a[1]], axis=-1)
        rhs = jnp.concatenate([jnp.concatenate([b[0], b[1]], axis=-1),
                               jnp.concatenate([b[0], jnp.zeros_like(b[0])], axis=-1)], axis=-2)
        r = mm(lhs, rhs)
        return r[..., :width] + r[..., width:]
    return mm(a[0], b[0]) + (mm(a[0], b[1]) + mm(a[1], b[0]))


def _delta_pair_terms(q, k, v, gcol, bcol):
    n, cp, d = q.shape
    c = DELTA_CHUNK
    wu, a_intra, egc = _wy_solve(q, k, v, gcol, bcol, c)
    wu = wu.astype(BF16)
    awu = jnp.einsum("nij,nje->nie", a_intra, wu, preferred_element_type=F32)
    qt = (q * egc - awu[..., :d]).astype(BF16)
    o0 = awu[..., d:]
    k64 = k.reshape(2 * n, c, d)
    g64 = gcol.reshape(2 * n, c, 1)
    glast = g64[:, c - 1:c, :]
    k_dec = (k64 * jnp.exp(glast - g64)).astype(BF16)
    mb = jnp.einsum("ncd,nce->nde", k_dec, wu.reshape(2 * n, c, 2 * d), preferred_element_type=F32)
    return (qt.reshape(2 * n, c, d), mb[..., :d].astype(BF16), mb[..., d:], o0.reshape(2 * n, c, d),
            jnp.exp(glast))


def _wy_solve(q, k, v, gcol, bcol, span):
    rows = q.shape[1]
    ri = lax.broadcasted_iota(jnp.int32, (rows, rows), 0)
    ci = lax.broadcasted_iota(jnp.int32, (rows, rows), 1)
    eye = (ri == ci).astype(F32)
    same = (ri // span) == (ci // span)
    strict = same & (ci < ri)
    causal = same & (ci <= ri)
    grow = jnp.sum(gcol * eye, axis=1, keepdims=True)
    diff = gcol - grow
    kb = k.astype(BF16)
    kk = jnp.einsum("ncd,nmd->ncm", kb, kb, preferred_element_type=F32)
    lmat = jnp.where(strict, bcol * kk * jnp.exp(jnp.where(strict, diff, 0.0)), 0.0)
    tinv = _unit_lower_inverse(lmat, rows, span)
    egc = jnp.exp(gcol)
    rhs = jnp.concatenate([k * (bcol * egc), v * bcol], axis=-1)
    wu = _mm3(_split(tinv), _split(rhs))
    qk = jnp.einsum("ncd,nmd->ncm", q.astype(BF16), kb, preferred_element_type=F32)
    a_intra = jnp.where(causal, qk * jnp.exp(jnp.where(causal, diff, 0.0)), 0.0).astype(BF16)
    return wu, a_intra, egc


def _l2norm(x):
    return x * lax.rsqrt(jnp.sum(x * x, axis=-1, keepdims=True) + EPS)


def _block_cumsum(g, c):
    rows, lanes = g.shape
    n = rows // c
    ri = lax.broadcasted_iota(jnp.int32, (n, c, c), 1)
    ci = lax.broadcasted_iota(jnp.int32, (n, c, c), 2)
    tri = (ci <= ri).astype(BF16)
    g3 = g.reshape(n, c, lanes)
    hi = g3.astype(BF16)
    r1 = g3 - hi.astype(F32)
    mid = r1.astype(BF16)
    lo = (r1 - mid.astype(F32)).astype(BF16)

    def mm(p):
        return jnp.einsum("nij,njk->nik", tri, p, preferred_element_type=F32)

    return (mm(hi) + (mm(mid) + mm(lo))).reshape(rows, lanes)


def _shifted(x, prev, shift, row):
    xs = pltpu.roll(x, shift, x.ndim - 2)
    if x.ndim == 2:
        ps = pltpu.roll(prev, shift, 0)
        reps = x.shape[0] // SUBLANES
        ps = jnp.broadcast_to(ps[None], (reps,) + ps.shape).reshape(x.shape)
        return jnp.where(row < shift, ps, xs)
    for t in range(shift):
        src = prev.shape[1] - shift + t
        xs = jnp.where(row == t, prev[:, src:src + 1, :], xs)
    return xs


def _short_conv_silu(x, prev, w, row):
    y = x * w[CONV_W - 1:CONV_W]
    for shift in range(1, CONV_W):
        tap = CONV_W - 1 - shift
        y = y + _shifted(x, prev, shift, row) * w[tap:tap + 1]
    return _silu(y)


def _delta_prompt_kernel(q_ref, k_ref, v_ref, z_ref, ab_ref, wq_ref, wk_ref, wv_ref, alog_ref, dtb_ref,
                         og_ref, o_ref, sout_ref, s_ref, pq_ref, pk_ref, pv_ref, bq_ref, bk_ref, bv_ref, *, tt):
    t = pl.program_id(2)
    c = DELTA_CHUNK
    n = tt // c
    d = HEAD_DIM

    @pl.when(t == 0)
    def _():
        s_ref[...] = jnp.zeros_like(s_ref)
        for ref in (pq_ref, pk_ref, pv_ref, bq_ref, bk_ref, bv_ref):
            ref[...] = jnp.zeros_like(ref)

    row = lax.broadcasted_iota(jnp.int32, (tt, 1), 0)

    def conv(x_ref, p_ref, b_ref, w_ref):
        x = x_ref[...]
        w = w_ref[...]
        x1 = _shifted(x, p_ref[...], 1, row)
        b = x * w[1:2] + x1 * w[0:1]
        y = x * w[3:4] + x1 * w[2:3] + _shifted(b, b_ref[...], 2, row)
        p_ref[...] = x[tt - SUBLANES:, :]
        b_ref[...] = b[tt - SUBLANES:, :]
        return _silu(y)

    assert CONV_W == 4
    qc = conv(q_ref, pq_ref, bq_ref, wq_ref)
    kc = conv(k_ref, pk_ref, bk_ref, wk_ref)
    vc = conv(v_ref, pv_ref, bv_ref, wv_ref)

    ab = ab_ref[...]
    g_all = -jnp.exp(alog_ref[...]) * _softplus(ab + dtb_ref[...])
    beta_all = _sigmoid(ab)
    gc_all = _block_cumsum(g_all, c)

    heads = range(HEAD_GROUP)

    def pairs(cols):
        return jnp.concatenate([x.reshape(n // 2, 2 * c, x.shape[-1]) for x in cols], axis=0)

    q = pairs([_l2norm(qc[:, hh * d:(hh + 1) * d]) * (d ** -0.5) for hh in heads])
    k = pairs([_l2norm(kc[:, hh * d:(hh + 1) * d]) for hh in heads])
    v = pairs([vc[:, hh * d:(hh + 1) * d] for hh in heads])
    gcol = pairs([gc_all[:, hh:hh + 1] for hh in heads])
    bcol = pairs([beta_all[:, HEAD_GROUP + hh:HEAD_GROUP + hh + 1] for hh in heads])
    qt, m, bmat, o0, g_last = _delta_pair_terms(q, k, v, gcol, bcol)

    gain = og_ref[...]
    states = [s_ref[hh] for hh in heads]
    for ch in range(n):
        for hh in heads:
            i = hh * n + ch
            s = states[hh]
            r = jnp.dot(jnp.concatenate([qt[i], m[i]], axis=0), s.astype(BF16),
                        preferred_element_type=F32)
            o = r[:c] + o0[i]
            states[hh] = s * g_last[i] + (bmat[i] - r[c:])
            zz = z_ref[ch * c:(ch + 1) * c, hh * d:(hh + 1) * d]
            o_ref[ch * c:(ch + 1) * c, hh * d:(hh + 1) * d] = _rms(o, gain) * _silu(zz)
    for hh in range(HEAD_GROUP):
        s_ref[hh] = states[hh]
        sout_ref[hh] = states[hh]


def _delta_prompt(proj, tail, batch, seq, conv_w, alog_rows, dtb_rows, out_gain):
    tt = min(512, seq)
    nt = seq // tt
    gw = HEAD_GROUP * HEAD_DIM
    ng = N_HEAD_GROUPS
    assert seq % tt == 0 and tt % DELTA_CHUNK == 0

    def col(base):
        return pl.BlockSpec((tt, gw), lambda b, g, t: (b * nt + t, base * ng + g))

    def wcol(base):
        return pl.BlockSpec((CONV_W, gw), lambda b, g, t: (0, base * ng + g))

    ab_base = X_W // LANES
    return pl.pallas_call(
        functools.partial(_delta_prompt_kernel, tt=tt),
        grid=(batch, ng, nt),
        in_specs=[
            col(0), col(1), col(2), col(3),
            pl.BlockSpec((tt, LANES), lambda b, g, t: (b * nt + t, ab_base + g)),
            wcol(0), wcol(1), wcol(2),
            pl.BlockSpec((1, LANES), lambda b, g, t: (0, g)),
            pl.BlockSpec((1, LANES), lambda b, g, t: (0, g)),
            pl.BlockSpec((1, HEAD_DIM), lambda b, g, t: (0, 0)),
        ],
        out_specs=[
            pl.BlockSpec((tt, gw), lambda b, g, t: (b * nt + t, g)),
            pl.BlockSpec((None, HEAD_GROUP, HEAD_DIM, HEAD_DIM), lambda b, g, t: (b, g, 0, 0)),
        ],
        out_shape=[
            jax.ShapeDtypeStruct((batch * seq, DELTA_W), F32),
            jax.ShapeDtypeStruct((batch, DELTA_HEADS, HEAD_DIM, HEAD_DIM), F32),
        ],
        scratch_shapes=[
            pltpu.VMEM((HEAD_GROUP, HEAD_DIM, HEAD_DIM), F32),
        ] + [pltpu.VMEM((SUBLANES, gw), F32)] * 6,
        compiler_params=_params("parallel", "parallel", "arbitrary"),
    )(proj, proj, proj, proj, tail, conv_w, conv_w, conv_w, alog_rows, dtb_rows, out_gain)


def _delta_sample_kernel(q_ref, k_ref, v_ref, z_ref, ab_ref, bq_ref, bk_ref, bv_ref, s0_ref, wq_ref, wk_ref,
                         wv_ref, alog_ref, dtb_ref, og_ref, o_ref, sout_ref, *, bb, seq):
    d = HEAD_DIM
    rows = bb * seq
    row = lax.broadcasted_iota(jnp.int32, (1, seq, 1), 1)

    def conv(x_ref, buf_ref, w_ref):
        x = x_ref[...].reshape(bb, seq, x_ref.shape[1])
        return _short_conv_silu(x, buf_ref[...], w_ref[...], row).reshape(rows, x_ref.shape[1])

    qc = conv(q_ref, bq_ref, wq_ref)
    kc = conv(k_ref, bk_ref, wk_ref)
    vc = conv(v_ref, bv_ref, wv_ref)

    ab = ab_ref[...]
    g_all = -jnp.exp(alog_ref[...]) * _softplus(ab + dtb_ref[...])
    beta_all = _sigmoid(ab)
    gc_all = _block_cumsum(g_all, seq)

    heads = range(HEAD_GROUP)
    q = jnp.stack([_l2norm(qc[:, hh * d:(hh + 1) * d]) * (d ** -0.5) for hh in heads])
    k = jnp.stack([_l2norm(kc[:, hh * d:(hh + 1) * d]) for hh in heads])
    v = jnp.stack([vc[:, hh * d:(hh + 1) * d] for hh in heads])
    gcol = jnp.stack([gc_all[:, hh:hh + 1] for hh in heads])
    bcol = jnp.stack([beta_all[:, HEAD_GROUP + hh:HEAD_GROUP + hh + 1] for hh in heads])
    wu, a_intra, egc = _wy_solve(q, k, v, gcol, bcol, seq)
    q_dec = (q * egc).astype(BF16)
    g4 = gcol.reshape(HEAD_GROUP, bb, seq, 1)
    glast = g4[:, :, seq - 1:seq, :]
    k_dec = (k.reshape(HEAD_GROUP, bb, seq, d) * jnp.exp(glast - g4)).astype(BF16)
    g_last = jnp.exp(glast)

    gain = og_ref[...]
    for hh in heads:
        s = s0_ref[:, hh]
        w = wu[hh, :, :d].astype(BF16).reshape(bb, seq, d)
        u = wu[hh, :, d:].reshape(bb, seq, d)
        wq = jnp.concatenate([w, q_dec[hh].reshape(bb, seq, d)], axis=1)
        r = jnp.einsum("bcd,bde->bce", wq, s.astype(BF16), preferred_element_type=F32)
        v_new = u - r[:, :seq]
        vb = v_new.astype(BF16)
        o = r[:, seq:].reshape(rows, d) + jnp.dot(a_intra[hh], vb.reshape(rows, d), preferred_element_type=F32)
        sout_ref[:, hh] = s * g_last[hh] + jnp.einsum("bcd,bce->bde", k_dec[hh], vb,
                                                      preferred_element_type=F32)
        o_ref[:, hh * d:(hh + 1) * d] = _rms(o, gain) * _silu(z_ref[:, hh * d:(hh + 1) * d])


def _delta_sample_into_kernel(*refs, bb, seq):
    _delta_sample_kernel(*refs[:15], *refs[16:], bb=bb, seq=seq)


def _delta_sample_first_kernel(*refs, bb, seq, layer):
    sout_ref = refs[-1]
    for other in range(sout_ref.shape[0]):
        if other != layer:
            sout_ref[other] = jnp.zeros(sout_ref.shape[1:], F32)
    _delta_sample_kernel(*refs[:-1], sout_ref.at[layer], bb=bb, seq=seq)


def _delta_sample(proj, tail, batch, seq, conv_bufs, states, j, conv_w, alog_rows, dtb_rows, out_gain, new_states):
    bb = LANES // seq
    assert batch % bb == 0 and seq == SUBLANES
    rows = bb * seq
    gw = HEAD_GROUP * HEAD_DIM
    ng = N_HEAD_GROUPS

    def col(base):
        return pl.BlockSpec((rows, gw), lambda i, g: (i, base * ng + g))

    def bufcol(base):
        return pl.BlockSpec((None, bb, CONV_W - 1, gw), lambda i, g: (j, i, 0, base * ng + g))

    def wcol(base):
        return pl.BlockSpec((CONV_W, gw), lambda i, g: (0, base * ng + g))

    ab_base = X_W // LANES
    state_spec = pl.BlockSpec((None, bb, HEAD_GROUP, HEAD_DIM, HEAD_DIM), lambda i, g: (j, i, g, 0, 0))
    in_specs = [
        col(0), col(1), col(2), col(3),
        pl.BlockSpec((rows, LANES), lambda i, g: (i, ab_base + g)),
        bufcol(0), bufcol(1), bufcol(2),
        state_spec,
        wcol(0), wcol(1), wcol(2),
        pl.BlockSpec((1, LANES), lambda i, g: (0, g)),
        pl.BlockSpec((1, LANES), lambda i, g: (0, g)),
        pl.BlockSpec((1, HEAD_DIM), lambda i, g: (0, 0)),
    ]
    args = (proj, proj, proj, proj, tail, conv_bufs, conv_bufs, conv_bufs, states, conv_w, conv_w, conv_w,
            alog_rows, dtb_rows, out_gain)
    if new_states is None:
        body, aliases = functools.partial(_delta_sample_first_kernel, bb=bb, seq=seq, layer=j), {}
        out_state_spec = pl.BlockSpec((states.shape[0], bb, HEAD_GROUP, HEAD_DIM, HEAD_DIM),
                                      lambda i, g: (0, i, g, 0, 0))
    else:
        in_specs.append(pl.BlockSpec(memory_space=pl.ANY))
        args += (new_states,)
        body, aliases = functools.partial(_delta_sample_into_kernel, bb=bb, seq=seq), {len(args) - 1: 1}
        out_state_spec = state_spec
    return pl.pallas_call(
        body,
        grid=(batch // bb, ng),
        in_specs=in_specs,
        out_specs=[
            pl.BlockSpec((rows, gw), lambda i, g: (i, g)),
            out_state_spec,
        ],
        out_shape=[
            jax.ShapeDtypeStruct((batch * seq, DELTA_W), F32),
            jax.ShapeDtypeStruct(states.shape, F32),
        ],
        input_output_aliases=aliases,
        compiler_params=_params("parallel", "parallel"),
    )(*args)


def _gelu(x):
    return 0.5 * x * (1.0 + lax.erf(x * (2.0 ** -0.5)))


def _layernorm(x, gain, bias):
    mu = jnp.mean(x, axis=-1, keepdims=True)
    xc = x - mu
    var = jnp.mean(xc * xc, axis=-1, keepdims=True)
    return xc * lax.rsqrt(var + EPS) * gain + bias


def _gmlp_prompt_kernel(u_ref, v_ref, lng_ref, lnb_ref, ws_ref, bs_ref, o_ref, *, chunks):
    c = GMLP_CHUNK
    d = HEAD_DIM
    ri = lax.broadcasted_iota(jnp.int32, (c, c), 0)
    ci = lax.broadcasted_iota(jnp.int32, (c, c), 1)
    lower = ci <= ri
    v = _layernorm(_gelu(v_ref[...]), lng_ref[...], lnb_ref[...]).astype(BF16)
    bs = bs_ref[...]
    for g in range(GMLP_GROUPS):
        w = jnp.where(lower, ws_ref[g], 0.0).astype(BF16)
        bias = bs[:, g:g + 1]
        for ch in range(chunks):
            rs = slice(ch * c, (ch + 1) * c)
            cs = slice(g * d, (g + 1) * d)
            mixed = jnp.dot(w, v[rs, cs], preferred_element_type=F32) + bias
            o_ref[rs, cs] = _gelu(u_ref[rs, cs]) * mixed


def _gmlp_prompt(proj, ln_gain, ln_bias, w_s, b_s_t):
    m = proj.shape[0]
    chunks = 2
    tm = chunks * GMLP_CHUNK
    assert m % tm == 0
    return pl.pallas_call(
        functools.partial(_gmlp_prompt_kernel, chunks=chunks),
        grid=(m // tm,),
        in_specs=[
            pl.BlockSpec((tm, GMLP_W), lambda i: (i, 0)),
            pl.BlockSpec((tm, GMLP_W), lambda i: (i, 1)),
            pl.BlockSpec((1, GMLP_W), lambda i: (0, 0)),
            pl.BlockSpec((1, GMLP_W), lambda i: (0, 0)),
            pl.BlockSpec((GMLP_GROUPS, GMLP_CHUNK, GMLP_CHUNK), lambda i: (0, 0, 0)),
            pl.BlockSpec((GMLP_CHUNK, GMLP_GROUPS), lambda i: (0, 0)),
        ],
        out_specs=pl.BlockSpec((tm, GMLP_W), lambda i: (i, 0)),
        out_shape=jax.ShapeDtypeStruct((m, GMLP_W), F32),
        compiler_params=_params("parallel"),
    )(proj, proj, ln_gain, ln_bias, w_s, b_s_t)


def _gmlp_sample_kernel(u_ref, v_ref, lng_ref, lnb_ref, wx_ref, bx_ref, o_ref, vout_ref, *, bb, seq):
    v = _layernorm(_gelu(v_ref[...]), lng_ref[...], lnb_ref[...])
    vout_ref[...] = v
    v3 = v.reshape(bb, seq, GMLP_W)
    row = lax.broadcasted_iota(jnp.int32, (seq, 1), 0)
    mixed = jnp.broadcast_to(bx_ref[...][None], (bb, seq, GMLP_W))
    for j in range(seq):
        wj = jnp.where(row >= j, wx_ref[j], 0.0)
        mixed = mixed + wj[None] * v3[:, j:j + 1, :]
    o_ref[...] = _gelu(u_ref[...]) * mixed.reshape(bb * seq, GMLP_W)


def _gmlp_sample(proj, batch, seq, ln_gain, ln_bias, w_exp, b_exp):
    bb = 32
    assert batch % bb == 0 and seq == SUBLANES
    rows = bb * seq
    return pl.pallas_call(
        functools.partial(_gmlp_sample_kernel, bb=bb, seq=seq),
        grid=(batch // bb,),
        in_specs=[
            pl.BlockSpec((rows, GMLP_W), lambda i: (i, 0)),
            pl.BlockSpec((rows, GMLP_W), lambda i: (i, 1)),
            pl.BlockSpec((1, GMLP_W), lambda i: (0, 0)),
            pl.BlockSpec((1, GMLP_W), lambda i: (0, 0)),
            pl.BlockSpec((seq, seq, GMLP_W), lambda i: (0, 0, 0)),
            pl.BlockSpec((seq, GMLP_W), lambda i: (0, 0)),
        ],
        out_specs=[
            pl.BlockSpec((rows, GMLP_W), lambda i: (i, 0)),
            pl.BlockSpec((rows, GMLP_W), lambda i: (i, 0)),
        ],
        out_shape=[
            jax.ShapeDtypeStruct((batch * seq, GMLP_W), F32),
            jax.ShapeDtypeStruct((batch * seq, GMLP_W), F32),
        ],
        compiler_params=_params("parallel"),
    )(proj, proj, ln_gain, ln_bias, w_exp, b_exp)


def _softmax(s):
    m = jnp.max(s, axis=-1, keepdims=True)
    e = jnp.exp(s - m)
    return e / jnp.sum(e, axis=-1, keepdims=True)


def _mem_attn_prompt_kernel(q_ref, k_ref, v_ref, o_ref):
    d = HEAD_DIM
    for h in range(X_HEADS):
        sl = slice(h * d, (h + 1) * d)
        s = lax.dot_general(q_ref[:, sl].astype(BF16), k_ref[:, sl].astype(BF16),
                            (((1,), (1,)), ((), ())), preferred_element_type=F32) * (d ** -0.5)
        o_ref[:, sl] = _bdot(_softmax(s), v_ref[:, sl])


def _mem_attn_prompt(proj, q_block, batch, seq, mem_k, mem_v):
    tt = min(512, seq)
    nt = seq // tt
    n_mem = mem_k.shape[1]
    return pl.pallas_call(
        _mem_attn_prompt_kernel,
        grid=(batch, nt),
        in_specs=[
            pl.BlockSpec((tt, X_W), lambda b, t: (b * nt + t, q_block)),
            pl.BlockSpec((None, n_mem, X_W), lambda b, t: (b, 0, 0)),
            pl.BlockSpec((None, n_mem, X_W), lambda b, t: (b, 0, 0)),
        ],
        out_specs=pl.BlockSpec((tt, X_W), lambda b, t: (b * nt + t, 0)),
        out_shape=jax.ShapeDtypeStruct((batch * seq, X_W), F32),
        compiler_params=_params("parallel", "parallel"),
    )(proj, mem_k, mem_v)


def _mem_attn_sample_kernel(q_ref, k_ref, v_ref, o_ref, *, bb, seq):
    d = HEAD_DIM
    nq = X_HEADS * seq
    nk = k_ref.shape[1]
    q = jnp.concatenate([q_ref[:, h * d:(h + 1) * d].reshape(bb, seq, d) for h in range(X_HEADS)], axis=1)
    s = jnp.einsum("bqd,bkd->bqk", q.astype(BF16), k_ref[...].astype(BF16),
                   preferred_element_type=F32) * (d ** -0.5)
    q_head = lax.broadcasted_iota(jnp.int32, (nq, nk), 0) // seq
    k_head = lax.broadcasted_iota(jnp.int32, (nq, nk), 1) % X_HEADS
    p = _softmax(jnp.where(q_head == k_head, s, -1e30)).astype(BF16)
    o = jnp.einsum("bqk,bkd->bqd", p, v_ref[...].astype(BF16), preferred_element_type=F32)
    for h in range(X_HEADS):
        o_ref[:, h * d:(h + 1) * d] = o[:, h * seq:(h + 1) * seq, :].reshape(bb * seq, d)


def _mem_attn_sample(proj, q_block, batch, seq, mem_k, mem_v, layer):
    bb = 8
    rows = bb * seq
    n_rows = mem_k.shape[2]
    return pl.pallas_call(
        functools.partial(_mem_attn_sample_kernel, bb=bb, seq=seq),
        grid=(batch // bb,),
        in_specs=[
            pl.BlockSpec((rows, X_W), lambda i: (i, q_block)),
            pl.BlockSpec((None, bb, n_rows, HEAD_DIM), lambda i: (layer, i, 0, 0)),
            pl.BlockSpec((None, bb, n_rows, HEAD_DIM), lambda i: (layer, i, 0, 0)),
        ],
        out_specs=pl.BlockSpec((rows, X_W), lambda i: (i, 0)),
        out_shape=jax.ShapeDtypeStruct((batch * seq, X_W), F32),
        compiler_params=_params("parallel"),
    )(proj, mem_k, mem_v)


def _delta_in_weight(w_in_a):
    d_model = w_in_a.shape[0]
    a = w_in_a[:, 4 * DELTA_W:4 * DELTA_W + DELTA_HEADS]
    b = w_in_a[:, 4 * DELTA_W + DELTA_HEADS:4 * DELTA_W + 2 * DELTA_HEADS]
    parts = [w_in_a[:, 4 * DELTA_W + 2 * DELTA_HEADS:]]
    pad = jnp.zeros((d_model, LANES - 2 * HEAD_GROUP), w_in_a.dtype)
    for g in range(N_HEAD_GROUPS):
        hs = slice(g * HEAD_GROUP, (g + 1) * HEAD_GROUP)
        parts += [a[:, hs], b[:, hs], pad]
    parts.append(jnp.zeros((d_model, AB_W - N_HEAD_GROUPS * LANES), w_in_a.dtype))
    return w_in_a[:, :4 * DELTA_W].astype(BF16), jnp.concatenate(parts, axis=1).astype(BF16)


def _gate_rows(vec):
    row = jnp.zeros((N_HEAD_GROUPS, LANES), F32)
    row = row.at[:, :HEAD_GROUP].set(vec.astype(F32).reshape(N_HEAD_GROUPS, HEAD_GROUP))
    return jnp.concatenate([row.reshape(1, -1), jnp.zeros((1, AB_W - N_HEAD_GROUPS * LANES), F32)], axis=1)


def _ffn_step(x, p, layer, slot, pre, post, round_next):
    w_gu, w_dn = p["ffn_bf16"][(layer, slot)]
    nxt = (layer, 1) if slot == 0 else (layer + 1, 0)
    if round_next and nxt[0] < p["w_out"].shape[0]:
        x, gu_next, dn_next = _ffn(x, p["gains"], layer * 6, pre, post, w_gu, w_dn,
                                   cast_next=(p["w_ffn_gu"], p["w_ffn_dn"]) + nxt)
        p["ffn_bf16"][nxt] = (gu_next, dn_next)
        return x
    return _ffn(x, p["gains"], layer * 6, pre, post, w_gu, w_dn)


def _trunk(x, batch, seq, is_prompt, mem_k, mem_v, conv_bufs, delta_states, p):
    new_conv, new_delta, new_v = [], [], []
    gains = p["gains"]
    depth = p["w_out"].shape[0]
    for i in range(depth):
        j = i // N_MIXERS
        x = _ffn_step(x, p, i, 0, 0, 1, is_prompt)
        if i % N_MIXERS == 0:
            w_main, w_tail = p["w_in_a"][j]
            proj, q_src = _norm_matmul_tail(x, gains, i * 6 + 2, w_main, w_tail, PROJ_COLS_A)
            q_block = 0
            if is_prompt:
                mix, s_new = _delta_prompt(proj, q_src, batch, seq, p["conv_w"][j], p["alog_rows"][j],
                                           p["dtb_rows"][j], p["delta_norm_gain"][j])
                new_delta.append(s_new)
            else:
                mix, new_delta = _delta_sample(proj, q_src, batch, seq, conv_bufs, delta_states, j, p["conv_w"][j],
                                               p["alog_rows"][j], p["dtb_rows"][j], p["delta_norm_gain"][j],
                                               new_delta if j > 0 else None)
            new_conv.append(proj.reshape(batch, seq, -1)[:, seq - (CONV_W - 1):, :3 * DELTA_W])
        else:
            proj = q_src = _norm_matmul(x, gains, i * 6 + 2, p["w_in_b"][j], PROJ_COLS)
            q_block = 2 * GMLP_W // X_W
            if is_prompt:
                mix = _gmlp_prompt(proj, p["gmlp_ln_gain"][j], p["gmlp_ln_bias"][j], p["w_spatial"][j],
                                   p["b_spatial_t"][j])
            else:
                mix, v_rows = _gmlp_sample(proj, batch, seq, p["gmlp_ln_gain"][j], p["gmlp_ln_bias"][j],
                                           p["w_spatial_exp"][j], p["b_spatial_exp"][j])
                new_v.append(v_rows.reshape(batch, seq, GMLP_W))
        if is_prompt:
            mem_out = _mem_attn_prompt(q_src, q_block, batch, seq, mem_k[i], mem_v[i])
        else:
            mem_out = _mem_attn_sample(q_src, q_block, batch, seq, mem_k, mem_v, i)
        x = _out_proj(x, mix, mem_out, p["w_out"], gains, i)
        x = _ffn_step(x, p, i, 1, 4, 5, is_prompt)
    return x, new_conv, new_delta, new_v


def kernel(x_prompt, x_sample, mem_prompt, cache_mem_k, cache_mem_v, state_delta, state_conv, norm_gains, w_ffn_gu, w_ffn_dn, w_in_a, conv_w, a_log, dt_bias, delta_norm_gain, w_in_b, gmlp_ln_gain, gmlp_ln_bias, w_spatial, b_spatial, mem_norm_gain, w_mem_kv, w_out):
    batch, seq, d_model = x_prompt.shape
    dec_batch, dec_seq, _ = x_sample.shape
    depth = w_out.shape[0]
    n_mem = mem_prompt.shape[1]
    n_a = w_in_a.shape[0]
    n_b = w_in_b.shape[0]

    w_s_dec = w_spatial[:, :, :dec_seq, :dec_seq]
    p = {
        "gains": norm_gains.reshape(depth * 6, 1, d_model),
        "w_ffn_gu": w_ffn_gu,
        "w_ffn_dn": w_ffn_dn,
        "ffn_bf16": {(0, 0): (w_ffn_gu[0, 0].astype(BF16), w_ffn_dn[0, 0].astype(BF16))},
        "w_in_a": [_delta_in_weight(w_in_a[j]) for j in range(n_a)],
        "w_in_b": [w_in_b[j].astype(BF16) for j in range(n_b)],
        "w_out": w_out.astype(BF16),
        "conv_w": conv_w,
        "alog_rows": [_gate_rows(a_log[j]) for j in range(n_a)],
        "dtb_rows": [_gate_rows(dt_bias[j]) for j in range(n_a)],
        "delta_norm_gain": delta_norm_gain.reshape(n_a, 1, HEAD_DIM),
        "gmlp_ln_gain": gmlp_ln_gain.reshape(n_b, 1, GMLP_W),
        "gmlp_ln_bias": gmlp_ln_bias.reshape(n_b, 1, GMLP_W),
        "w_spatial": w_spatial,
        "b_spatial_t": jnp.swapaxes(b_spatial, 1, 2),
        "w_spatial_exp": jnp.repeat(jnp.transpose(w_s_dec, (0, 3, 2, 1)), HEAD_DIM, axis=-1),
        "b_spatial_exp": jnp.repeat(jnp.swapaxes(b_spatial[:, :, :dec_seq], 1, 2), HEAD_DIM, axis=-1),
    }

    mem2d = mem_prompt.reshape(batch * n_mem, d_model)
    mem_gains = mem_norm_gain.reshape(depth, 1, d_model)
    w_kv = w_mem_kv.astype(BF16)
    kvs = [_norm_matmul(mem2d, mem_gains, i, w_kv[i], 1024) for i in range(depth)]
    mem_k_prompt = jnp.stack([kv[:, :X_W].reshape(batch, n_mem, X_W) for kv in kvs])
    mem_v_prompt = jnp.stack([kv[:, X_W:].reshape(batch, n_mem, X_W) for kv in kvs])

    y_prompt, conv_p, delta_p, _ = _trunk(x_prompt.reshape(batch * seq, d_model), batch, seq, True,
                                          mem_k_prompt, mem_v_prompt, None, None, p)

    y_sample, conv_s, delta_s, v_s = _trunk(
        x_sample.reshape(dec_batch * dec_seq, d_model), dec_batch, dec_seq, False,
        cache_mem_k.reshape(depth, dec_batch, n_mem * X_HEADS, HEAD_DIM),
        cache_mem_v.reshape(depth, dec_batch, n_mem * X_HEADS, HEAD_DIM),
        state_conv, state_delta, p)

    return (y_prompt.reshape(batch, seq, d_model),
            y_sample.reshape(dec_batch, dec_seq, d_model),
            mem_k_prompt.reshape(depth, batch, n_mem, X_HEADS, HEAD_DIM),
            mem_v_prompt.reshape(depth, batch, n_mem, X_HEADS, HEAD_DIM),
            jnp.stack(delta_p), jnp.stack(conv_p), delta_s, jnp.stack(conv_s), jnp.stack(v_s))
```

```python
import functools

import jax
import jax.numpy as jnp
from jax import lax
from jax.experimental import pallas as pl
from jax.experimental.pallas import tpu as pltpu

F32 = jnp.float32
BF16 = jnp.bfloat16

EPS = 1e-6
HEAD_DIM = 128
X_HEADS = 4
X_W = X_HEADS * HEAD_DIM
DELTA_HEADS = 12
DELTA_W = DELTA_HEADS * HEAD_DIM
CONV_W = 4
DELTA_CHUNK = 64
GMLP_GROUPS = 12
GMLP_W = GMLP_GROUPS * HEAD_DIM
GMLP_CHUNK = 128
N_MIXERS = 2

LANES = 128
SUBLANES = 8
VMEM_LIMIT = 56 * 1024 * 1024
FFN_VMEM_LIMIT = 62 * 1024 * 1024

HEAD_GROUP = 6
N_HEAD_GROUPS = DELTA_HEADS // HEAD_GROUP
FFN_ROWS = 1024
FFN_COLS = 512
PROJ_ROWS = 1024
PROJ_COLS = 1792
PROJ_COLS_A = 1024
AB_W = 512
TAIL_W = X_W + AB_W


def _params(*sem, vmem=VMEM_LIMIT):
    return pltpu.CompilerParams(dimension_semantics=sem, vmem_limit_bytes=vmem)


def _rms(x, gain):
    ms = jnp.mean(x * x, axis=-1, keepdims=True)
    return x * lax.rsqrt(ms + EPS) * gain


def _sigmoid(x):
    return 0.5 * jnp.tanh(0.5 * x) + 0.5


def _silu(x):
    return x * _sigmoid(x)


def _softplus(x):
    return jnp.maximum(x, 0.0) + jnp.log1p(jnp.exp(-jnp.abs(x)))


def _bdot(a, b):
    return jnp.dot(a.astype(BF16), b.astype(BF16), preferred_element_type=F32)


def _ffn_cast_kernel(x_ref, gpre_ref, gpost_ref, wg_ref, wu_ref, wd_ref, cgu_ref, cdn_ref,
                     o_ref, ogu_ref, odn_ref, hn_ref, act_ref):
    ogu_ref[...] = cgu_ref[...].astype(BF16)
    odn_ref[...] = cdn_ref[...].astype(BF16)
    _ffn_kernel(x_ref, gpre_ref, gpost_ref, wg_ref, wu_ref, wd_ref, o_ref, hn_ref, act_ref)


def _ffn_kernel(x_ref, gpre_ref, gpost_ref, wg_ref, wu_ref, wd_ref, o_ref, hn_ref, act_ref):
    f = pl.program_id(1)
    last = pl.num_programs(1) - 1

    def gate_up(slot):
        h = hn_ref[...]
        gate = jnp.dot(h, wg_ref[...], preferred_element_type=F32)
        up = jnp.dot(h, wu_ref[...], preferred_element_type=F32)
        act_ref[slot] = (_silu(gate) * up).astype(BF16)

    def down(slot):
        o_ref[...] += jnp.dot(act_ref[slot], wd_ref[...], preferred_element_type=F32)

    @pl.when(f == 0)
    def _():
        hn_ref[...] = _rms(x_ref[...], gpre_ref[...]).astype(BF16)
        o_ref[...] = jnp.zeros_like(o_ref)
        gate_up(0)

    @pl.when((f > 0) & (f < last))
    def _():
        down((f - 1) % 2)
        gate_up(f % 2)

    @pl.when(f == last)
    def _():
        down((f - 1) % 2)
        o_ref[...] = x_ref[...] + 0.5 * _rms(o_ref[...], gpost_ref[...])


def _ffn(x, gains, gain_base, pre, post, w_gu, w_dn, cast_next=None):
    m, d = x.shape
    ffn = w_dn.shape[0]
    tm = min(FFN_ROWS, m)
    tf = FFN_COLS
    nf = ffn // tf
    nm = m // tm
    assert m % tm == 0 and ffn % tf == 0
    def up_blk(f):
        return jnp.minimum(f, nf - 1)

    def dn_blk(f):
        return jnp.maximum(f - 1, 0)

    in_specs = [
        pl.BlockSpec((tm, d), lambda i, f: (i, 0)),
        pl.BlockSpec((None, 1, d), lambda i, f: (gain_base + pre, 0, 0)),
        pl.BlockSpec((None, 1, d), lambda i, f: (gain_base + post, 0, 0)),
        pl.BlockSpec((d, tf), lambda i, f: (0, up_blk(f))),
        pl.BlockSpec((d, tf), lambda i, f: (0, nf + up_blk(f))),
        pl.BlockSpec((tf, d), lambda i, f: (dn_blk(f), 0)),
    ]
    out_specs = pl.BlockSpec((tm, d), lambda i, f: (i, 0))
    out_shape = jax.ShapeDtypeStruct((m, d), F32)
    scratch = [pltpu.VMEM((tm, d), BF16), pltpu.VMEM((2, tm, tf), BF16)]
    if cast_next is None:
        return pl.pallas_call(
            _ffn_kernel, grid=(nm, nf + 1), in_specs=in_specs, out_specs=out_specs, out_shape=out_shape,
            scratch_shapes=scratch, compiler_params=_params("parallel", "arbitrary", vmem=FFN_VMEM_LIMIT),
        )(x, gains, gains, w_gu, w_gu, w_dn)
    gu_all, dn_all, layer, slot = cast_next
    gu_rows, gu_cols = d // nm, 2 * ffn // nf
    dn_rows = ffn // (nm * nf)
    assert gu_rows * nm == d and gu_cols * nf == 2 * ffn and dn_rows * nm * nf == ffn
    assert gu_rows % 16 == 0 and gu_cols % LANES == 0 and dn_rows % 16 == 0
    return pl.pallas_call(
        _ffn_cast_kernel,
        grid=(nm, nf + 1),
        in_specs=in_specs + [
            pl.BlockSpec((None, None, gu_rows, gu_cols), lambda i, f: (layer, slot, i, up_blk(f))),
            pl.BlockSpec((None, None, dn_rows, d), lambda i, f: (layer, slot, i * nf + up_blk(f), 0)),
        ],
        out_specs=[
            out_specs,
            pl.BlockSpec((gu_rows, gu_cols), lambda i, f: (i, up_blk(f))),
            pl.BlockSpec((dn_rows, d), lambda i, f: (i * nf + up_blk(f), 0)),
        ],
        out_shape=[out_shape, jax.ShapeDtypeStruct((d, 2 * ffn), BF16), jax.ShapeDtypeStruct((ffn, d), BF16)],
        scratch_shapes=scratch,
        compiler_params=_params("parallel", "arbitrary", vmem=FFN_VMEM_LIMIT),
    )(x, gains, gains, w_gu, w_gu, w_dn, gu_all, dn_all)


def _norm_matmul_kernel(x_ref, g_ref, w_ref, o_ref, hn_ref):
    @pl.when(pl.program_id(1) == 0)
    def _():
        hn_ref[...] = _rms(x_ref[...], g_ref[...]).astype(BF16)

    o_ref[...] = jnp.dot(hn_ref[...], w_ref[...], preferred_element_type=F32)


def _norm_matmul(x, gains, gain_idx, w, layer, tn):
    m, d = x.shape
    n = w.shape[2]
    tm = min(PROJ_ROWS, m)
    assert m % tm == 0 and n % tn == 0
    return pl.pallas_call(
        _norm_matmul_kernel,
        grid=(m // tm, n // tn),
        in_specs=[
            pl.BlockSpec((tm, d), lambda i, j: (i, 0)),
            pl.BlockSpec((None, 1, d), lambda i, j: (gain_idx, 0, 0)),
            pl.BlockSpec((None, d, tn), lambda i, j: (layer, 0, j)),
        ],
        out_specs=pl.BlockSpec((tm, tn), lambda i, j: (i, j)),
        out_shape=jax.ShapeDtypeStruct((m, n), F32),
        scratch_shapes=[pltpu.VMEM((tm, d), BF16)],
        compiler_params=_params("parallel", "arbitrary"),
    )(x, gains, w)


def _norm_matmul_tail_kernel(x_ref, g_ref, w_ref, wt_ref, o_ref, ot_ref, hn_ref):
    j = pl.program_id(1)
    n_main = pl.num_programs(1) - 1

    @pl.when(j == 0)
    def _():
        hn_ref[...] = _rms(x_ref[...], g_ref[...]).astype(BF16)

    @pl.when(j < n_main)
    def _():
        o_ref[...] = jnp.dot(hn_ref[...], w_ref[...], preferred_element_type=F32)

    @pl.when(j == n_main)
    def _():
        ot_ref[...] = jnp.dot(hn_ref[...], wt_ref[...], preferred_element_type=F32)


def _norm_matmul_tail(x, gains, gain_idx, w, layer, n, w_tail, tn):
    m, d = x.shape
    nt = w_tail.shape[1]
    tm = min(PROJ_ROWS, m)
    n_main = n // tn
    assert m % tm == 0 and n % tn == 0

    def blk(j):
        return jnp.minimum(j, n_main - 1)

    return pl.pallas_call(
        _norm_matmul_tail_kernel,
        grid=(m // tm, n_main + 1),
        in_specs=[
            pl.BlockSpec((tm, d), lambda i, j: (i, 0)),
            pl.BlockSpec((None, 1, d), lambda i, j: (gain_idx, 0, 0)),
            pl.BlockSpec((None, d, tn), lambda i, j: (layer, 0, blk(j))),
            pl.BlockSpec((d, nt), lambda i, j: (0, 0)),
        ],
        out_specs=[
            pl.BlockSpec((tm, tn), lambda i, j: (i, blk(j))),
            pl.BlockSpec((tm, nt), lambda i, j: (i, 0)),
        ],
        out_shape=[jax.ShapeDtypeStruct((m, n), F32), jax.ShapeDtypeStruct((m, nt), F32)],
        scratch_shapes=[pltpu.VMEM((tm, d), BF16)],
        compiler_params=_params("parallel", "arbitrary"),
    )(x, gains, w, w_tail)


def _out_proj_kernel(x_ref, mix_ref, mem_ref, wmix_ref, wmem_ref, g_ref, o_ref):
    y = _bdot(mix_ref[...], wmix_ref[...]) + _bdot(mem_ref[...], wmem_ref[...])
    o_ref[...] = x_ref[...] + _rms(y, g_ref[...])


def _out_proj(x, mix, mem, w_out, gains, layer):
    m, d = x.shape
    wm = mix.shape[1]
    wx = mem.shape[1]
    tm = min(512, m)
    assert wm % wx == 0
    return pl.pallas_call(
        _out_proj_kernel,
        grid=(m // tm,),
        in_specs=[
            pl.BlockSpec((tm, d), lambda i: (i, 0)),
            pl.BlockSpec((tm, wm), lambda i: (i, 0)),
            pl.BlockSpec((tm, wx), lambda i: (i, 0)),
            pl.BlockSpec((None, wm, d), lambda i: (layer, 0, 0)),
            pl.BlockSpec((None, wx, d), lambda i: (layer, wm // wx, 0)),
            pl.BlockSpec((None, 1, d), lambda i: (layer * 6 + 3, 0, 0)),
        ],
        out_specs=pl.BlockSpec((tm, d), lambda i: (i, 0)),
        out_shape=jax.ShapeDtypeStruct((m, d), F32),
        compiler_params=_params("parallel"),
    )(x, mix, mem, w_out, w_out, gains)


def _unit_lower_inverse(lmat, c, span=None):
    span = c if span is None else span
    base = 4
    ri = lax.broadcasted_iota(jnp.int32, (c, c), 0)
    ci = lax.broadcasted_iota(jnp.int32, (c, c), 1)
    eye = (ri == ci).astype(F32)
    l_hi, l_lo = _split(lmat)

    def masked(keep):
        return jnp.where(keep, l_hi, jnp.zeros_like(l_hi)), jnp.where(keep, l_lo, jnp.zeros_like(l_lo))

    same = (ri // base) == (ci // base)
    dm = masked(same)
    x = eye - jnp.where(same, lmat, 0.0)
    x = x + _mm3(_split(x), _split(_mm3(dm, dm)))
    size = base
    while size < span:
        inner = ((ri // (2 * size)) == (ci // (2 * size))) & ((ri // size) != (ci // size))
        xs = _split(x)
        x = x - _mm3(_split(_mm3(xs, masked(inner))), xs)
        size *= 2
    return x


def _split(x):
    hi = x.astype(BF16)
    return hi, (x - hi.astype(F32)).astype(BF16)


def _mm3(a, b):
    def mm(p, q):
        return jnp.einsum("nij,njk->nik", p, q, preferred_element_type=F32)

    width = b[0].shape[-1]
    if a[0].shape[-1] % LANES == 0 and width % LANES == 0:
        lhs = jnp.concatenate([a[0], a[1]], axis=-1)
        rhs = jnp.concatenate([jnp.concatenate([b[0], b[1]], axis=-1),
                               jnp.concatenate([b[0], jnp.zeros_like(b[0])], axis=-1)], axis=-2)
        r = mm(lhs, rhs)
        return r[..., :width] + r[..., width:]
    return mm(a[0], b[0]) + (mm(a[0], b[1]) + mm(a[1], b[0]))


def _delta_pair_terms(q, k, v, gcol, bcol):
    n, cp, d = q.shape
    c = DELTA_CHUNK
    wu, a_intra, egc = _wy_solve(q, k, v, gcol, bcol, c)
    wu = wu.astype(BF16)
    awu = jnp.einsum("nij,nje->nie", a_intra, wu, preferred_element_type=F32)
    qt = (q * egc - awu[..., :d]).astype(BF16)
    o0 = awu[..., d:]
    k64 = k.reshape(2 * n, c, d)
    g64 = gcol.reshape(2 * n, c, 1)
    glast = g64[:, c - 1:c, :]
    k_dec = (k64 * jnp.exp(glast - g64)).astype(BF16)
    mb = jnp.einsum("ncd,nce->nde", k_dec, wu.reshape(2 * n, c, 2 * d), preferred_element_type=F32)
    return (qt.reshape(2 * n, c, d), mb[..., :d].astype(BF16), mb[..., d:], o0.reshape(2 * n, c, d),
            jnp.exp(glast))


def _wy_solve(q, k, v, gcol, bcol, span):
    rows = q.shape[1]
    ri = lax.broadcasted_iota(jnp.int32, (rows, rows), 0)
    ci = lax.broadcasted_iota(jnp.int32, (rows, rows), 1)
    eye = (ri == ci).astype(F32)
    same = (ri // span) == (ci // span)
    strict = same & (ci < ri)
    causal = same & (ci <= ri)
    grow = jnp.sum(gcol * eye, axis=1, keepdims=True)
    diff = gcol - grow
    kb = k.astype(BF16)
    kk = jnp.einsum("ncd,nmd->ncm", kb, kb, preferred_element_type=F32)
    lmat = jnp.where(strict, bcol * kk * jnp.exp(jnp.where(strict, diff, 0.0)), 0.0)
    tinv = _unit_lower_inverse(lmat, rows, span)
    egc = jnp.exp(gcol)
    rhs = jnp.concatenate([k * (bcol * egc), v * bcol], axis=-1)
    wu = _mm3(_split(tinv), _split(rhs))
    qk = jnp.einsum("ncd,nmd->ncm", q.astype(BF16), kb, preferred_element_type=F32)
    a_intra = jnp.where(causal, qk * jnp.exp(jnp.where(causal, diff, 0.0)), 0.0).astype(BF16)
    return wu, a_intra, egc


def _l2norm(x):
    return x * lax.rsqrt(jnp.sum(x * x, axis=-1, keepdims=True) + EPS)


def _block_cumsum(g, c):
    rows, lanes = g.shape
    n = rows // c
    ri = lax.broadcasted_iota(jnp.int32, (n, c, c), 1)
    ci = lax.broadcasted_iota(jnp.int32, (n, c, c), 2)
    tri = (ci <= ri).astype(BF16)
    g3 = g.reshape(n, c, lanes)
    hi = g3.astype(BF16)
    r1 = g3 - hi.astype(F32)
    mid = r1.astype(BF16)
    lo = (r1 - mid.astype(F32)).astype(BF16)

    def mm(p):
        return jnp.einsum("nij,njk->nik", tri, p, preferred_element_type=F32)

    return (mm(hi) + (mm(mid) + mm(lo))).reshape(rows, lanes)


def _shifted(x, prev, shift, row):
    xs = pltpu.roll(x, shift, x.ndim - 2)
    if x.ndim == 2:
        ps = pltpu.roll(prev, shift, 0)
        reps = x.shape[0] // SUBLANES
        ps = jnp.broadcast_to(ps[None], (reps,) + ps.shape).reshape(x.shape)
        return jnp.where(row < shift, ps, xs)
    for t in range(shift):
        src = prev.shape[1] - shift + t
        xs = jnp.where(row == t, prev[:, src:src + 1, :], xs)
    return xs


def _short_conv_silu(x, prev, w, row):
    y = x * w[CONV_W - 1:CONV_W]
    for shift in range(1, CONV_W):
        tap = CONV_W - 1 - shift
        y = y + _shifted(x, prev, shift, row) * w[tap:tap + 1]
    return _silu(y)


def _delta_prompt_kernel(q_ref, k_ref, v_ref, z_ref, ab_ref, wq_ref, wk_ref, wv_ref, alog_ref, dtb_ref,
                         og_ref, o_ref, sout_ref, s_ref, pq_ref, pk_ref, pv_ref, bq_ref, bk_ref, bv_ref, *, tt):
    t = pl.program_id(2)
    c = DELTA_CHUNK
    n = tt // c
    d = HEAD_DIM

    @pl.when(t == 0)
    def _():
        s_ref[...] = jnp.zeros_like(s_ref)
        for ref in (pq_ref, pk_ref, pv_ref, bq_ref, bk_ref, bv_ref):
            ref[...] = jnp.zeros_like(ref)

    row = lax.broadcasted_iota(jnp.int32, (tt, 1), 0)

    def conv(x_ref, p_ref, b_ref, w_ref):
        x = x_ref[...]
        w = w_ref[...]
        x1 = _shifted(x, p_ref[...], 1, row)
        b = x * w[1:2] + x1 * w[0:1]
        y = x * w[3:4] + x1 * w[2:3] + _shifted(b, b_ref[...], 2, row)
        p_ref[...] = x[tt - SUBLANES:, :]
        b_ref[...] = b[tt - SUBLANES:, :]
        return _silu(y)

    assert CONV_W == 4
    qc = conv(q_ref, pq_ref, bq_ref, wq_ref)
    kc = conv(k_ref, pk_ref, bk_ref, wk_ref)
    vc = conv(v_ref, pv_ref, bv_ref, wv_ref)

    ab = ab_ref[...]
    g_all = -jnp.exp(alog_ref[...]) * _softplus(ab + dtb_ref[...])
    beta_all = _sigmoid(ab)
    gc_all = _block_cumsum(g_all, c)

    heads = range(HEAD_GROUP)

    def pairs(cols):
        return jnp.concatenate([x.reshape(n // 2, 2 * c, x.shape[-1]) for x in cols], axis=0)

    q = pairs([_l2norm(qc[:, hh * d:(hh + 1) * d]) * (d ** -0.5) for hh in heads])
    k = pairs([_l2norm(kc[:, hh * d:(hh + 1) * d]) for hh in heads])
    v = pairs([vc[:, hh * d:(hh + 1) * d] for hh in heads])
    gcol = pairs([gc_all[:, hh:hh + 1] for hh in heads])
    bcol = pairs([beta_all[:, HEAD_GROUP + hh:HEAD_GROUP + hh + 1] for hh in heads])
    qt, m, bmat, o0, g_last = _delta_pair_terms(q, k, v, gcol, bcol)

    gain = og_ref[...]
    states = [s_ref[hh] for hh in heads]
    for ch in range(n):
        for hh in heads:
            i = hh * n + ch
            s = states[hh]
            r = jnp.dot(jnp.concatenate([qt[i], m[i]], axis=0), s.astype(BF16),
                        preferred_element_type=F32)
            o = r[:c] + o0[i]
            states[hh] = s * g_last[i] + (bmat[i] - r[c:])
            zz = z_ref[ch * c:(ch + 1) * c, hh * d:(hh + 1) * d]
            o_ref[ch * c:(ch + 1) * c, hh * d:(hh + 1) * d] = _rms(o, gain) * _silu(zz)
    for hh in range(HEAD_GROUP):
        s_ref[hh] = states[hh]
        sout_ref[hh] = states[hh]


def _delta_prompt(proj, tail, batch, seq, conv_w, alog_rows, dtb_rows, out_gain):
    tt = min(512, seq)
    nt = seq // tt
    gw = HEAD_GROUP * HEAD_DIM
    ng = N_HEAD_GROUPS
    assert seq % tt == 0 and tt % DELTA_CHUNK == 0

    def col(base):
        return pl.BlockSpec((tt, gw), lambda b, g, t: (b * nt + t, base * ng + g))

    def wcol(base):
        return pl.BlockSpec((CONV_W, gw), lambda b, g, t: (0, base * ng + g))

    ab_base = X_W // LANES
    return pl.pallas_call(
        functools.partial(_delta_prompt_kernel, tt=tt),
        grid=(batch, ng, nt),
        in_specs=[
            col(0), col(1), col(2), col(3),
            pl.BlockSpec((tt, LANES), lambda b, g, t: (b * nt + t, ab_base + g)),
            wcol(0), wcol(1), wcol(2),
            pl.BlockSpec((1, LANES), lambda b, g, t: (0, g)),
            pl.BlockSpec((1, LANES), lambda b, g, t: (0, g)),
            pl.BlockSpec((1, HEAD_DIM), lambda b, g, t: (0, 0)),
        ],
        out_specs=[
            pl.BlockSpec((tt, gw), lambda b, g, t: (b * nt + t, g)),
            pl.BlockSpec((None, HEAD_GROUP, HEAD_DIM, HEAD_DIM), lambda b, g, t: (b, g, 0, 0)),
        ],
        out_shape=[
            jax.ShapeDtypeStruct((batch * seq, DELTA_W), F32),
            jax.ShapeDtypeStruct((batch, DELTA_HEADS, HEAD_DIM, HEAD_DIM), F32),
        ],
        scratch_shapes=[
            pltpu.VMEM((HEAD_GROUP, HEAD_DIM, HEAD_DIM), F32),
        ] + [pltpu.VMEM((SUBLANES, gw), F32)] * 6,
        compiler_params=_params("parallel", "parallel", "arbitrary"),
    )(proj, proj, proj, proj, tail, conv_w, conv_w, conv_w, alog_rows, dtb_rows, out_gain)


def _delta_sample_kernel(q_ref, k_ref, v_ref, z_ref, ab_ref, bq_ref, bk_ref, bv_ref, s0_ref, wq_ref, wk_ref,
                         wv_ref, alog_ref, dtb_ref, og_ref, o_ref, sout_ref, *, bb, seq):
    d = HEAD_DIM
    rows = bb * seq
    row = lax.broadcasted_iota(jnp.int32, (1, seq, 1), 1)

    def conv(x_ref, buf_ref, w_ref):
        x = x_ref[...].reshape(bb, seq, x_ref.shape[1])
        return _short_conv_silu(x, buf_ref[...], w_ref[...], row).reshape(rows, x_ref.shape[1])

    qc = conv(q_ref, bq_ref, wq_ref)
    kc = conv(k_ref, bk_ref, wk_ref)
    vc = conv(v_ref, bv_ref, wv_ref)

    ab = ab_ref[...]
    g_all = -jnp.exp(alog_ref[...]) * _softplus(ab + dtb_ref[...])
    beta_all = _sigmoid(ab)
    gc_all = _block_cumsum(g_all, seq)

    heads = range(HEAD_GROUP)
    q = jnp.stack([_l2norm(qc[:, hh * d:(hh + 1) * d]) * (d ** -0.5) for hh in heads])
    k = jnp.stack([_l2norm(kc[:, hh * d:(hh + 1) * d]) for hh in heads])
    v = jnp.stack([vc[:, hh * d:(hh + 1) * d] for hh in heads])
    gcol = jnp.stack([gc_all[:, hh:hh + 1] for hh in heads])
    bcol = jnp.stack([beta_all[:, HEAD_GROUP + hh:HEAD_GROUP + hh + 1] for hh in heads])
    wu, a_intra, egc = _wy_solve(q, k, v, gcol, bcol, seq)
    q_dec = (q * egc).astype(BF16)
    g4 = gcol.reshape(HEAD_GROUP, bb, seq, 1)
    glast = g4[:, :, seq - 1:seq, :]
    k_dec = (k.reshape(HEAD_GROUP, bb, seq, d) * jnp.exp(glast - g4)).astype(BF16)
    g_last = jnp.exp(glast)

    gain = og_ref[...]
    for hh in heads:
        s = s0_ref[:, hh]
        w = wu[hh, :, :d].astype(BF16).reshape(bb, seq, d)
        u = wu[hh, :, d:].reshape(bb, seq, d)
        wq = jnp.concatenate([w, q_dec[hh].reshape(bb, seq, d)], axis=1)
        r = jnp.einsum("bcd,bde->bce", wq, s.astype(BF16), preferred_element_type=F32)
        v_new = u - r[:, :seq]
        vb = v_new.astype(BF16)
        o = r[:, seq:].reshape(rows, d) + jnp.dot(a_intra[hh], vb.reshape(rows, d), preferred_element_type=F32)
        sout_ref[:, hh] = s * g_last[hh] + jnp.einsum("bcd,bce->bde", k_dec[hh], vb,
                                                      preferred_element_type=F32)
        o_ref[:, hh * d:(hh + 1) * d] = _rms(o, gain) * _silu(z_ref[:, hh * d:(hh + 1) * d])


def _delta_sample_into_kernel(*refs, bb, seq):
    _delta_sample_kernel(*refs[:15], *refs[16:], bb=bb, seq=seq)


def _delta_sample_first_kernel(*refs, bb, seq, layer):
    sout_ref = refs[-1]
    for other in range(sout_ref.shape[0]):
        if other != layer:
            sout_ref[other] = jnp.zeros(sout_ref.shape[1:], F32)
    _delta_sample_kernel(*refs[:-1], sout_ref.at[layer], bb=bb, seq=seq)


def _delta_sample(proj, tail, batch, seq, conv_bufs, states, j, conv_w, alog_rows, dtb_rows, out_gain, new_states):
    bb = LANES // seq
    assert batch % bb == 0 and seq == SUBLANES
    rows = bb * seq
    gw = HEAD_GROUP * HEAD_DIM
    ng = N_HEAD_GROUPS

    def col(base):
        return pl.BlockSpec((rows, gw), lambda i, g: (i, base * ng + g))

    def bufcol(base):
        return pl.BlockSpec((None, bb, CONV_W - 1, gw), lambda i, g: (j, i, 0, base * ng + g))

    def wcol(base):
        return pl.BlockSpec((CONV_W, gw), lambda i, g: (0, base * ng + g))

    ab_base = X_W // LANES
    state_spec = pl.BlockSpec((None, bb, HEAD_GROUP, HEAD_DIM, HEAD_DIM), lambda i, g: (j, i, g, 0, 0))
    in_specs = [
        col(0), col(1), col(2), col(3),
        pl.BlockSpec((rows, LANES), lambda i, g: (i, ab_base + g)),
        bufcol(0), bufcol(1), bufcol(2),
        state_spec,
        wcol(0), wcol(1), wcol(2),
        pl.BlockSpec((1, LANES), lambda i, g: (0, g)),
        pl.BlockSpec((1, LANES), lambda i, g: (0, g)),
        pl.BlockSpec((1, HEAD_DIM), lambda i, g: (0, 0)),
    ]
    args = (proj, proj, proj, proj, tail, conv_bufs, conv_bufs, conv_bufs, states, conv_w, conv_w, conv_w,
            alog_rows, dtb_rows, out_gain)
    if new_states is None:
        body, aliases = functools.partial(_delta_sample_first_kernel, bb=bb, seq=seq, layer=j), {}
        out_state_spec = pl.BlockSpec((states.shape[0], bb, HEAD_GROUP, HEAD_DIM, HEAD_DIM),
                                      lambda i, g: (0, i, g, 0, 0))
    else:
        in_specs.append(pl.BlockSpec(memory_space=pl.ANY))
        args += (new_states,)
        body, aliases = functools.partial(_delta_sample_into_kernel, bb=bb, seq=seq), {len(args) - 1: 1}
        out_state_spec = state_spec
    return pl.pallas_call(
        body,
        grid=(batch // bb, ng),
        in_specs=in_specs,
        out_specs=[
            pl.BlockSpec((rows, gw), lambda i, g: (i, g)),
            out_state_spec,
        ],
        out_shape=[
            jax.ShapeDtypeStruct((batch * seq, DELTA_W), F32),
            jax.ShapeDtypeStruct(states.shape, F32),
        ],
        input_output_aliases=aliases,
        compiler_params=_params("parallel", "parallel"),
    )(*args)


def _gelu(x):
    return 0.5 * x * (1.0 + lax.erf(x * (2.0 ** -0.5)))


def _layernorm(x, gain, bias):
    mu = jnp.mean(x, axis=-1, keepdims=True)
    xc = x - mu
    var = jnp.mean(xc * xc, axis=-1, keepdims=True)
    return xc * lax.rsqrt(var + EPS) * gain + bias


def _gmlp_prompt_kernel(u_ref, v_ref, lng_ref, lnb_ref, ws_ref, bs_ref, o_ref, *, chunks):
    c = GMLP_CHUNK
    d = HEAD_DIM
    ri = lax.broadcasted_iota(jnp.int32, (c, c), 0)
    ci = lax.broadcasted_iota(jnp.int32, (c, c), 1)
    lower = ci <= ri
    v = _layernorm(_gelu(v_ref[...]), lng_ref[...], lnb_ref[...]).astype(BF16)
    bs = bs_ref[...]
    for g in range(GMLP_GROUPS):
        w = jnp.where(lower, ws_ref[g], 0.0).astype(BF16)
        bias = bs[:, g:g + 1]
        for ch in range(chunks):
            rs = slice(ch * c, (ch + 1) * c)
            cs = slice(g * d, (g + 1) * d)
            mixed = jnp.dot(w, v[rs, cs], preferred_element_type=F32) + bias
            o_ref[rs, cs] = _gelu(u_ref[rs, cs]) * mixed


def _gmlp_prompt(proj, ln_gain, ln_bias, w_s, b_s_t):
    m = proj.shape[0]
    chunks = 2
    tm = chunks * GMLP_CHUNK
    assert m % tm == 0
    return pl.pallas_call(
        functools.partial(_gmlp_prompt_kernel, chunks=chunks),
        grid=(m // tm,),
        in_specs=[
            pl.BlockSpec((tm, GMLP_W), lambda i: (i, 0)),
            pl.BlockSpec((tm, GMLP_W), lambda i: (i, 1)),
            pl.BlockSpec((1, GMLP_W), lambda i: (0, 0)),
            pl.BlockSpec((1, GMLP_W), lambda i: (0, 0)),
            pl.BlockSpec((GMLP_GROUPS, GMLP_CHUNK, GMLP_CHUNK), lambda i: (0, 0, 0)),
            pl.BlockSpec((GMLP_CHUNK, GMLP_GROUPS), lambda i: (0, 0)),
        ],
        out_specs=pl.BlockSpec((tm, GMLP_W), lambda i: (i, 0)),
        out_shape=jax.ShapeDtypeStruct((m, GMLP_W), F32),
        compiler_params=_params("parallel"),
    )(proj, proj, ln_gain, ln_bias, w_s, b_s_t)


def _gmlp_sample_kernel(u_ref, v_ref, lng_ref, lnb_ref, wx_ref, bx_ref, o_ref, vout_ref, *, bb, seq):
    v = _layernorm(_gelu(v_ref[...]), lng_ref[...], lnb_ref[...])
    vout_ref[...] = v
    v3 = v.reshape(bb, seq, GMLP_W)
    row = lax.broadcasted_iota(jnp.int32, (seq, 1), 0)
    mixed = jnp.broadcast_to(bx_ref[...][None], (bb, seq, GMLP_W))
    for j in range(seq):
        wj = jnp.where(row >= j, wx_ref[j], 0.0)
        mixed = mixed + wj[None] * v3[:, j:j + 1, :]
    o_ref[...] = _gelu(u_ref[...]) * mixed.reshape(bb * seq, GMLP_W)


def _gmlp_sample(proj, batch, seq, ln_gain, ln_bias, w_exp, b_exp):
    bb = 32
    assert batch % bb == 0 and seq == SUBLANES
    rows = bb * seq
    return pl.pallas_call(
        functools.partial(_gmlp_sample_kernel, bb=bb, seq=seq),
        grid=(batch // bb,),
        in_specs=[
            pl.BlockSpec((rows, GMLP_W), lambda i: (i, 0)),
            pl.BlockSpec((rows, GMLP_W), lambda i: (i, 1)),
            pl.BlockSpec((1, GMLP_W), lambda i: (0, 0)),
            pl.BlockSpec((1, GMLP_W), lambda i: (0, 0)),
            pl.BlockSpec((seq, seq, GMLP_W), lambda i: (0, 0, 0)),
            pl.BlockSpec((seq, GMLP_W), lambda i: (0, 0)),
        ],
        out_specs=[
            pl.BlockSpec((rows, GMLP_W), lambda i: (i, 0)),
            pl.BlockSpec((rows, GMLP_W), lambda i: (i, 0)),
        ],
        out_shape=[
            jax.ShapeDtypeStruct((batch * seq, GMLP_W), F32),
            jax.ShapeDtypeStruct((batch * seq, GMLP_W), F32),
        ],
        compiler_params=_params("parallel"),
    )(proj, proj, ln_gain, ln_bias, w_exp, b_exp)


def _softmax(s):
    m = jnp.max(s, axis=-1, keepdims=True)
    e = jnp.exp(s - m)
    return e / jnp.sum(e, axis=-1, keepdims=True)


def _mem_attn_prompt_kernel(q_ref, k_ref, v_ref, o_ref):
    d = HEAD_DIM
    for h in range(X_HEADS):
        sl = slice(h * d, (h + 1) * d)
        s = lax.dot_general(q_ref[:, sl].astype(BF16), k_ref[:, sl].astype(BF16),
                            (((1,), (1,)), ((), ())), preferred_element_type=F32) * (d ** -0.5)
        o_ref[:, sl] = _bdot(_softmax(s), v_ref[:, sl])


def _mem_attn_prompt(proj, q_block, batch, seq, mem_k, mem_v):
    tt = min(512, seq)
    nt = seq // tt
    n_mem = mem_k.shape[1]
    return pl.pallas_call(
        _mem_attn_prompt_kernel,
        grid=(batch, nt),
        in_specs=[
            pl.BlockSpec((tt, X_W), lambda b, t: (b * nt + t, q_block)),
            pl.BlockSpec((None, n_mem, X_W), lambda b, t: (b, 0, 0)),
            pl.BlockSpec((None, n_mem, X_W), lambda b, t: (b, 0, 0)),
        ],
        out_specs=pl.BlockSpec((tt, X_W), lambda b, t: (b * nt + t, 0)),
        out_shape=jax.ShapeDtypeStruct((batch * seq, X_W), F32),
        compiler_params=_params("parallel", "parallel"),
    )(proj, mem_k, mem_v)


def _mem_attn_sample_kernel(q_ref, k_ref, v_ref, o_ref, *, bb, seq):
    d = HEAD_DIM
    nq = X_HEADS * seq
    nk = k_ref.shape[1]
    q = jnp.concatenate([q_ref[:, h * d:(h + 1) * d].reshape(bb, seq, d) for h in range(X_HEADS)], axis=1)
    s = jnp.einsum("bqd,bkd->bqk", q.astype(BF16), k_ref[...].astype(BF16),
                   preferred_element_type=F32) * (d ** -0.5)
    q_head = lax.broadcasted_iota(jnp.int32, (nq, nk), 0) // seq
    k_head = lax.broadcasted_iota(jnp.int32, (nq, nk), 1) % X_HEADS
    p = _softmax(jnp.where(q_head == k_head, s, -1e30)).astype(BF16)
    o = jnp.einsum("bqk,bkd->bqd", p, v_ref[...].astype(BF16), preferred_element_type=F32)
    for h in range(X_HEADS):
        o_ref[:, h * d:(h + 1) * d] = o[:, h * seq:(h + 1) * seq, :].reshape(bb * seq, d)


def _mem_attn_sample(proj, q_block, batch, seq, mem_k, mem_v, layer):
    bb = 8
    rows = bb * seq
    n_rows = mem_k.shape[2]
    return pl.pallas_call(
        functools.partial(_mem_attn_sample_kernel, bb=bb, seq=seq),
        grid=(batch // bb,),
        in_specs=[
            pl.BlockSpec((rows, X_W), lambda i: (i, q_block)),
            pl.BlockSpec((None, bb, n_rows, HEAD_DIM), lambda i: (layer, i, 0, 0)),
            pl.BlockSpec((None, bb, n_rows, HEAD_DIM), lambda i: (layer, i, 0, 0)),
        ],
        out_specs=pl.BlockSpec((rows, X_W), lambda i: (i, 0)),
        out_shape=jax.ShapeDtypeStruct((batch * seq, X_W), F32),
        compiler_params=_params("parallel"),
    )(proj, mem_k, mem_v)


def _delta_tail_weight(w_in_a):
    d_model = w_in_a.shape[0]
    a = w_in_a[:, 4 * DELTA_W:4 * DELTA_W + DELTA_HEADS]
    b = w_in_a[:, 4 * DELTA_W + DELTA_HEADS:4 * DELTA_W + 2 * DELTA_HEADS]
    parts = [w_in_a[:, 4 * DELTA_W + 2 * DELTA_HEADS:]]
    pad = jnp.zeros((d_model, LANES - 2 * HEAD_GROUP), w_in_a.dtype)
    for g in range(N_HEAD_GROUPS):
        hs = slice(g * HEAD_GROUP, (g + 1) * HEAD_GROUP)
        parts += [a[:, hs], b[:, hs], pad]
    parts.append(jnp.zeros((d_model, AB_W - N_HEAD_GROUPS * LANES), w_in_a.dtype))
    return jnp.concatenate(parts, axis=1).astype(BF16)


def _gate_rows(vec):
    row = jnp.zeros((N_HEAD_GROUPS, LANES), F32)
    row = row.at[:, :HEAD_GROUP].set(vec.astype(F32).reshape(N_HEAD_GROUPS, HEAD_GROUP))
    return jnp.concatenate([row.reshape(1, -1), jnp.zeros((1, AB_W - N_HEAD_GROUPS * LANES), F32)], axis=1)


def _ffn_step(x, p, layer, slot, pre, post, round_next):
    w_gu, w_dn = p["ffn_bf16"][(layer, slot)]
    nxt = (layer, 1) if slot == 0 else (layer + 1, 0)
    if round_next and nxt[0] < p["w_out"].shape[0]:
        x, gu_next, dn_next = _ffn(x, p["gains"], layer * 6, pre, post, w_gu, w_dn,
                                   cast_next=(p["w_ffn_gu"], p["w_ffn_dn"]) + nxt)
        p["ffn_bf16"][nxt] = (gu_next, dn_next)
        return x
    return _ffn(x, p["gains"], layer * 6, pre, post, w_gu, w_dn)


def _trunk(x, batch, seq, is_prompt, mem_k, mem_v, conv_bufs, delta_states, p):
    new_conv, new_delta, new_v = [], [], []
    gains = p["gains"]
    depth = p["w_out"].shape[0]
    for i in range(depth):
        j = i // N_MIXERS
        x = _ffn_step(x, p, i, 0, 0, 1, is_prompt)
        if i % N_MIXERS == 0:
            proj, q_src = _norm_matmul_tail(x, gains, i * 6 + 2, p["w_in_a"], j, 4 * DELTA_W, p["w_in_a_tail"][j],
                                            PROJ_COLS_A)
            q_block = 0
            if is_prompt:
                mix, s_new = _delta_prompt(proj, q_src, batch, seq, p["conv_w"][j], p["alog_rows"][j],
                                           p["dtb_rows"][j], p["delta_norm_gain"][j])
                new_delta.append(s_new)
            else:
                mix, new_delta = _delta_sample(proj, q_src, batch, seq, conv_bufs, delta_states, j, p["conv_w"][j],
                                               p["alog_rows"][j], p["dtb_rows"][j], p["delta_norm_gain"][j],
                                               new_delta if j > 0 else None)
            new_conv.append(proj.reshape(batch, seq, -1)[:, seq - (CONV_W - 1):, :3 * DELTA_W])
        else:
            proj = q_src = _norm_matmul(x, gains, i * 6 + 2, p["w_in_b"], j, PROJ_COLS)
            q_block = 2 * GMLP_W // X_W
            if is_prompt:
                mix = _gmlp_prompt(proj, p["gmlp_ln_gain"][j], p["gmlp_ln_bias"][j], p["w_spatial"][j],
                                   p["b_spatial_t"][j])
            else:
                mix, v_rows = _gmlp_sample(proj, batch, seq, p["gmlp_ln_gain"][j], p["gmlp_ln_bias"][j],
                                           p["w_spatial_exp"][j], p["b_spatial_exp"][j])
                new_v.append(v_rows.reshape(batch, seq, GMLP_W))
        if is_prompt:
            mem_out = _mem_attn_prompt(q_src, q_block, batch, seq, mem_k[i], mem_v[i])
        else:
            mem_out = _mem_attn_sample(q_src, q_block, batch, seq, mem_k, mem_v, i)
        x = _out_proj(x, mix, mem_out, p["w_out"], gains, i)
        x = _ffn_step(x, p, i, 1, 4, 5, is_prompt)
    return x, new_conv, new_delta, new_v


def kernel(x_prompt, x_sample, mem_prompt, cache_mem_k, cache_mem_v, state_delta, state_conv, norm_gains, w_ffn_gu, w_ffn_dn, w_in_a, conv_w, a_log, dt_bias, delta_norm_gain, w_in_b, gmlp_ln_gain, gmlp_ln_bias, w_spatial, b_spatial, mem_norm_gain, w_mem_kv, w_out):
    batch, seq, d_model = x_prompt.shape
    dec_batch, dec_seq, _ = x_sample.shape
    depth = w_out.shape[0]
    n_mem = mem_prompt.shape[1]
    n_a = w_in_a.shape[0]
    n_b = w_in_b.shape[0]

    w_s_dec = w_spatial[:, :, :dec_seq, :dec_seq]
    p = {
        "gains": norm_gains.reshape(depth * 6, 1, d_model),
        "w_ffn_gu": w_ffn_gu,
        "w_ffn_dn": w_ffn_dn,
        "ffn_bf16": {(0, 0): (w_ffn_gu[0, 0].astype(BF16), w_ffn_dn[0, 0].astype(BF16))},
        "w_in_a": w_in_a.astype(BF16),
        "w_in_a_tail": [_delta_tail_weight(w_in_a[j]) for j in range(n_a)],
        "w_in_b": w_in_b.astype(BF16),
        "w_out": w_out.astype(BF16),
        "conv_w": conv_w,
        "alog_rows": [_gate_rows(a_log[j]) for j in range(n_a)],
        "dtb_rows": [_gate_rows(dt_bias[j]) for j in range(n_a)],
        "delta_norm_gain": delta_norm_gain.reshape(n_a, 1, HEAD_DIM),
        "gmlp_ln_gain": gmlp_ln_gain.reshape(n_b, 1, GMLP_W),
        "gmlp_ln_bias": gmlp_ln_bias.reshape(n_b, 1, GMLP_W),
        "w_spatial": w_spatial,
        "b_spatial_t": jnp.swapaxes(b_spatial, 1, 2),
        "w_spatial_exp": jnp.repeat(jnp.transpose(w_s_dec, (0, 3, 2, 1)), HEAD_DIM, axis=-1),
        "b_spatial_exp": jnp.repeat(jnp.swapaxes(b_spatial[:, :, :dec_seq], 1, 2), HEAD_DIM, axis=-1),
    }

    mem2d = mem_prompt.reshape(batch * n_mem, d_model)
    mem_gains = mem_norm_gain.reshape(depth, 1, d_model)
    w_kv = w_mem_kv.astype(BF16)
    kvs = [_norm_matmul(mem2d, mem_gains, i, w_kv, i, 1024) for i in range(depth)]
    mem_k_prompt = jnp.stack([kv[:, :X_W].reshape(batch, n_mem, X_W) for kv in kvs])
    mem_v_prompt = jnp.stack([kv[:, X_W:].reshape(batch, n_mem, X_W) for kv in kvs])

    y_prompt, conv_p, delta_p, _ = _trunk(x_prompt.reshape(batch * seq, d_model), batch, seq, True,
                                          mem_k_prompt, mem_v_prompt, None, None, p)

    y_sample, conv_s, delta_s, v_s = _trunk(
        x_sample.reshape(dec_batch * dec_seq, d_model), dec_batch, dec_seq, False,
        cache_mem_k.reshape(depth, dec_batch, n_mem * X_HEADS, HEAD_DIM),
        cache_mem_v.reshape(depth, dec_batch, n_mem * X_HEADS, HEAD_DIM),
        state_conv, state_delta, p)

    return (y_prompt.reshape(batch, seq, d_model),
            y_sample.reshape(dec_batch, dec_seq, d_model),
            mem_k_prompt.reshape(depth, batch, n_mem, X_HEADS, HEAD_DIM),
            mem_v_prompt.reshape(depth, batch, n_mem, X_HEADS, HEAD_DIM),
            jnp.stack(delta_p), jnp.stack(conv_p), delta_s, jnp.stack(conv_s), jnp.stack(v_s))
```

```python
import functools

import jax
import jax.numpy as jnp
from jax import lax
from jax.experimental import pallas as pl
from jax.experimental.pallas import tpu as pltpu

F32 = jnp.float32
BF16 = jnp.bfloat16

EPS = 1e-6
HEAD_DIM = 128
X_HEADS = 4
X_W = X_HEADS * HEAD_DIM
DELTA_HEADS = 12
DELTA_W = DELTA_HEADS * HEAD_DIM
CONV_W = 4
DELTA_CHUNK = 64
GMLP_GROUPS = 12
GMLP_W = GMLP_GROUPS * HEAD_DIM
GMLP_CHUNK = 128
N_MIXERS = 2

LANES = 128
SUBLANES = 8
VMEM_LIMIT = 56 * 1024 * 1024
FFN_VMEM_LIMIT = 62 * 1024 * 1024

HEAD_GROUP = 6
N_HEAD_GROUPS = DELTA_HEADS // HEAD_GROUP
FFN_ROWS = 1024
FFN_COLS = 512
PROJ_ROWS = 1024
PROJ_COLS = 1792
PROJ_COLS_A = 1024
AB_W = 512
TAIL_W = X_W + AB_W


def _params(*sem, vmem=VMEM_LIMIT):
    return pltpu.CompilerParams(dimension_semantics=sem, vmem_limit_bytes=vmem)


def _rms(x, gain):
    ms = jnp.mean(x * x, axis=-1, keepdims=True)
    return x * lax.rsqrt(ms + EPS) * gain


def _sigmoid(x):
    return 0.5 * jnp.tanh(0.5 * x) + 0.5


def _silu(x):
    return x * _sigmoid(x)


def _softplus(x):
    return jnp.maximum(x, 0.0) + jnp.log1p(jnp.exp(-jnp.abs(x)))


def _bdot(a, b):
    return jnp.dot(a.astype(BF16), b.astype(BF16), preferred_element_type=F32)


def _ffn_cast_kernel(x_ref, gpre_ref, gpost_ref, wg_ref, wu_ref, wd_ref, cgu_ref, cdn_ref,
                     o_ref, ogu_ref, odn_ref, hn_ref, act_ref):
    ogu_ref[...] = cgu_ref[...].astype(BF16)
    odn_ref[...] = cdn_ref[...].astype(BF16)
    _ffn_kernel(x_ref, gpre_ref, gpost_ref, wg_ref, wu_ref, wd_ref, o_ref, hn_ref, act_ref)


def _ffn_kernel(x_ref, gpre_ref, gpost_ref, wg_ref, wu_ref, wd_ref, o_ref, hn_ref, act_ref):
    f = pl.program_id(1)
    last = pl.num_programs(1) - 1

    def gate_up(slot):
        h = hn_ref[...]
        gate = jnp.dot(h, wg_ref[...], preferred_element_type=F32)
        up = jnp.dot(h, wu_ref[...], preferred_element_type=F32)
        act_ref[slot] = (_silu(gate) * up).astype(BF16)

    def down(slot):
        o_ref[...] += jnp.dot(act_ref[slot], wd_ref[...], preferred_element_type=F32)

    @pl.when(f == 0)
    def _():
        hn_ref[...] = _rms(x_ref[...], gpre_ref[...]).astype(BF16)
        o_ref[...] = jnp.zeros_like(o_ref)
        gate_up(0)

    @pl.when((f > 0) & (f < last))
    def _():
        down((f - 1) % 2)
        gate_up(f % 2)

    @pl.when(f == last)
    def _():
        down((f - 1) % 2)
        o_ref[...] = x_ref[...] + 0.5 * _rms(o_ref[...], gpost_ref[...])


def _ffn(x, gains, gain_base, pre, post, w_gu, w_dn, cast_next=None):
    m, d = x.shape
    ffn = w_dn.shape[0]
    tm = min(FFN_ROWS, m)
    tf = FFN_COLS
    nf = ffn // tf
    nm = m // tm
    assert m % tm == 0 and ffn % tf == 0
    def up_blk(f):
        return jnp.minimum(f, nf - 1)

    def dn_blk(f):
        return jnp.maximum(f - 1, 0)

    in_specs = [
        pl.BlockSpec((tm, d), lambda i, f: (i, 0)),
        pl.BlockSpec((None, 1, d), lambda i, f: (gain_base + pre, 0, 0)),
        pl.BlockSpec((None, 1, d), lambda i, f: (gain_base + post, 0, 0)),
        pl.BlockSpec((d, tf), lambda i, f: (0, up_blk(f))),
        pl.BlockSpec((d, tf), lambda i, f: (0, nf + up_blk(f))),
        pl.BlockSpec((tf, d), lambda i, f: (dn_blk(f), 0)),
    ]
    out_specs = pl.BlockSpec((tm, d), lambda i, f: (i, 0))
    out_shape = jax.ShapeDtypeStruct((m, d), F32)
    scratch = [pltpu.VMEM((tm, d), BF16), pltpu.VMEM((2, tm, tf), BF16)]
    if cast_next is None:
        return pl.pallas_call(
            _ffn_kernel, grid=(nm, nf + 1), in_specs=in_specs, out_specs=out_specs, out_shape=out_shape,
            scratch_shapes=scratch, compiler_params=_params("parallel", "arbitrary", vmem=FFN_VMEM_LIMIT),
        )(x, gains, gains, w_gu, w_gu, w_dn)
    gu_all, dn_all, layer, slot = cast_next
    gu_rows, gu_cols = d // nm, 2 * ffn // nf
    dn_rows = ffn // (nm * nf)
    assert gu_rows * nm == d and gu_cols * nf == 2 * ffn and dn_rows * nm * nf == ffn
    assert gu_rows % 16 == 0 and gu_cols % LANES == 0 and dn_rows % 16 == 0
    return pl.pallas_call(
        _ffn_cast_kernel,
        grid=(nm, nf + 1),
        in_specs=in_specs + [
            pl.BlockSpec((None, None, gu_rows, gu_cols), lambda i, f: (layer, slot, i, up_blk(f))),
            pl.BlockSpec((None, None, dn_rows, d), lambda i, f: (layer, slot, i * nf + up_blk(f), 0)),
        ],
        out_specs=[
            out_specs,
            pl.BlockSpec((gu_rows, gu_cols), lambda i, f: (i, up_blk(f))),
            pl.BlockSpec((dn_rows, d), lambda i, f: (i * nf + up_blk(f), 0)),
        ],
        out_shape=[out_shape, jax.ShapeDtypeStruct((d, 2 * ffn), BF16), jax.ShapeDtypeStruct((ffn, d), BF16)],
        scratch_shapes=scratch,
        compiler_params=_params("parallel", "arbitrary", vmem=FFN_VMEM_LIMIT),
    )(x, gains, gains, w_gu, w_gu, w_dn, gu_all, dn_all)


def _norm_matmul_kernel(x_ref, g_ref, w_ref, o_ref, hn_ref):
    @pl.when(pl.program_id(1) == 0)
    def _():
        hn_ref[...] = _rms(x_ref[...], g_ref[...]).astype(BF16)

    o_ref[...] = jnp.dot(hn_ref[...], w_ref[...], preferred_element_type=F32)


def _norm_matmul(x, gains, gain_idx, w, layer, tn):
    m, d = x.shape
    n = w.shape[2]
    tm = min(PROJ_ROWS, m)
    assert m % tm == 0 and n % tn == 0
    return pl.pallas_call(
        _norm_matmul_kernel,
        grid=(m // tm, n // tn),
        in_specs=[
            pl.BlockSpec((tm, d), lambda i, j: (i, 0)),
            pl.BlockSpec((None, 1, d), lambda i, j: (gain_idx, 0, 0)),
            pl.BlockSpec((None, d, tn), lambda i, j: (layer, 0, j)),
        ],
        out_specs=pl.BlockSpec((tm, tn), lambda i, j: (i, j)),
        out_shape=jax.ShapeDtypeStruct((m, n), F32),
        scratch_shapes=[pltpu.VMEM((tm, d), BF16)],
        compiler_params=_params("parallel", "arbitrary"),
    )(x, gains, w)


def _norm_matmul_tail_kernel(x_ref, g_ref, w_ref, wt_ref, o_ref, ot_ref, hn_ref):
    j = pl.program_id(1)
    n_main = pl.num_programs(1) - 1

    @pl.when(j == 0)
    def _():
        hn_ref[...] = _rms(x_ref[...], g_ref[...]).astype(BF16)

    @pl.when(j < n_main)
    def _():
        o_ref[...] = jnp.dot(hn_ref[...], w_ref[...], preferred_element_type=F32)

    @pl.when(j == n_main)
    def _():
        ot_ref[...] = jnp.dot(hn_ref[...], wt_ref[...], preferred_element_type=F32)


def _norm_matmul_tail(x, gains, gain_idx, w, layer, n, w_tail, tn):
    m, d = x.shape
    nt = w_tail.shape[1]
    tm = min(PROJ_ROWS, m)
    n_main = n // tn
    assert m % tm == 0 and n % tn == 0

    def blk(j):
        return jnp.minimum(j, n_main - 1)

    return pl.pallas_call(
        _norm_matmul_tail_kernel,
        grid=(m // tm, n_main + 1),
        in_specs=[
            pl.BlockSpec((tm, d), lambda i, j: (i, 0)),
            pl.BlockSpec((None, 1, d), lambda i, j: (gain_idx, 0, 0)),
            pl.BlockSpec((None, d, tn), lambda i, j: (layer, 0, blk(j))),
            pl.BlockSpec((d, nt), lambda i, j: (0, 0)),
        ],
        out_specs=[
            pl.BlockSpec((tm, tn), lambda i, j: (i, blk(j))),
            pl.BlockSpec((tm, nt), lambda i, j: (i, 0)),
        ],
        out_shape=[jax.ShapeDtypeStruct((m, n), F32), jax.ShapeDtypeStruct((m, nt), F32)],
        scratch_shapes=[pltpu.VMEM((tm, d), BF16)],
        compiler_params=_params("parallel", "arbitrary"),
    )(x, gains, w, w_tail)


def _out_proj_kernel(x_ref, mix_ref, mem_ref, wmix_ref, wmem_ref, g_ref, o_ref):
    y = _bdot(mix_ref[...], wmix_ref[...]) + _bdot(mem_ref[...], wmem_ref[...])
    o_ref[...] = x_ref[...] + _rms(y, g_ref[...])


def _out_proj(x, mix, mem, w_out, gains, layer):
    m, d = x.shape
    wm = mix.shape[1]
    wx = mem.shape[1]
    tm = min(512, m)
    assert wm % wx == 0
    return pl.pallas_call(
        _out_proj_kernel,
        grid=(m // tm,),
        in_specs=[
            pl.BlockSpec((tm, d), lambda i: (i, 0)),
            pl.BlockSpec((tm, wm), lambda i: (i, 0)),
            pl.BlockSpec((tm, wx), lambda i: (i, 0)),
            pl.BlockSpec((None, wm, d), lambda i: (layer, 0, 0)),
            pl.BlockSpec((None, wx, d), lambda i: (layer, wm // wx, 0)),
            pl.BlockSpec((None, 1, d), lambda i: (layer * 6 + 3, 0, 0)),
        ],
        out_specs=pl.BlockSpec((tm, d), lambda i: (i, 0)),
        out_shape=jax.ShapeDtypeStruct((m, d), F32),
        compiler_params=_params("parallel"),
    )(x, mix, mem, w_out, w_out, gains)


def _unit_lower_inverse(lmat, c, span=None):
    span = c if span is None else span
    base = 4
    ri = lax.broadcasted_iota(jnp.int32, (c, c), 0)
    ci = lax.broadcasted_iota(jnp.int32, (c, c), 1)
    eye = (ri == ci).astype(F32)
    l_hi, l_lo = _split(lmat)

    def masked(keep):
        return jnp.where(keep, l_hi, jnp.zeros_like(l_hi)), jnp.where(keep, l_lo, jnp.zeros_like(l_lo))

    same = (ri // base) == (ci // base)
    dm = masked(same)
    x = eye - jnp.where(same, lmat, 0.0)
    x = x + _mm3(_split(x), _split(_mm3(dm, dm)))
    size = base
    while size < span:
        inner = ((ri // (2 * size)) == (ci // (2 * size))) & ((ri // size) != (ci // size))
        xs = _split(x)
        x = x - _mm3(_split(_mm3(xs, masked(inner))), xs)
        size *= 2
    return x


def _split(x):
    hi = x.astype(BF16)
    return hi, (x - hi.astype(F32)).astype(BF16)


def _mm3(a, b):
    def mm(p, q):
        return jnp.einsum("nij,njk->nik", p, q, preferred_element_type=F32)

    width = b[0].shape[-1]
    if a[0].shape[-1] % LANES == 0 and width % LANES == 0:
        lhs = jnp.concatenate([a[0], a[1]], axis=-1)
        rhs = jnp.concatenate([jnp.concatenate([b[0], b[1]], axis=-1),
                               jnp.concatenate([b[0], jnp.zeros_like(b[0])], axis=-1)], axis=-2)
        r = mm(lhs, rhs)
        return r[..., :width] + r[..., width:]
    return mm(a[0], b[0]) + (mm(a[0], b[1]) + mm(a[1], b[0]))


def _delta_pair_terms(q, k, v, gcol, bcol):
    n, cp, d = q.shape
    c = DELTA_CHUNK
    wu, a_intra, egc = _wy_solve(q, k, v, gcol, bcol, c)
    wu = wu.astype(BF16)
    awu = jnp.einsum("nij,nje->nie", a_intra, wu, preferred_element_type=F32)
    qt = (q * egc - awu[..., :d]).astype(BF16)
    o0 = awu[..., d:]
    k64 = k.reshape(2 * n, c, d)
    g64 = gcol.reshape(2 * n, c, 1)
    glast = g64[:, c - 1:c, :]
    k_dec = (k64 * jnp.exp(glast - g64)).astype(BF16)
    mb = jnp.einsum("ncd,nce->nde", k_dec, wu.reshape(2 * n, c, 2 * d), preferred_element_type=F32)
    return (qt.reshape(2 * n, c, d), mb[..., :d].astype(BF16), mb[..., d:], o0.reshape(2 * n, c, d),
            jnp.exp(glast))


def _wy_solve(q, k, v, gcol, bcol, span):
    rows = q.shape[1]
    ri = lax.broadcasted_iota(jnp.int32, (rows, rows), 0)
    ci = lax.broadcasted_iota(jnp.int32, (rows, rows), 1)
    eye = (ri == ci).astype(F32)
    same = (ri // span) == (ci // span)
    strict = same & (ci < ri)
    causal = same & (ci <= ri)
    grow = jnp.sum(gcol * eye, axis=1, keepdims=True)
    diff = gcol - grow
    kb = k.astype(BF16)
    kk = jnp.einsum("ncd,nmd->ncm", kb, kb, preferred_element_type=F32)
    lmat = jnp.where(strict, bcol * kk * jnp.exp(jnp.where(strict, diff, 0.0)), 0.0)
    tinv = _unit_lower_inverse(lmat, rows, span)
    egc = jnp.exp(gcol)
    rhs = jnp.concatenate([k * (bcol * egc), v * bcol], axis=-1)
    wu = _mm3(_split(tinv), _split(rhs))
    qk = jnp.einsum("ncd,nmd->ncm", q.astype(BF16), kb, preferred_element_type=F32)
    a_intra = jnp.where(causal, qk * jnp.exp(jnp.where(causal, diff, 0.0)), 0.0).astype(BF16)
    return wu, a_intra, egc


def _l2norm(x):
    return x * lax.rsqrt(jnp.sum(x * x, axis=-1, keepdims=True) + EPS)


def _block_cumsum(g, c):
    rows, lanes = g.shape
    n = rows // c
    ri = lax.broadcasted_iota(jnp.int32, (n, c, c), 1)
    ci = lax.broadcasted_iota(jnp.int32, (n, c, c), 2)
    tri = (ci <= ri).astype(BF16)
    g3 = g.reshape(n, c, lanes)
    hi = g3.astype(BF16)
    r1 = g3 - hi.astype(F32)
    mid = r1.astype(BF16)
    lo = (r1 - mid.astype(F32)).astype(BF16)

    def mm(p):
        return jnp.einsum("nij,njk->nik", tri, p, preferred_element_type=F32)

    return (mm(hi) + (mm(mid) + mm(lo))).reshape(rows, lanes)


def _shifted(x, prev, shift, row):
    xs = pltpu.roll(x, shift, x.ndim - 2)
    if x.ndim == 2:
        ps = pltpu.roll(prev, shift, 0)
        reps = x.shape[0] // SUBLANES
        ps = jnp.broadcast_to(ps[None], (reps,) + ps.shape).reshape(x.shape)
        return jnp.where(row < shift, ps, xs)
    for t in range(shift):
        src = prev.shape[1] - shift + t
        xs = jnp.where(row == t, prev[:, src:src + 1, :], xs)
    return xs


def _short_conv_silu(x, prev, w, row):
    y = x * w[CONV_W - 1:CONV_W]
    for shift in range(1, CONV_W):
        tap = CONV_W - 1 - shift
        y = y + _shifted(x, prev, shift, row) * w[tap:tap + 1]
    return _silu(y)


def _delta_prompt_kernel(q_ref, k_ref, v_ref, z_ref, ab_ref, wq_ref, wk_ref, wv_ref, alog_ref, dtb_ref,
                         og_ref, o_ref, sout_ref, s_ref, pq_ref, pk_ref, pv_ref, bq_ref, bk_ref, bv_ref, *, tt):
    t = pl.program_id(2)
    c = DELTA_CHUNK
    n = tt // c
    d = HEAD_DIM

    @pl.when(t == 0)
    def _():
        s_ref[...] = jnp.zeros_like(s_ref)
        for ref in (pq_ref, pk_ref, pv_ref, bq_ref, bk_ref, bv_ref):
            ref[...] = jnp.zeros_like(ref)

    row = lax.broadcasted_iota(jnp.int32, (tt, 1), 0)

    def conv(x_ref, p_ref, b_ref, w_ref):
        x = x_ref[...]
        w = w_ref[...]
        x1 = _shifted(x, p_ref[...], 1, row)
        b = x * w[1:2] + x1 * w[0:1]
        y = x * w[3:4] + x1 * w[2:3] + _shifted(b, b_ref[...], 2, row)
        p_ref[...] = x[tt - SUBLANES:, :]
        b_ref[...] = b[tt - SUBLANES:, :]
        return _silu(y)

    assert CONV_W == 4
    qc = conv(q_ref, pq_ref, bq_ref, wq_ref)
    kc = conv(k_ref, pk_ref, bk_ref, wk_ref)
    vc = conv(v_ref, pv_ref, bv_ref, wv_ref)

    ab = ab_ref[...]
    g_all = -jnp.exp(alog_ref[...]) * _softplus(ab + dtb_ref[...])
    beta_all = _sigmoid(ab)
    gc_all = _block_cumsum(g_all, c)

    heads = range(HEAD_GROUP)

    def pairs(cols):
        return jnp.concatenate([x.reshape(n // 2, 2 * c, x.shape[-1]) for x in cols], axis=0)

    q = pairs([_l2norm(qc[:, hh * d:(hh + 1) * d]) * (d ** -0.5) for hh in heads])
    k = pairs([_l2norm(kc[:, hh * d:(hh + 1) * d]) for hh in heads])
    v = pairs([vc[:, hh * d:(hh + 1) * d] for hh in heads])
    gcol = pairs([gc_all[:, hh:hh + 1] for hh in heads])
    bcol = pairs([beta_all[:, HEAD_GROUP + hh:HEAD_GROUP + hh + 1] for hh in heads])
    qt, m, bmat, o0, g_last = _delta_pair_terms(q, k, v, gcol, bcol)

    gain = og_ref[...]
    states = [s_ref[hh] for hh in heads]
    for ch in range(n):
        for hh in heads:
            i = hh * n + ch
            s = states[hh]
            r = jnp.dot(jnp.concatenate([qt[i], m[i]], axis=0), s.astype(BF16),
                        preferred_element_type=F32)
            o = r[:c] + o0[i]
            states[hh] = s * g_last[i] + (bmat[i] - r[c:])
            zz = z_ref[ch * c:(ch + 1) * c, hh * d:(hh + 1) * d]
            o_ref[ch * c:(ch + 1) * c, hh * d:(hh + 1) * d] = _rms(o, gain) * _silu(zz)
    for hh in range(HEAD_GROUP):
        s_ref[hh] = states[hh]
        sout_ref[hh] = states[hh]


def _delta_prompt(proj, tail, batch, seq, conv_w, alog_rows, dtb_rows, out_gain):
    tt = min(512, seq)
    nt = seq // tt
    gw = HEAD_GROUP * HEAD_DIM
    ng = N_HEAD_GROUPS
    assert seq % tt == 0 and tt % DELTA_CHUNK == 0

    def col(base):
        return pl.BlockSpec((tt, gw), lambda b, g, t: (b * nt + t, base * ng + g))

    def wcol(base):
        return pl.BlockSpec((CONV_W, gw), lambda b, g, t: (0, base * ng + g))

    ab_base = X_W // LANES
    return pl.pallas_call(
        functools.partial(_delta_prompt_kernel, tt=tt),
        grid=(batch, ng, nt),
        in_specs=[
            col(0), col(1), col(2), col(3),
            pl.BlockSpec((tt, LANES), lambda b, g, t: (b * nt + t, ab_base + g)),
            wcol(0), wcol(1), wcol(2),
            pl.BlockSpec((1, LANES), lambda b, g, t: (0, g)),
            pl.BlockSpec((1, LANES), lambda b, g, t: (0, g)),
            pl.BlockSpec((1, HEAD_DIM), lambda b, g, t: (0, 0)),
        ],
        out_specs=[
            pl.BlockSpec((tt, gw), lambda b, g, t: (b * nt + t, g)),
            pl.BlockSpec((None, HEAD_GROUP, HEAD_DIM, HEAD_DIM), lambda b, g, t: (b, g, 0, 0)),
        ],
        out_shape=[
            jax.ShapeDtypeStruct((batch * seq, DELTA_W), F32),
            jax.ShapeDtypeStruct((batch, DELTA_HEADS, HEAD_DIM, HEAD_DIM), F32),
        ],
        scratch_shapes=[
            pltpu.VMEM((HEAD_GROUP, HEAD_DIM, HEAD_DIM), F32),
        ] + [pltpu.VMEM((SUBLANES, gw), F32)] * 6,
        compiler_params=_params("parallel", "parallel", "arbitrary"),
    )(proj, proj, proj, proj, tail, conv_w, conv_w, conv_w, alog_rows, dtb_rows, out_gain)


def _delta_sample_kernel(q_ref, k_ref, v_ref, z_ref, ab_ref, bq_ref, bk_ref, bv_ref, s0_ref, wq_ref, wk_ref,
                         wv_ref, alog_ref, dtb_ref, og_ref, o_ref, sout_ref, *, bb, seq):
    d = HEAD_DIM
    rows = bb * seq
    row = lax.broadcasted_iota(jnp.int32, (1, seq, 1), 1)

    def conv(x_ref, buf_ref, w_ref):
        x = x_ref[...].reshape(bb, seq, x_ref.shape[1])
        return _short_conv_silu(x, buf_ref[...], w_ref[...], row).reshape(rows, x_ref.shape[1])

    qc = conv(q_ref, bq_ref, wq_ref)
    kc = conv(k_ref, bk_ref, wk_ref)
    vc = conv(v_ref, bv_ref, wv_ref)

    ab = ab_ref[...]
    g_all = -jnp.exp(alog_ref[...]) * _softplus(ab + dtb_ref[...])
    beta_all = _sigmoid(ab)
    gc_all = _block_cumsum(g_all, seq)

    heads = range(HEAD_GROUP)
    q = jnp.stack([_l2norm(qc[:, hh * d:(hh + 1) * d]) * (d ** -0.5) for hh in heads])
    k = jnp.stack([_l2norm(kc[:, hh * d:(hh + 1) * d]) for hh in heads])
    v = jnp.stack([vc[:, hh * d:(hh + 1) * d] for hh in heads])
    gcol = jnp.stack([gc_all[:, hh:hh + 1] for hh in heads])
    bcol = jnp.stack([beta_all[:, HEAD_GROUP + hh:HEAD_GROUP + hh + 1] for hh in heads])
    wu, a_intra, egc = _wy_solve(q, k, v, gcol, bcol, seq)
    q_dec = (q * egc).astype(BF16)
    g4 = gcol.reshape(HEAD_GROUP, bb, seq, 1)
    glast = g4[:, :, seq - 1:seq, :]
    k_dec = (k.reshape(HEAD_GROUP, bb, seq, d) * jnp.exp(glast - g4)).astype(BF16)
    g_last = jnp.exp(glast)

    gain = og_ref[...]
    for hh in heads:
        s = s0_ref[:, hh]
        w = wu[hh, :, :d].astype(BF16).reshape(bb, seq, d)
        u = wu[hh, :, d:].reshape(bb, seq, d)
        wq = jnp.concatenate([w, q_dec[hh].reshape(bb, seq, d)], axis=1)
        r = jnp.einsum("bcd,bde->bce", wq, s.astype(BF16), preferred_element_type=F32)
        v_new = u - r[:, :seq]
        vb = v_new.astype(BF16)
        o = r[:, seq:].reshape(rows, d) + jnp.dot(a_intra[hh], vb.reshape(rows, d), preferred_element_type=F32)
        sout_ref[:, hh] = s * g_last[hh] + jnp.einsum("bcd,bce->bde", k_dec[hh], vb,
                                                      preferred_element_type=F32)
        o_ref[:, hh * d:(hh + 1) * d] = _rms(o, gain) * _silu(z_ref[:, hh * d:(hh + 1) * d])


def _delta_sample_into_kernel(*refs, bb, seq):
    _delta_sample_kernel(*refs[:15], *refs[16:], bb=bb, seq=seq)


def _delta_sample_first_kernel(*refs, bb, seq, layer):
    sout_ref = refs[-1]
    for other in range(sout_ref.shape[0]):
        if other != layer:
            sout_ref[other] = jnp.zeros(sout_ref.shape[1:], F32)
    _delta_sample_kernel(*refs[:-1], sout_ref.at[layer], bb=bb, seq=seq)


def _delta_sample(proj, tail, batch, seq, conv_bufs, states, j, conv_w, alog_rows, dtb_rows, out_gain, new_states):
    bb = LANES // seq
    assert batch % bb == 0 and seq == SUBLANES
    rows = bb * seq
    gw = HEAD_GROUP * HEAD_DIM
    ng = N_HEAD_GROUPS

    def col(base):
        return pl.BlockSpec((rows, gw), lambda i, g: (i, base * ng + g))

    def bufcol(base):
        return pl.BlockSpec((None, bb, CONV_W - 1, gw), lambda i, g: (j, i, 0, base * ng + g))

    def wcol(base):
        return pl.BlockSpec((CONV_W, gw), lambda i, g: (0, base * ng + g))

    ab_base = X_W // LANES
    state_spec = pl.BlockSpec((None, bb, HEAD_GROUP, HEAD_DIM, HEAD_DIM), lambda i, g: (j, i, g, 0, 0))
    in_specs = [
        col(0), col(1), col(2), col(3),
        pl.BlockSpec((rows, LANES), lambda i, g: (i, ab_base + g)),
        bufcol(0), bufcol(1), bufcol(2),
        state_spec,
        wcol(0), wcol(1), wcol(2),
        pl.BlockSpec((1, LANES), lambda i, g: (0, g)),
        pl.BlockSpec((1, LANES), lambda i, g: (0, g)),
        pl.BlockSpec((1, HEAD_DIM), lambda i, g: (0, 0)),
    ]
    args = (proj, proj, proj, proj, tail, conv_bufs, conv_bufs, conv_bufs, states, conv_w, conv_w, conv_w,
            alog_rows, dtb_rows, out_gain)
    if new_states is None:
        body, aliases = functools.partial(_delta_sample_first_kernel, bb=bb, seq=seq, layer=j), {}
        out_state_spec = pl.BlockSpec((states.shape[0], bb, HEAD_GROUP, HEAD_DIM, HEAD_DIM),
                                      lambda i, g: (0, i, g, 0, 0))
    else:
        in_specs.append(pl.BlockSpec(memory_space=pl.ANY))
        args += (new_states,)
        body, aliases = functools.partial(_delta_sample_into_kernel, bb=bb, seq=seq), {len(args) - 1: 1}
        out_state_spec = state_spec
    return pl.pallas_call(
        body,
        grid=(batch // bb, ng),
        in_specs=in_specs,
        out_specs=[
            pl.BlockSpec((rows, gw), lambda i, g: (i, g)),
            out_state_spec,
        ],
        out_shape=[
            jax.ShapeDtypeStruct((batch * seq, DELTA_W), F32),
            jax.ShapeDtypeStruct(states.shape, F32),
        ],
        input_output_aliases=aliases,
        compiler_params=_params("parallel", "parallel"),
    )(*args)


def _gelu(x):
    return 0.5 * x * (1.0 + lax.erf(x * (2.0 ** -0.5)))


def _layernorm(x, gain, bias):
    mu = jnp.mean(x, axis=-1, keepdims=True)
    xc = x - mu
    var = jnp.mean(xc * xc, axis=-1, keepdims=True)
    return xc * lax.rsqrt(var + EPS) * gain + bias


def _gmlp_prompt_kernel(u_ref, v_ref, lng_ref, lnb_ref, ws_ref, bs_ref, o_ref, *, chunks):
    c = GMLP_CHUNK
    d = HEAD_DIM
    ri = lax.broadcasted_iota(jnp.int32, (c, c), 0)
    ci = lax.broadcasted_iota(jnp.int32, (c, c), 1)
    lower = ci <= ri
    v = _layernorm(_gelu(v_ref[...]), lng_ref[...], lnb_ref[...]).astype(BF16)
    bs = bs_ref[...]
    for g in range(GMLP_GROUPS):
        w = jnp.where(lower, ws_ref[g], 0.0).astype(BF16)
        bias = bs[:, g:g + 1]
        for ch in range(chunks):
            rs = slice(ch * c, (ch + 1) * c)
            cs = slice(g * d, (g + 1) * d)
            mixed = jnp.dot(w, v[rs, cs], preferred_element_type=F32) + bias
            o_ref[rs, cs] = _gelu(u_ref[rs, cs]) * mixed


def _gmlp_prompt(proj, ln_gain, ln_bias, w_s, b_s_t):
    m = proj.shape[0]
    chunks = 2
    tm = chunks * GMLP_CHUNK
    assert m % tm == 0
    return pl.pallas_call(
        functools.partial(_gmlp_prompt_kernel, chunks=chunks),
        grid=(m // tm,),
        in_specs=[
            pl.BlockSpec((tm, GMLP_W), lambda i: (i, 0)),
            pl.BlockSpec((tm, GMLP_W), lambda i: (i, 1)),
            pl.BlockSpec((1, GMLP_W), lambda i: (0, 0)),
            pl.BlockSpec((1, GMLP_W), lambda i: (0, 0)),
            pl.BlockSpec((GMLP_GROUPS, GMLP_CHUNK, GMLP_CHUNK), lambda i: (0, 0, 0)),
            pl.BlockSpec((GMLP_CHUNK, GMLP_GROUPS), lambda i: (0, 0)),
        ],
        out_specs=pl.BlockSpec((tm, GMLP_W), lambda i: (i, 0)),
        out_shape=jax.ShapeDtypeStruct((m, GMLP_W), F32),
        compiler_params=_params("parallel"),
    )(proj, proj, ln_gain, ln_bias, w_s, b_s_t)


def _gmlp_sample_kernel(u_ref, v_ref, lng_ref, lnb_ref, wx_ref, bx_ref, o_ref, vout_ref, *, bb, seq):
    v = _layernorm(_gelu(v_ref[...]), lng_ref[...], lnb_ref[...])
    vout_ref[...] = v
    v3 = v.reshape(bb, seq, GMLP_W)
    row = lax.broadcasted_iota(jnp.int32, (seq, 1), 0)
    mixed = jnp.broadcast_to(bx_ref[...][None], (bb, seq, GMLP_W))
    for j in range(seq):
        wj = jnp.where(row >= j, wx_ref[j], 0.0)
        mixed = mixed + wj[None] * v3[:, j:j + 1, :]
    o_ref[...] = _gelu(u_ref[...]) * mixed.reshape(bb * seq, GMLP_W)


def _gmlp_sample(proj, batch, seq, ln_gain, ln_bias, w_exp, b_exp):
    bb = 32
    assert batch % bb == 0 and seq == SUBLANES
    rows = bb * seq
    return pl.pallas_call(
        functools.partial(_gmlp_sample_kernel, bb=bb, seq=seq),
        grid=(batch // bb,),
        in_specs=[
            pl.BlockSpec((rows, GMLP_W), lambda i: (i, 0)),
            pl.BlockSpec((rows, GMLP_W), lambda i: (i, 1)),
            pl.BlockSpec((1, GMLP_W), lambda i: (0, 0)),
            pl.BlockSpec((1, GMLP_W), lambda i: (0, 0)),
            pl.BlockSpec((seq, seq, GMLP_W), lambda i: (0, 0, 0)),
            pl.BlockSpec((seq, GMLP_W), lambda i: (0, 0)),
        ],
        out_specs=[
            pl.BlockSpec((rows, GMLP_W), lambda i: (i, 0)),
            pl.BlockSpec((rows, GMLP_W), lambda i: (i, 0)),
        ],
        out_shape=[
            jax.ShapeDtypeStruct((batch * seq, GMLP_W), F32),
            jax.ShapeDtypeStruct((batch * seq, GMLP_W), F32),
        ],
        compiler_params=_params("parallel"),
    )(proj, proj, ln_gain, ln_bias, w_exp, b_exp)


def _softmax(s):
    m = jnp.max(s, axis=-1, keepdims=True)
    e = jnp.exp(s - m)
    return e / jnp.sum(e, axis=-1, keepdims=True)


def _mem_attn_prompt_kernel(q_ref, k_ref, v_ref, o_ref):
    d = HEAD_DIM
    for h in range(X_HEADS):
        sl = slice(h * d, (h + 1) * d)
        s = lax.dot_general(q_ref[:, sl].astype(BF16), k_ref[:, sl].astype(BF16),
                            (((1,), (1,)), ((), ())), preferred_element_type=F32) * (d ** -0.5)
        o_ref[:, sl] = _bdot(_softmax(s), v_ref[:, sl])


def _mem_attn_prompt(proj, q_block, batch, seq, mem_k, mem_v):
    tt = min(512, seq)
    nt = seq // tt
    n_mem = mem_k.shape[1]
    return pl.pallas_call(
        _mem_attn_prompt_kernel,
        grid=(batch, nt),
        in_specs=[
            pl.BlockSpec((tt, X_W), lambda b, t: (b * nt + t, q_block)),
            pl.BlockSpec((None, n_mem, X_W), lambda b, t: (b, 0, 0)),
            pl.BlockSpec((None, n_mem, X_W), lambda b, t: (b, 0, 0)),
        ],
        out_specs=pl.BlockSpec((tt, X_W), lambda b, t: (b * nt + t, 0)),
        out_shape=jax.ShapeDtypeStruct((batch * seq, X_W), F32),
        compiler_params=_params("parallel", "parallel"),
    )(proj, mem_k, mem_v)


def _mem_attn_sample_kernel(q_ref, k_ref, v_ref, o_ref, *, bb, seq):
    d = HEAD_DIM
    nq = X_HEADS * seq
    nk = k_ref.shape[1]
    q = jnp.concatenate([q_ref[:, h * d:(h + 1) * d].reshape(bb, seq, d) for h in range(X_HEADS)], axis=1)
    s = jnp.einsum("bqd,bkd->bqk", q.astype(BF16), k_ref[...].astype(BF16),
                   preferred_element_type=F32) * (d ** -0.5)
    q_head = lax.broadcasted_iota(jnp.int32, (nq, nk), 0) // seq
    k_head = lax.broadcasted_iota(jnp.int32, (nq, nk), 1) % X_HEADS
    p = _softmax(jnp.where(q_head == k_head, s, -1e30)).astype(BF16)
    o = jnp.einsum("bqk,bkd->bqd", p, v_ref[...].astype(BF16), preferred_element_type=F32)
    for h in range(X_HEADS):
        o_ref[:, h * d:(h + 1) * d] = o[:, h * seq:(h + 1) * seq, :].reshape(bb * seq, d)


def _mem_attn_sample(proj, q_block, batch, seq, mem_k, mem_v, layer):
    bb = 8
    rows = bb * seq
    n_rows = mem_k.shape[2]
    return pl.pallas_call(
        functools.partial(_mem_attn_sample_kernel, bb=bb, seq=seq),
        grid=(batch // bb,),
        in_specs=[
            pl.BlockSpec((rows, X_W), lambda i: (i, q_block)),
            pl.BlockSpec((None, bb, n_rows, HEAD_DIM), lambda i: (layer, i, 0, 0)),
            pl.BlockSpec((None, bb, n_rows, HEAD_DIM), lambda i: (layer, i, 0, 0)),
        ],
        out_specs=pl.BlockSpec((rows, X_W), lambda i: (i, 0)),
        out_shape=jax.ShapeDtypeStruct((batch * seq, X_W), F32),
        compiler_params=_params("parallel"),
    )(proj, mem_k, mem_v)


def _delta_tail_weight(w_in_a):
    d_model = w_in_a.shape[0]
    a = w_in_a[:, 4 * DELTA_W:4 * DELTA_W + DELTA_HEADS]
    b = w_in_a[:, 4 * DELTA_W + DELTA_HEADS:4 * DELTA_W + 2 * DELTA_HEADS]
    parts = [w_in_a[:, 4 * DELTA_W + 2 * DELTA_HEADS:]]
    pad = jnp.zeros((d_model, LANES - 2 * HEAD_GROUP), w_in_a.dtype)
    for g in range(N_HEAD_GROUPS):
        hs = slice(g * HEAD_GROUP, (g + 1) * HEAD_GROUP)
        parts += [a[:, hs], b[:, hs], pad]
    parts.append(jnp.zeros((d_model, AB_W - N_HEAD_GROUPS * LANES), w_in_a.dtype))
    return jnp.concatenate(parts, axis=1).astype(BF16)


def _gate_rows(vec):
    row = jnp.zeros((N_HEAD_GROUPS, LANES), F32)
    row = row.at[:, :HEAD_GROUP].set(vec.astype(F32).reshape(N_HEAD_GROUPS, HEAD_GROUP))
    return jnp.concatenate([row.reshape(1, -1), jnp.zeros((1, AB_W - N_HEAD_GROUPS * LANES), F32)], axis=1)


def _ffn_step(x, p, layer, slot, pre, post, round_next):
    w_gu, w_dn = p["ffn_bf16"][(layer, slot)]
    nxt = (layer, 1) if slot == 0 else (layer + 1, 0)
    if round_next and nxt[0] < p["w_out"].shape[0]:
        x, gu_next, dn_next = _ffn(x, p["gains"], layer * 6, pre, post, w_gu, w_dn,
                                   cast_next=(p["w_ffn_gu"], p["w_ffn_dn"]) + nxt)
        p["ffn_bf16"][nxt] = (gu_next, dn_next)
        return x
    return _ffn(x, p["gains"], layer * 6, pre, post, w_gu, w_dn)


def _trunk(x, batch, seq, is_prompt, mem_k, mem_v, conv_bufs, delta_states, p):
    new_conv, new_delta, new_v = [], [], []
    gains = p["gains"]
    depth = p["w_out"].shape[0]
    for i in range(depth):
        j = i // N_MIXERS
        x = _ffn_step(x, p, i, 0, 0, 1, is_prompt)
        if i % N_MIXERS == 0:
            proj, q_src = _norm_matmul_tail(x, gains, i * 6 + 2, p["w_in_a"], j, 4 * DELTA_W, p["w_in_a_tail"][j],
                                            PROJ_COLS_A)
            q_block = 0
            if is_prompt:
                mix, s_new = _delta_prompt(proj, q_src, batch, seq, p["conv_w"][j], p["alog_rows"][j],
                                           p["dtb_rows"][j], p["delta_norm_gain"][j])
                new_delta.append(s_new)
            else:
                mix, new_delta = _delta_sample(proj, q_src, batch, seq, conv_bufs, delta_states, j, p["conv_w"][j],
                                               p["alog_rows"][j], p["dtb_rows"][j], p["delta_norm_gain"][j],
                                               new_delta if j > 0 else None)
            new_conv.append(proj.reshape(batch, seq, -1)[:, seq - (CONV_W - 1):, :3 * DELTA_W])
        else:
            proj = q_src = _norm_matmul(x, gains, i * 6 + 2, p["w_in_b"], j, PROJ_COLS)
            q_block = 2 * GMLP_W // X_W
            if is_prompt:
                mix = _gmlp_prompt(proj, p["gmlp_ln_gain"][j], p["gmlp_ln_bias"][j], p["w_spatial"][j],
                                   p["b_spatial_t"][j])
            else:
                mix, v_rows = _gmlp_sample(proj, batch, seq, p["gmlp_ln_gain"][j], p["gmlp_ln_bias"][j],
                                           p["w_spatial_exp"][j], p["b_spatial_exp"][j])
                new_v.append(v_rows.reshape(batch, seq, GMLP_W))
        if is_prompt:
            mem_out = _mem_attn_prompt(q_src, q_block, batch, seq, mem_k[i], mem_v[i])
        else:
            mem_out = _mem_attn_sample(q_src, q_block, batch, seq, mem_k, mem_v, i)
        x = _out_proj(x, mix, mem_out, p["w_out"], gains, i)
        x = _ffn_step(x, p, i, 1, 4, 5, is_prompt)
    return x, new_conv, new_delta, new_v


def kernel(x_prompt, x_sample, mem_prompt, cache_mem_k, cache_mem_v, state_delta, state_conv, norm_gains, w_ffn_gu, w_ffn_dn, w_in_a, conv_w, a_log, dt_bias, delta_norm_gain, w_in_b, gmlp_ln_gain, gmlp_ln_bias, w_spatial, b_spatial, mem_norm_gain, w_mem_kv, w_out):
    batch, seq, d_model = x_prompt.shape
    dec_batch, dec_seq, _ = x_sample.shape
    depth = w_out.shape[0]
    n_mem = mem_prompt.shape[1]
    n_a = w_in_a.shape[0]
    n_b = w_in_b.shape[0]

    w_s_dec = w_spatial[:, :, :dec_seq, :dec_seq]
    p = {
        "gains": norm_gains.reshape(depth * 6, 1, d_model),
        "w_ffn_gu": w_ffn_gu,
        "w_ffn_dn": w_ffn_dn,
        "ffn_bf16": {(0, 0): (w_ffn_gu[0, 0].astype(BF16), w_ffn_dn[0, 0].astype(BF16))},
        "w_in_a": w_in_a[:, :, :4 * DELTA_W].astype(BF16),
        "w_in_a_tail": [_delta_tail_weight(w_in_a[j]) for j in range(n_a)],
        "w_in_b": w_in_b.astype(BF16),
        "w_out": w_out.astype(BF16),
        "conv_w": conv_w,
        "alog_rows": [_gate_rows(a_log[j]) for j in range(n_a)],
        "dtb_rows": [_gate_rows(dt_bias[j]) for j in range(n_a)],
        "delta_norm_gain": delta_norm_gain.reshape(n_a, 1, HEAD_DIM),
        "gmlp_ln_gain": gmlp_ln_gain.reshape(n_b, 1, GMLP_W),
        "gmlp_ln_bias": gmlp_ln_bias.reshape(n_b, 1, GMLP_W),
        "w_spatial": w_spatial,
        "b_spatial_t": jnp.swapaxes(b_spatial, 1, 2),
        "w_spatial_exp": jnp.repeat(jnp.transpose(w_s_dec, (0, 3, 2, 1)), HEAD_DIM, axis=-1),
        "b_spatial_exp": jnp.repeat(jnp.swapaxes(b_spatial[:, :, :dec_seq], 1, 2), HEAD_DIM, axis=-1),
    }

    mem2d = mem_prompt.reshape(batch * n_mem, d_model)
    mem_gains = mem_norm_gain.reshape(depth, 1, d_model)
    w_kv = w_mem_kv.astype(BF16)
    kvs = [_norm_matmul(mem2d, mem_gains, i, w_kv, i, 1024) for i in range(depth)]
    mem_k_prompt = jnp.stack([kv[:, :X_W].reshape(batch, n_mem, X_W) for kv in kvs])
    mem_v_prompt = jnp.stack([kv[:, X_W:].reshape(batch, n_mem, X_W) for kv in kvs])

    y_prompt, conv_p, delta_p, _ = _trunk(x_prompt.reshape(batch * seq, d_model), batch, seq, True,
                                          mem_k_prompt, mem_v_prompt, None, None, p)

    y_sample, conv_s, delta_s, v_s = _trunk(
        x_sample.reshape(dec_batch * dec_seq, d_model), dec_batch, dec_seq, False,
        cache_mem_k.reshape(depth, dec_batch, n_mem * X_HEADS, HEAD_DIM),
        cache_mem_v.reshape(depth, dec_batch, n_mem * X_HEADS, HEAD_DIM),
        state_conv, state_delta, p)

    return (y_prompt.reshape(batch, seq, d_model),
            y_sample.reshape(dec_batch, dec_seq, d_model),
            mem_k_prompt.reshape(depth, batch, n_mem, X_HEADS, HEAD_DIM),
            mem_v_prompt.reshape(depth, batch, n_mem, X_HEADS, HEAD_DIM),
            jnp.stack(delta_p), jnp.stack(conv_p), delta_s, jnp.stack(conv_s), jnp.stack(v_s))
```

```python
import functools

import jax
import jax.numpy as jnp
from jax import lax
from jax.experimental import pallas as pl
from jax.experimental.pallas import tpu as pltpu

F32 = jnp.float32
BF16 = jnp.bfloat16

EPS = 1e-6
HEAD_DIM = 128
X_HEADS = 4
X_W = X_HEADS * HEAD_DIM
DELTA_HEADS = 12
DELTA_W = DELTA_HEADS * HEAD_DIM
CONV_W = 4
DELTA_CHUNK = 64
GMLP_GROUPS = 12
GMLP_W = GMLP_GROUPS * HEAD_DIM
GMLP_CHUNK = 128
N_MIXERS = 2

LANES = 128
SUBLANES = 8
VMEM_LIMIT = 56 * 1024 * 1024
FFN_VMEM_LIMIT = 62 * 1024 * 1024

HEAD_GROUP = 6
N_HEAD_GROUPS = DELTA_HEADS // HEAD_GROUP
FFN_ROWS = 1024
FFN_COLS = 512
PROJ_ROWS = 1024
PROJ_COLS = 1792
PROJ_COLS_A = 1024
AB_W = 512
TAIL_W = X_W + AB_W


def _params(*sem, vmem=VMEM_LIMIT):
    return pltpu.CompilerParams(dimension_semantics=sem, vmem_limit_bytes=vmem)


def _rms(x, gain):
    ms = jnp.mean(x * x, axis=-1, keepdims=True)
    return x * lax.rsqrt(ms + EPS) * gain


def _sigmoid(x):
    return 0.5 * jnp.tanh(0.5 * x) + 0.5


def _silu(x):
    return x * _sigmoid(x)


def _softplus(x):
    return jnp.maximum(x, 0.0) + jnp.log1p(jnp.exp(-jnp.abs(x)))


def _bdot(a, b):
    return jnp.dot(a.astype(BF16), b.astype(BF16), preferred_element_type=F32)


def _ffn_cast_kernel(x_ref, gpre_ref, gpost_ref, wg_ref, wu_ref, wd_ref, cgu_ref, cdn_ref,
                     o_ref, ogu_ref, odn_ref, hn_ref, act_ref):
    ogu_ref[...] = cgu_ref[...].astype(BF16)
    odn_ref[...] = cdn_ref[...].astype(BF16)
    _ffn_kernel(x_ref, gpre_ref, gpost_ref, wg_ref, wu_ref, wd_ref, o_ref, hn_ref, act_ref)


def _ffn_kernel(x_ref, gpre_ref, gpost_ref, wg_ref, wu_ref, wd_ref, o_ref, hn_ref, act_ref):
    f = pl.program_id(1)
    last = pl.num_programs(1) - 1

    def gate_up(slot):
        h = hn_ref[...]
        gate = jnp.dot(h, wg_ref[...], preferred_element_type=F32)
        up = jnp.dot(h, wu_ref[...], preferred_element_type=F32)
        act_ref[slot] = (_silu(gate) * up).astype(BF16)

    def down(slot):
        o_ref[...] += jnp.dot(act_ref[slot], wd_ref[...], preferred_element_type=F32)

    @pl.when(f == 0)
    def _():
        hn_ref[...] = _rms(x_ref[...], gpre_ref[...]).astype(BF16)
        o_ref[...] = jnp.zeros_like(o_ref)
        gate_up(0)

    @pl.when((f > 0) & (f < last))
    def _():
        down((f - 1) % 2)
        gate_up(f % 2)

    @pl.when(f == last)
    def _():
        down((f - 1) % 2)
        o_ref[...] = x_ref[...] + 0.5 * _rms(o_ref[...], gpost_ref[...])


def _ffn(x, gains, gain_base, pre, post, w_gu, w_dn, cast_next=None):
    m, d = x.shape
    ffn = w_dn.shape[0]
    tm = min(FFN_ROWS, m)
    tf = FFN_COLS
    nf = ffn // tf
    nm = m // tm
    assert m % tm == 0 and ffn % tf == 0
    def up_blk(f):
        return jnp.minimum(f, nf - 1)

    def dn_blk(f):
        return jnp.maximum(f - 1, 0)

    in_specs = [
        pl.BlockSpec((tm, d), lambda i, f: (i, 0)),
        pl.BlockSpec((None, 1, d), lambda i, f: (gain_base + pre, 0, 0)),
        pl.BlockSpec((None, 1, d), lambda i, f: (gain_base + post, 0, 0)),
        pl.BlockSpec((d, tf), lambda i, f: (0, up_blk(f))),
        pl.BlockSpec((d, tf), lambda i, f: (0, nf + up_blk(f))),
        pl.BlockSpec((tf, d), lambda i, f: (dn_blk(f), 0)),
    ]
    out_specs = pl.BlockSpec((tm, d), lambda i, f: (i, 0))
    out_shape = jax.ShapeDtypeStruct((m, d), F32)
    scratch = [pltpu.VMEM((tm, d), BF16), pltpu.VMEM((2, tm, tf), BF16)]
    if cast_next is None:
        return pl.pallas_call(
            _ffn_kernel, grid=(nm, nf + 1), in_specs=in_specs, out_specs=out_specs, out_shape=out_shape,
            scratch_shapes=scratch, compiler_params=_params("parallel", "arbitrary", vmem=FFN_VMEM_LIMIT),
        )(x, gains, gains, w_gu, w_gu, w_dn)
    gu_all, dn_all, layer, slot = cast_next
    gu_rows, gu_cols = d // nm, 2 * ffn // nf
    dn_rows = ffn // (nm * nf)
    assert gu_rows * nm == d and gu_cols * nf == 2 * ffn and dn_rows * nm * nf == ffn
    assert gu_rows % 16 == 0 and gu_cols % LANES == 0 and dn_rows % 16 == 0
    return pl.pallas_call(
        _ffn_cast_kernel,
        grid=(nm, nf + 1),
        in_specs=in_specs + [
            pl.BlockSpec((None, None, gu_rows, gu_cols), lambda i, f: (layer, slot, i, up_blk(f))),
            pl.BlockSpec((None, None, dn_rows, d), lambda i, f: (layer, slot, i * nf + up_blk(f), 0)),
        ],
        out_specs=[
            out_specs,
            pl.BlockSpec((gu_rows, gu_cols), lambda i, f: (i, up_blk(f))),
            pl.BlockSpec((dn_rows, d), lambda i, f: (i * nf + up_blk(f), 0)),
        ],
        out_shape=[out_shape, jax.ShapeDtypeStruct((d, 2 * ffn), BF16), jax.ShapeDtypeStruct((ffn, d), BF16)],
        scratch_shapes=scratch,
        compiler_params=_params("parallel", "arbitrary", vmem=FFN_VMEM_LIMIT),
    )(x, gains, gains, w_gu, w_gu, w_dn, gu_all, dn_all)


def _norm_matmul_kernel(x_ref, g_ref, w_ref, o_ref, hn_ref):
    @pl.when(pl.program_id(1) == 0)
    def _():
        hn_ref[...] = _rms(x_ref[...], g_ref[...]).astype(BF16)

    o_ref[...] = jnp.dot(hn_ref[...], w_ref[...], preferred_element_type=F32)


def _norm_matmul(x, gains, gain_idx, w, layer, tn):
    m, d = x.shape
    n = w.shape[2]
    tm = min(PROJ_ROWS, m)
    assert m % tm == 0 and n % tn == 0
    return pl.pallas_call(
        _norm_matmul_kernel,
        grid=(m // tm, n // tn),
        in_specs=[
            pl.BlockSpec((tm, d), lambda i, j: (i, 0)),
            pl.BlockSpec((None, 1, d), lambda i, j: (gain_idx, 0, 0)),
            pl.BlockSpec((None, d, tn), lambda i, j: (layer, 0, j)),
        ],
        out_specs=pl.BlockSpec((tm, tn), lambda i, j: (i, j)),
        out_shape=jax.ShapeDtypeStruct((m, n), F32),
        scratch_shapes=[pltpu.VMEM((tm, d), BF16)],
        compiler_params=_params("parallel", "arbitrary"),
    )(x, gains, w)


def _mem_kv_kernel(x_ref, g_ref, w_ref, k_ref, v_ref):
    kv = jnp.dot(_rms(x_ref[...], g_ref[...]).astype(BF16), w_ref[...], preferred_element_type=F32)
    half = k_ref.shape[-1]
    k_ref[...] = kv[:, :half]
    v_ref[...] = kv[:, half:]


def _mem_kv(mem, gains, w_kv):
    m, d = mem.shape
    depth, _, n = w_kv.shape
    out = jax.ShapeDtypeStruct((depth, m, n // 2), F32)
    return pl.pallas_call(
        _mem_kv_kernel,
        grid=(depth,),
        in_specs=[
            pl.BlockSpec((m, d), lambda l: (0, 0)),
            pl.BlockSpec((None, 1, d), lambda l: (l, 0, 0)),
            pl.BlockSpec((None, d, n), lambda l: (l, 0, 0)),
        ],
        out_specs=[pl.BlockSpec((None, m, n // 2), lambda l: (l, 0, 0))] * 2,
        out_shape=[out, out],
        compiler_params=_params("arbitrary"),
    )(mem, gains, w_kv)


def _norm_matmul_tail_kernel(x_ref, g_ref, w_ref, wt_ref, o_ref, ot_ref, hn_ref):
    j = pl.program_id(1)
    n_main = pl.num_programs(1) - 1

    @pl.when(j == 0)
    def _():
        hn_ref[...] = _rms(x_ref[...], g_ref[...]).astype(BF16)

    @pl.when(j < n_main)
    def _():
        o_ref[...] = jnp.dot(hn_ref[...], w_ref[...], preferred_element_type=F32)

    @pl.when(j == n_main)
    def _():
        ot_ref[...] = jnp.dot(hn_ref[...], wt_ref[...], preferred_element_type=F32)


def _norm_matmul_tail(x, gains, gain_idx, w, layer, n, w_tail, tn):
    m, d = x.shape
    nt = w_tail.shape[1]
    tm = min(PROJ_ROWS, m)
    n_main = n // tn
    assert m % tm == 0 and n % tn == 0

    def blk(j):
        return jnp.minimum(j, n_main - 1)

    return pl.pallas_call(
        _norm_matmul_tail_kernel,
        grid=(m // tm, n_main + 1),
        in_specs=[
            pl.BlockSpec((tm, d), lambda i, j: (i, 0)),
            pl.BlockSpec((None, 1, d), lambda i, j: (gain_idx, 0, 0)),
            pl.BlockSpec((None, d, tn), lambda i, j: (layer, 0, blk(j))),
            pl.BlockSpec((d, nt), lambda i, j: (0, 0)),
        ],
        out_specs=[
            pl.BlockSpec((tm, tn), lambda i, j: (i, blk(j))),
            pl.BlockSpec((tm, nt), lambda i, j: (i, 0)),
        ],
        out_shape=[jax.ShapeDtypeStruct((m, n), F32), jax.ShapeDtypeStruct((m, nt), F32)],
        scratch_shapes=[pltpu.VMEM((tm, d), BF16)],
        compiler_params=_params("parallel", "arbitrary"),
    )(x, gains, w, w_tail)


def _out_proj_kernel(x_ref, mix_ref, mem_ref, wmix_ref, wmem_ref, g_ref, o_ref):
    y = _bdot(mix_ref[...], wmix_ref[...]) + _bdot(mem_ref[...], wmem_ref[...])
    o_ref[...] = x_ref[...] + _rms(y, g_ref[...])


def _out_proj(x, mix, mem, w_out, gains, layer):
    m, d = x.shape
    wm = mix.shape[1]
    wx = mem.shape[1]
    tm = min(512, m)
    assert wm % wx == 0
    return pl.pallas_call(
        _out_proj_kernel,
        grid=(m // tm,),
        in_specs=[
            pl.BlockSpec((tm, d), lambda i: (i, 0)),
            pl.BlockSpec((tm, wm), lambda i: (i, 0)),
            pl.BlockSpec((tm, wx), lambda i: (i, 0)),
            pl.BlockSpec((None, wm, d), lambda i: (layer, 0, 0)),
            pl.BlockSpec((None, wx, d), lambda i: (layer, wm // wx, 0)),
            pl.BlockSpec((None, 1, d), lambda i: (layer * 6 + 3, 0, 0)),
        ],
        out_specs=pl.BlockSpec((tm, d), lambda i: (i, 0)),
        out_shape=jax.ShapeDtypeStruct((m, d), F32),
        compiler_params=_params("parallel"),
    )(x, mix, mem, w_out, w_out, gains)


def _unit_lower_inverse(lmat, c, span=None):
    span = c if span is None else span
    base = 4
    ri = lax.broadcasted_iota(jnp.int32, (c, c), 0)
    ci = lax.broadcasted_iota(jnp.int32, (c, c), 1)
    eye = (ri == ci).astype(F32)
    l_hi, l_lo = _split(lmat)

    def masked(keep):
        return jnp.where(keep, l_hi, jnp.zeros_like(l_hi)), jnp.where(keep, l_lo, jnp.zeros_like(l_lo))

    same = (ri // base) == (ci // base)
    dm = masked(same)
    x = eye - jnp.where(same, lmat, 0.0)
    x = x + _mm3(_split(x), _split(_mm3(dm, dm)))
    size = base
    while size < span:
        inner = ((ri // (2 * size)) == (ci // (2 * size))) & ((ri // size) != (ci // size))
        xs = _split(x)
        x = x - _mm3(_split(_mm3(xs, masked(inner))), xs)
        size *= 2
    return x


def _split(x):
    hi = x.astype(BF16)
    return hi, (x - hi.astype(F32)).astype(BF16)


def _mm3(a, b):
    def mm(p, q):
        return jnp.einsum("nij,njk->nik", p, q, preferred_element_type=F32)

    width = b[0].shape[-1]
    if a[0].shape[-1] % LANES == 0 and width % LANES == 0:
        lhs = jnp.concatenate([a[0], a[1]], axis=-1)
        rhs = jnp.concatenate([jnp.concatenate([b[0], b[1]], axis=-1),
                               jnp.concatenate([b[0], jnp.zeros_like(b[0])], axis=-1)], axis=-2)
        r = mm(lhs, rhs)
        return r[..., :width] + r[..., width:]
    return mm(a[0], b[0]) + (mm(a[0], b[1]) + mm(a[1], b[0]))


def _delta_pair_terms(q, k, v, gcol, bcol):
    n, cp, d = q.shape
    c = DELTA_CHUNK
    wu, a_intra, egc = _wy_solve(q, k, v, gcol, bcol, c)
    wu = wu.astype(BF16)
    awu = jnp.einsum("nij,nje->nie", a_intra, wu, preferred_element_type=F32)
    qt = (q * egc - awu[..., :d]).astype(BF16)
    o0 = awu[..., d:]
    k64 = k.reshape(2 * n, c, d)
    g64 = gcol.reshape(2 * n, c, 1)
    glast = g64[:, c - 1:c, :]
    k_dec = (k64 * jnp.exp(glast - g64)).astype(BF16)
    mb = jnp.einsum("ncd,nce->nde", k_dec, wu.reshape(2 * n, c, 2 * d), preferred_element_type=F32)
    return (qt.reshape(2 * n, c, d), mb[..., :d].astype(BF16), mb[..., d:], o0.reshape(2 * n, c, d),
            jnp.exp(glast))


def _wy_solve(q, k, v, gcol, bcol, span):
    rows = q.shape[1]
    ri = lax.broadcasted_iota(jnp.int32, (rows, rows), 0)
    ci = lax.broadcasted_iota(jnp.int32, (rows, rows), 1)
    eye = (ri == ci).astype(F32)
    same = (ri // span) == (ci // span)
    strict = same & (ci < ri)
    causal = same & (ci <= ri)
    grow = jnp.sum(gcol * eye, axis=1, keepdims=True)
    diff = gcol - grow
    kb = k.astype(BF16)
    kk = jnp.einsum("ncd,nmd->ncm", kb, kb, preferred_element_type=F32)
    lmat = jnp.where(strict, bcol * kk * jnp.exp(jnp.where(strict, diff, 0.0)), 0.0)
    tinv = _unit_lower_inverse(lmat, rows, span)
    egc = jnp.exp(gcol)
    rhs = jnp.concatenate([k * (bcol * egc), v * bcol], axis=-1)
    wu = _mm3(_split(tinv), _split(rhs))
    qk = jnp.einsum("ncd,nmd->ncm", q.astype(BF16), kb, preferred_element_type=F32)
    a_intra = jnp.where(causal, qk * jnp.exp(jnp.where(causal, diff, 0.0)), 0.0).astype(BF16)
    return wu, a_intra, egc


def _l2norm(x):
    return x * lax.rsqrt(jnp.sum(x * x, axis=-1, keepdims=True) + EPS)


def _block_cumsum(g, c):
    rows, lanes = g.shape
    n = rows // c
    ri = lax.broadcasted_iota(jnp.int32, (n, c, c), 1)
    ci = lax.broadcasted_iota(jnp.int32, (n, c, c), 2)
    tri = (ci <= ri).astype(BF16)
    g3 = g.reshape(n, c, lanes)
    hi = g3.astype(BF16)
    r1 = g3 - hi.astype(F32)
    mid = r1.astype(BF16)
    lo = (r1 - mid.astype(F32)).astype(BF16)

    def mm(p):
        return jnp.einsum("nij,njk->nik", tri, p, preferred_element_type=F32)

    return (mm(hi) + (mm(mid) + mm(lo))).reshape(rows, lanes)


def _shifted(x, prev, shift, row):
    xs = pltpu.roll(x, shift, x.ndim - 2)
    if x.ndim == 2:
        ps = pltpu.roll(prev, shift, 0)
        reps = x.shape[0] // SUBLANES
        ps = jnp.broadcast_to(ps[None], (reps,) + ps.shape).reshape(x.shape)
        return jnp.where(row < shift, ps, xs)
    for t in range(shift):
        src = prev.shape[1] - shift + t
        xs = jnp.where(row == t, prev[:, src:src + 1, :], xs)
    return xs


def _short_conv_silu(x, prev, w, row):
    y = x * w[CONV_W - 1:CONV_W]
    for shift in range(1, CONV_W):
        tap = CONV_W - 1 - shift
        y = y + _shifted(x, prev, shift, row) * w[tap:tap + 1]
    return _silu(y)


def _delta_prompt_kernel(q_ref, k_ref, v_ref, z_ref, ab_ref, wq_ref, wk_ref, wv_ref, alog_ref, dtb_ref,
                         og_ref, o_ref, sout_ref, s_ref, pq_ref, pk_ref, pv_ref, bq_ref, bk_ref, bv_ref, *, tt):
    t = pl.program_id(2)
    c = DELTA_CHUNK
    n = tt // c
    d = HEAD_DIM

    @pl.when(t == 0)
    def _():
        s_ref[...] = jnp.zeros_like(s_ref)
        for ref in (pq_ref, pk_ref, pv_ref, bq_ref, bk_ref, bv_ref):
            ref[...] = jnp.zeros_like(ref)

    row = lax.broadcasted_iota(jnp.int32, (tt, 1), 0)

    def conv(x_ref, p_ref, b_ref, w_ref):
        x = x_ref[...]
        w = w_ref[...]
        x1 = _shifted(x, p_ref[...], 1, row)
        b = x * w[1:2] + x1 * w[0:1]
        y = x * w[3:4] + x1 * w[2:3] + _shifted(b, b_ref[...], 2, row)
        p_ref[...] = x[tt - SUBLANES:, :]
        b_ref[...] = b[tt - SUBLANES:, :]
        return _silu(y)

    assert CONV_W == 4
    qc = conv(q_ref, pq_ref, bq_ref, wq_ref)
    kc = conv(k_ref, pk_ref, bk_ref, wk_ref)
    vc = conv(v_ref, pv_ref, bv_ref, wv_ref)

    ab = ab_ref[...]
    g_all = -jnp.exp(alog_ref[...]) * _softplus(ab + dtb_ref[...])
    beta_all = _sigmoid(ab)
    gc_all = _block_cumsum(g_all, c)

    heads = range(HEAD_GROUP)

    def pairs(cols):
        return jnp.concatenate([x.reshape(n // 2, 2 * c, x.shape[-1]) for x in cols], axis=0)

    q = pairs([_l2norm(qc[:, hh * d:(hh + 1) * d]) * (d ** -0.5) for hh in heads])
    k = pairs([_l2norm(kc[:, hh * d:(hh + 1) * d]) for hh in heads])
    v = pairs([vc[:, hh * d:(hh + 1) * d] for hh in heads])
    gcol = pairs([gc_all[:, hh:hh + 1] for hh in heads])
    bcol = pairs([beta_all[:, HEAD_GROUP + hh:HEAD_GROUP + hh + 1] for hh in heads])
    qt, m, bmat, o0, g_last = _delta_pair_terms(q, k, v, gcol, bcol)

    gain = og_ref[...]
    states = [s_ref[hh] for hh in heads]
    for ch in range(n):
        for hh in heads:
            i = hh * n + ch
            s = states[hh]
            r = jnp.dot(jnp.concatenate([qt[i], m[i]], axis=0), s.astype(BF16),
                        preferred_element_type=F32)
            o = r[:c] + o0[i]
            states[hh] = s * g_last[i] + (bmat[i] - r[c:])
            zz = z_ref[ch * c:(ch + 1) * c, hh * d:(hh + 1) * d]
            o_ref[ch * c:(ch + 1) * c, hh * d:(hh + 1) * d] = _rms(o, gain) * _silu(zz)
    for hh in range(HEAD_GROUP):
        s_ref[hh] = states[hh]
        sout_ref[hh] = states[hh]


def _delta_prompt(proj, tail, batch, seq, conv_w, alog_rows, dtb_rows, out_gain):
    tt = min(512, seq)
    nt = seq // tt
    gw = HEAD_GROUP * HEAD_DIM
    ng = N_HEAD_GROUPS
    assert seq % tt == 0 and tt % DELTA_CHUNK == 0

    def col(base):
        return pl.BlockSpec((tt, gw), lambda b, g, t: (b * nt + t, base * ng + g))

    def wcol(base):
        return pl.BlockSpec((CONV_W, gw), lambda b, g, t: (0, base * ng + g))

    ab_base = X_W // LANES
    return pl.pallas_call(
        functools.partial(_delta_prompt_kernel, tt=tt),
        grid=(batch, ng, nt),
        in_specs=[
            col(0), col(1), col(2), col(3),
            pl.BlockSpec((tt, LANES), lambda b, g, t: (b * nt + t, ab_base + g)),
            wcol(0), wcol(1), wcol(2),
            pl.BlockSpec((1, LANES), lambda b, g, t: (0, g)),
            pl.BlockSpec((1, LANES), lambda b, g, t: (0, g)),
            pl.BlockSpec((1, HEAD_DIM), lambda b, g, t: (0, 0)),
        ],
        out_specs=[
            pl.BlockSpec((tt, gw), lambda b, g, t: (b * nt + t, g)),
            pl.BlockSpec((None, HEAD_GROUP, HEAD_DIM, HEAD_DIM), lambda b, g, t: (b, g, 0, 0)),
        ],
        out_shape=[
            jax.ShapeDtypeStruct((batch * seq, DELTA_W), F32),
            jax.ShapeDtypeStruct((batch, DELTA_HEADS, HEAD_DIM, HEAD_DIM), F32),
        ],
        scratch_shapes=[
            pltpu.VMEM((HEAD_GROUP, HEAD_DIM, HEAD_DIM), F32),
        ] + [pltpu.VMEM((SUBLANES, gw), F32)] * 6,
        compiler_params=_params("parallel", "parallel", "arbitrary"),
    )(proj, proj, proj, proj, tail, conv_w, conv_w, conv_w, alog_rows, dtb_rows, out_gain)


def _delta_sample_kernel(q_ref, k_ref, v_ref, z_ref, ab_ref, bq_ref, bk_ref, bv_ref, s0_ref, wq_ref, wk_ref,
                         wv_ref, alog_ref, dtb_ref, og_ref, o_ref, sout_ref, *, bb, seq):
    d = HEAD_DIM
    rows = bb * seq
    row = lax.broadcasted_iota(jnp.int32, (1, seq, 1), 1)

    def conv(x_ref, buf_ref, w_ref):
        x = x_ref[...].reshape(bb, seq, x_ref.shape[1])
        return _short_conv_silu(x, buf_ref[...], w_ref[...], row).reshape(rows, x_ref.shape[1])

    qc = conv(q_ref, bq_ref, wq_ref)
    kc = conv(k_ref, bk_ref, wk_ref)
    vc = conv(v_ref, bv_ref, wv_ref)

    ab = ab_ref[...]
    g_all = -jnp.exp(alog_ref[...]) * _softplus(ab + dtb_ref[...])
    beta_all = _sigmoid(ab)
    gc_all = _block_cumsum(g_all, seq)

    heads = range(HEAD_GROUP)
    q = jnp.stack([_l2norm(qc[:, hh * d:(hh + 1) * d]) * (d ** -0.5) for hh in heads])
    k = jnp.stack([_l2norm(kc[:, hh * d:(hh + 1) * d]) for hh in heads])
    v = jnp.stack([vc[:, hh * d:(hh + 1) * d] for hh in heads])
    gcol = jnp.stack([gc_all[:, hh:hh + 1] for hh in heads])
    bcol = jnp.stack([beta_all[:, HEAD_GROUP + hh:HEAD_GROUP + hh + 1] for hh in heads])
    wu, a_intra, egc = _wy_solve(q, k, v, gcol, bcol, seq)
    q_dec = (q * egc).astype(BF16)
    g4 = gcol.reshape(HEAD_GROUP, bb, seq, 1)
    glast = g4[:, :, seq - 1:seq, :]
    k_dec = (k.reshape(HEAD_GROUP, bb, seq, d) * jnp.exp(glast - g4)).astype(BF16)
    g_last = jnp.exp(glast)

    gain = og_ref[...]
    for hh in heads:
        s = s0_ref[:, hh]
        w = wu[hh, :, :d].astype(BF16).reshape(bb, seq, d)
        u = wu[hh, :, d:].reshape(bb, seq, d)
        wq = jnp.concatenate([w, q_dec[hh].reshape(bb, seq, d)], axis=1)
        r = jnp.einsum("bcd,bde->bce", wq, s.astype(BF16), preferred_element_type=F32)
        v_new = u - r[:, :seq]
        vb = v_new.astype(BF16)
        o = r[:, seq:].reshape(rows, d) + jnp.dot(a_intra[hh], vb.reshape(rows, d), preferred_element_type=F32)
        sout_ref[:, hh] = s * g_last[hh] + jnp.einsum("bcd,bce->bde", k_dec[hh], vb,
                                                      preferred_element_type=F32)
        o_ref[:, hh * d:(hh + 1) * d] = _rms(o, gain) * _silu(z_ref[:, hh * d:(hh + 1) * d])


def _delta_sample_into_kernel(*refs, bb, seq):
    _delta_sample_kernel(*refs[:15], *refs[16:], bb=bb, seq=seq)


def _delta_sample_first_kernel(*refs, bb, seq, layer):
    sout_ref = refs[-1]
    for other in range(sout_ref.shape[0]):
        if other != layer:
            sout_ref[other] = jnp.zeros(sout_ref.shape[1:], F32)
    _delta_sample_kernel(*refs[:-1], sout_ref.at[layer], bb=bb, seq=seq)


def _delta_sample(proj, tail, batch, seq, conv_bufs, states, j, conv_w, alog_rows, dtb_rows, out_gain, new_states):
    bb = LANES // seq
    assert batch % bb == 0 and seq == SUBLANES
    rows = bb * seq
    gw = HEAD_GROUP * HEAD_DIM
    ng = N_HEAD_GROUPS

    def col(base):
        return pl.BlockSpec((rows, gw), lambda i, g: (i, base * ng + g))

    def bufcol(base):
        return pl.BlockSpec((None, bb, CONV_W - 1, gw), lambda i, g: (j, i, 0, base * ng + g))

    def wcol(base):
        return pl.BlockSpec((CONV_W, gw), lambda i, g: (0, base * ng + g))

    ab_base = X_W // LANES
    state_spec = pl.BlockSpec((None, bb, HEAD_GROUP, HEAD_DIM, HEAD_DIM), lambda i, g: (j, i, g, 0, 0))
    in_specs = [
        col(0), col(1), col(2), col(3),
        pl.BlockSpec((rows, LANES), lambda i, g: (i, ab_base + g)),
        bufcol(0), bufcol(1), bufcol(2),
        state_spec,
        wcol(0), wcol(1), wcol(2),
        pl.BlockSpec((1, LANES), lambda i, g: (0, g)),
        pl.BlockSpec((1, LANES), lambda i, g: (0, g)),
        pl.BlockSpec((1, HEAD_DIM), lambda i, g: (0, 0)),
    ]
    args = (proj, proj, proj, proj, tail, conv_bufs, conv_bufs, conv_bufs, states, conv_w, conv_w, conv_w,
            alog_rows, dtb_rows, out_gain)
    if new_states is None:
        body, aliases = functools.partial(_delta_sample_first_kernel, bb=bb, seq=seq, layer=j), {}
        out_state_spec = pl.BlockSpec((states.shape[0], bb, HEAD_GROUP, HEAD_DIM, HEAD_DIM),
                                      lambda i, g: (0, i, g, 0, 0))
    else:
        in_specs.append(pl.BlockSpec(memory_space=pl.ANY))
        args += (new_states,)
        body, aliases = functools.partial(_delta_sample_into_kernel, bb=bb, seq=seq), {len(args) - 1: 1}
        out_state_spec = state_spec
    return pl.pallas_call(
        body,
        grid=(batch // bb, ng),
        in_specs=in_specs,
        out_specs=[
            pl.BlockSpec((rows, gw), lambda i, g: (i, g)),
            out_state_spec,
        ],
        out_shape=[
            jax.ShapeDtypeStruct((batch * seq, DELTA_W), F32),
            jax.ShapeDtypeStruct(states.shape, F32),
        ],
        input_output_aliases=aliases,
        compiler_params=_params("parallel", "parallel"),
    )(*args)


def _gelu(x):
    return 0.5 * x * (1.0 + lax.erf(x * (2.0 ** -0.5)))


def _layernorm(x, gain, bias):
    mu = jnp.mean(x, axis=-1, keepdims=True)
    xc = x - mu
    var = jnp.mean(xc * xc, axis=-1, keepdims=True)
    return xc * lax.rsqrt(var + EPS) * gain + bias


def _gmlp_prompt_kernel(u_ref, v_ref, lng_ref, lnb_ref, ws_ref, bs_ref, o_ref, *, chunks):
    c = GMLP_CHUNK
    d = HEAD_DIM
    ri = lax.broadcasted_iota(jnp.int32, (c, c), 0)
    ci = lax.broadcasted_iota(jnp.int32, (c, c), 1)
    lower = ci <= ri
    v = _layernorm(_gelu(v_ref[...]), lng_ref[...], lnb_ref[...]).astype(BF16)
    bs = bs_ref[...]
    for g in range(GMLP_GROUPS):
        w = jnp.where(lower, ws_ref[g], 0.0).astype(BF16)
        bias = bs[:, g:g + 1]
        for ch in range(chunks):
            rs = slice(ch * c, (ch + 1) * c)
            cs = slice(g * d, (g + 1) * d)
            mixed = jnp.dot(w, v[rs, cs], preferred_element_type=F32) + bias
            o_ref[rs, cs] = _gelu(u_ref[rs, cs]) * mixed


def _gmlp_prompt(proj, ln_gain, ln_bias, w_s, b_s_t):
    m = proj.shape[0]
    chunks = 2
    tm = chunks * GMLP_CHUNK
    assert m % tm == 0
    return pl.pallas_call(
        functools.partial(_gmlp_prompt_kernel, chunks=chunks),
        grid=(m // tm,),
        in_specs=[
            pl.BlockSpec((tm, GMLP_W), lambda i: (i, 0)),
            pl.BlockSpec((tm, GMLP_W), lambda i: (i, 1)),
            pl.BlockSpec((1, GMLP_W), lambda i: (0, 0)),
            pl.BlockSpec((1, GMLP_W), lambda i: (0, 0)),
            pl.BlockSpec((GMLP_GROUPS, GMLP_CHUNK, GMLP_CHUNK), lambda i: (0, 0, 0)),
            pl.BlockSpec((GMLP_CHUNK, GMLP_GROUPS), lambda i: (0, 0)),
        ],
        out_specs=pl.BlockSpec((tm, GMLP_W), lambda i: (i, 0)),
        out_shape=jax.ShapeDtypeStruct((m, GMLP_W), F32),
        compiler_params=_params("parallel"),
    )(proj, proj, ln_gain, ln_bias, w_s, b_s_t)


def _gmlp_sample_kernel(u_ref, v_ref, lng_ref, lnb_ref, wx_ref, bx_ref, o_ref, vout_ref, *, bb, seq):
    v = _layernorm(_gelu(v_ref[...]), lng_ref[...], lnb_ref[...])
    vout_ref[...] = v
    v3 = v.reshape(bb, seq, GMLP_W)
    row = lax.broadcasted_iota(jnp.int32, (seq, 1), 0)
    mixed = jnp.broadcast_to(bx_ref[...][None], (bb, seq, GMLP_W))
    for j in range(seq):
        wj = jnp.where(row >= j, wx_ref[j], 0.0)
        mixed = mixed + wj[None] * v3[:, j:j + 1, :]
    o_ref[...] = _gelu(u_ref[...]) * mixed.reshape(bb * seq, GMLP_W)


def _gmlp_sample(proj, batch, seq, ln_gain, ln_bias, w_exp, b_exp):
    bb = 32
    assert batch % bb == 0 and seq == SUBLANES
    rows = bb * seq
    return pl.pallas_call(
        functools.partial(_gmlp_sample_kernel, bb=bb, seq=seq),
        grid=(batch // bb,),
        in_specs=[
            pl.BlockSpec((rows, GMLP_W), lambda i: (i, 0)),
            pl.BlockSpec((rows, GMLP_W), lambda i: (i, 1)),
            pl.BlockSpec((1, GMLP_W), lambda i: (0, 0)),
            pl.BlockSpec((1, GMLP_W), lambda i: (0, 0)),
            pl.BlockSpec((seq, seq, GMLP_W), lambda i: (0, 0, 0)),
            pl.BlockSpec((seq, GMLP_W), lambda i: (0, 0)),
        ],
        out_specs=[
            pl.BlockSpec((rows, GMLP_W), lambda i: (i, 0)),
            pl.BlockSpec((rows, GMLP_W), lambda i: (i, 0)),
        ],
        out_shape=[
            jax.ShapeDtypeStruct((batch * seq, GMLP_W), F32),
            jax.ShapeDtypeStruct((batch * seq, GMLP_W), F32),
        ],
        compiler_params=_params("parallel"),
    )(proj, proj, ln_gain, ln_bias, w_exp, b_exp)


def _softmax(s):
    m = jnp.max(s, axis=-1, keepdims=True)
    e = jnp.exp(s - m)
    return e / jnp.sum(e, axis=-1, keepdims=True)


def _mem_attn_prompt_kernel(q_ref, k_ref, v_ref, o_ref):
    d = HEAD_DIM
    for h in range(X_HEADS):
        sl = slice(h * d, (h + 1) * d)
        s = lax.dot_general(q_ref[:, sl].astype(BF16), k_ref[:, sl].astype(BF16),
                            (((1,), (1,)), ((), ())), preferred_element_type=F32) * (d ** -0.5)
        o_ref[:, sl] = _bdot(_softmax(s), v_ref[:, sl])


def _mem_attn_prompt(proj, q_block, batch, seq, mem_k, mem_v, layer):
    tt = min(512, seq)
    nt = seq // tt
    n_mem = mem_k.shape[2]
    return pl.pallas_call(
        _mem_attn_prompt_kernel,
        grid=(batch, nt),
        in_specs=[
            pl.BlockSpec((tt, X_W), lambda b, t: (b * nt + t, q_block)),
            pl.BlockSpec((None, None, n_mem, X_W), lambda b, t: (layer, b, 0, 0)),
            pl.BlockSpec((None, None, n_mem, X_W), lambda b, t: (layer, b, 0, 0)),
        ],
        out_specs=pl.BlockSpec((tt, X_W), lambda b, t: (b * nt + t, 0)),
        out_shape=jax.ShapeDtypeStruct((batch * seq, X_W), F32),
        compiler_params=_params("parallel", "parallel"),
    )(proj, mem_k, mem_v)


def _mem_attn_sample_kernel(q_ref, k_ref, v_ref, o_ref, *, bb, seq):
    d = HEAD_DIM
    nq = X_HEADS * seq
    nk = k_ref.shape[1]
    q = jnp.concatenate([q_ref[:, h * d:(h + 1) * d].reshape(bb, seq, d) for h in range(X_HEADS)], axis=1)
    s = jnp.einsum("bqd,bkd->bqk", q.astype(BF16), k_ref[...].astype(BF16),
                   preferred_element_type=F32) * (d ** -0.5)
    q_head = lax.broadcasted_iota(jnp.int32, (nq, nk), 0) // seq
    k_head = lax.broadcasted_iota(jnp.int32, (nq, nk), 1) % X_HEADS
    p = _softmax(jnp.where(q_head == k_head, s, -1e30)).astype(BF16)
    o = jnp.einsum("bqk,bkd->bqd", p, v_ref[...].astype(BF16), preferred_element_type=F32)
    for h in range(X_HEADS):
        o_ref[:, h * d:(h + 1) * d] = o[:, h * seq:(h + 1) * seq, :].reshape(bb * seq, d)


def _mem_attn_sample(proj, q_block, batch, seq, mem_k, mem_v, layer):
    bb = 8
    rows = bb * seq
    n_rows = mem_k.shape[2]
    return pl.pallas_call(
        functools.partial(_mem_attn_sample_kernel, bb=bb, seq=seq),
        grid=(batch // bb,),
        in_specs=[
            pl.BlockSpec((rows, X_W), lambda i: (i, q_block)),
            pl.BlockSpec((None, bb, n_rows, HEAD_DIM), lambda i: (layer, i, 0, 0)),
            pl.BlockSpec((None, bb, n_rows, HEAD_DIM), lambda i: (layer, i, 0, 0)),
        ],
        out_specs=pl.BlockSpec((rows, X_W), lambda i: (i, 0)),
        out_shape=jax.ShapeDtypeStruct((batch * seq, X_W), F32),
        compiler_params=_params("parallel"),
    )(proj, mem_k, mem_v)


def _delta_tail_weight(w_in_a):
    d_model = w_in_a.shape[0]
    a = w_in_a[:, 4 * DELTA_W:4 * DELTA_W + DELTA_HEADS]
    b = w_in_a[:, 4 * DELTA_W + DELTA_HEADS:4 * DELTA_W + 2 * DELTA_HEADS]
    parts = [w_in_a[:, 4 * DELTA_W + 2 * DELTA_HEADS:]]
    pad = jnp.zeros((d_model, LANES - 2 * HEAD_GROUP), w_in_a.dtype)
    for g in range(N_HEAD_GROUPS):
        hs = slice(g * HEAD_GROUP, (g + 1) * HEAD_GROUP)
        parts += [a[:, hs], b[:, hs], pad]
    parts.append(jnp.zeros((d_model, AB_W - N_HEAD_GROUPS * LANES), w_in_a.dtype))
    return jnp.concatenate(parts, axis=1).astype(BF16)


def _gate_rows(vec):
    row = jnp.zeros((N_HEAD_GROUPS, LANES), F32)
    row = row.at[:, :HEAD_GROUP].set(vec.astype(F32).reshape(N_HEAD_GROUPS, HEAD_GROUP))
    return jnp.concatenate([row.reshape(1, -1), jnp.zeros((1, AB_W - N_HEAD_GROUPS * LANES), F32)], axis=1)


def _ffn_step(x, p, layer, slot, pre, post, round_next):
    w_gu, w_dn = p["ffn_bf16"][(layer, slot)]
    nxt = (layer, 1) if slot == 0 else (layer + 1, 0)
    if round_next and nxt[0] < p["w_out"].shape[0]:
        x, gu_next, dn_next = _ffn(x, p["gains"], layer * 6, pre, post, w_gu, w_dn,
                                   cast_next=(p["w_ffn_gu"], p["w_ffn_dn"]) + nxt)
        p["ffn_bf16"][nxt] = (gu_next, dn_next)
        return x
    return _ffn(x, p["gains"], layer * 6, pre, post, w_gu, w_dn)


def _trunk(x, batch, seq, is_prompt, mem_k, mem_v, conv_bufs, delta_states, p):
    new_conv, new_delta, new_v = [], [], []
    gains = p["gains"]
    depth = p["w_out"].shape[0]
    for i in range(depth):
        j = i // N_MIXERS
        x = _ffn_step(x, p, i, 0, 0, 1, is_prompt)
        if i % N_MIXERS == 0:
            proj, q_src = _norm_matmul_tail(x, gains, i * 6 + 2, p["w_in_a"], j, 4 * DELTA_W, p["w_in_a_tail"][j],
                                            PROJ_COLS_A)
            q_block = 0
            if is_prompt:
                mix, s_new = _delta_prompt(proj, q_src, batch, seq, p["conv_w"][j], p["alog_rows"][j],
                                           p["dtb_rows"][j], p["delta_norm_gain"][j])
                new_delta.append(s_new)
            else:
                mix, new_delta = _delta_sample(proj, q_src, batch, seq, conv_bufs, delta_states, j, p["conv_w"][j],
                                               p["alog_rows"][j], p["dtb_rows"][j], p["delta_norm_gain"][j],
                                               new_delta if j > 0 else None)
            new_conv.append(proj.reshape(batch, seq, -1)[:, seq - (CONV_W - 1):, :3 * DELTA_W])
        else:
            proj = q_src = _norm_matmul(x, gains, i * 6 + 2, p["w_in_b"], j, PROJ_COLS)
            q_block = 2 * GMLP_W // X_W
            if is_prompt:
                mix = _gmlp_prompt(proj, p["gmlp_ln_gain"][j], p["gmlp_ln_bias"][j], p["w_spatial"][j],
                                   p["b_spatial_t"][j])
            else:
                mix, v_rows = _gmlp_sample(proj, batch, seq, p["gmlp_ln_gain"][j], p["gmlp_ln_bias"][j],
                                           p["w_spatial_exp"][j], p["b_spatial_exp"][j])
                new_v.append(v_rows.reshape(batch, seq, GMLP_W))
        if is_prompt:
            mem_out = _mem_attn_prompt(q_src, q_block, batch, seq, mem_k, mem_v, i)
        else:
            mem_out = _mem_attn_sample(q_src, q_block, batch, seq, mem_k, mem_v, i)
        x = _out_proj(x, mix, mem_out, p["w_out"], gains, i)
        x = _ffn_step(x, p, i, 1, 4, 5, is_prompt)
    return x, new_conv, new_delta, new_v


def kernel(x_prompt, x_sample, mem_prompt, cache_mem_k, cache_mem_v, state_delta, state_conv, norm_gains, w_ffn_gu, w_ffn_dn, w_in_a, conv_w, a_log, dt_bias, delta_norm_gain, w_in_b, gmlp_ln_gain, gmlp_ln_bias, w_spatial, b_spatial, mem_norm_gain, w_mem_kv, w_out):
    batch, seq, d_model = x_prompt.shape
    dec_batch, dec_seq, _ = x_sample.shape
    depth = w_out.shape[0]
    n_mem = mem_prompt.shape[1]
    n_a = w_in_a.shape[0]
    n_b = w_in_b.shape[0]

    w_s_dec = w_spatial[:, :, :dec_seq, :dec_seq]
    p = {
        "gains": norm_gains.reshape(depth * 6, 1, d_model),
        "w_ffn_gu": w_ffn_gu,
        "w_ffn_dn": w_ffn_dn,
        "ffn_bf16": {(0, 0): (w_ffn_gu[0, 0].astype(BF16), w_ffn_dn[0, 0].astype(BF16))},
        "w_in_a": w_in_a.astype(BF16),
        "w_in_a_tail": [_delta_tail_weight(w_in_a[j]) for j in range(n_a)],
        "w_in_b": w_in_b.astype(BF16),
        "w_out": w_out.astype(BF16),
        "conv_w": conv_w,
        "alog_rows": [_gate_rows(a_log[j]) for j in range(n_a)],
        "dtb_rows": [_gate_rows(dt_bias[j]) for j in range(n_a)],
        "delta_norm_gain": delta_norm_gain.reshape(n_a, 1, HEAD_DIM),
        "gmlp_ln_gain": gmlp_ln_gain.reshape(n_b, 1, GMLP_W),
        "gmlp_ln_bias": gmlp_ln_bias.reshape(n_b, 1, GMLP_W),
        "w_spatial": w_spatial,
        "b_spatial_t": jnp.swapaxes(b_spatial, 1, 2),
        "w_spatial_exp": jnp.repeat(jnp.transpose(w_s_dec, (0, 3, 2, 1)), HEAD_DIM, axis=-1),
        "b_spatial_exp": jnp.repeat(jnp.swapaxes(b_spatial[:, :, :dec_seq], 1, 2), HEAD_DIM, axis=-1),
    }

    mem2d = mem_prompt.reshape(batch * n_mem, d_model)
    mem_gains = mem_norm_gain.reshape(depth, 1, d_model)
    w_kv = w_mem_kv.astype(BF16)
    mem_k_prompt, mem_v_prompt = (kv.reshape(depth, batch, n_mem, X_W) for kv in _mem_kv(mem2d, mem_gains, w_kv))

    y_prompt, conv_p, delta_p, _ = _trunk(x_prompt.reshape(batch * seq, d_model), batch, seq, True,
                                          mem_k_prompt, mem_v_prompt, None, None, p)

    y_sample, conv_s, delta_s, v_s = _trunk(
        x_sample.reshape(dec_batch * dec_seq, d_model), dec_batch, dec_seq, False,
        cache_mem_k.reshape(depth, dec_batch, n_mem * X_HEADS, HEAD_DIM),
        cache_mem_v.reshape(depth, dec_batch, n_mem * X_HEADS, HEAD_DIM),
        state_conv, state_delta, p)

    return (y_prompt.reshape(batch, seq, d_model),
            y_sample.reshape(dec_batch, dec_seq, d_model),
            mem_k_prompt.reshape(depth, batch, n_mem, X_HEADS, HEAD_DIM),
            mem_v_prompt.reshape(depth, batch, n_mem, X_HEADS, HEAD_DIM),
            jnp.stack(delta_p), jnp.stack(conv_p), delta_s, jnp.stack(conv_s), jnp.stack(v_s))
```
